```python
import math
import jax, jax.numpy as jnp
from jax import lax
import numpy as np

D_MODEL = 1024
BATCH = 2
SEQ = 8192
DEPTH = 2

CHUNK = 64
D_FF = 2816
MIX_W = 256
N_BRANCH = 4
S5_GROUPS = 16
S5_GROUP_CH = MIX_W // S5_GROUPS
S5_STATE = 64
POOL_WINDOWS = (2, 4, 8, 16)
POOL_GROUP_CH = MIX_W // len(POOL_WINDOWS)
RW_HEAD = 64
RW_HEADS = MIX_W // RW_HEAD
RW_W_RANK = 64
RW_A_RANK = 64
RW_G_RANK = 128
RW_GN_EPS = 64e-5
CONV_W = 3
NORM_EPS = 1e-6
SPLIT_POINTS = tuple(MIX_W * i for i in range(1, 9))
IN_WIDTH = 8 * MIX_W + N_BRANCH * D_MODEL

kernel_name = 'hybrid_gated_s5_pool_rwkv7_shortconv_macaron'


def rmsnorm(x, g):
    x32 = x.astype(jnp.float32)
    y = x32 * lax.rsqrt(jnp.mean(x32 * x32, axis=-1, keepdims=True) + NORM_EPS)
    return (y * g.astype(jnp.float32)).astype(x.dtype)


def swiglu(h, w_gate, w_up, w_down):
    return (jax.nn.silu(h @ w_gate) * (h @ w_up)) @ w_down


def shift1(z):
    return jnp.pad(z, ((0, 0), (1, 0), (0, 0)))[:, :-1]


def cmul(ar, ai, br, bi):
    return ar * br - ai * bi, ar * bi + ai * br


def _linrec_combine(e1, e2):
    a1r, a1i, b1r, b1i = e1
    a2r, a2i, b2r, b2i = e2
    ar, ai = cmul(a2r, a2i, a1r, a1i)
    br, bi = cmul(a2r, a2i, b1r, b1i)
    return ar, ai, br + b2r, bi + b2i


def s5_mixer(u, lam_re, lam_im, log_dt, b_re, b_im, c_re, c_im, d, w_glu):
    f32 = jnp.float32
    bsz, s, _ = u.shape
    n_chunk = s // CHUNK
    u32 = u.astype(f32).reshape(bsz, n_chunk, CHUNK, S5_GROUPS, S5_GROUP_CH)
    lr = lam_re.astype(f32)
    li = lam_im.astype(f32)
    dt = jnp.exp(log_dt.astype(f32))[:, None]
    mag = jnp.exp(lr * dt)
    abar_re, abar_im = mag * jnp.cos(li * dt), mag * jnp.sin(li * dt)
    inv = 1.0 / (lr * lr + li * li)
    coef_re, coef_im = cmul(abar_re - 1.0, abar_im, lr * inv, -li * inv)
    bb_re, bb_im = cmul(coef_re[..., None], coef_im[..., None],
                        b_re.astype(f32), b_im.astype(f32))
    bu_re = jnp.einsum('bncgh,gph->bncgp', u32, bb_re)
    bu_im = jnp.einsum('bncgh,gph->bncgp', u32, bb_im)
    a_re = jnp.broadcast_to(abar_re, bu_re.shape)
    a_im = jnp.broadcast_to(abar_im, bu_im.shape)
    pw_re, pw_im, hl_re, hl_im = lax.associative_scan(
        _linrec_combine, (a_re, a_im, bu_re, bu_im), axis=2)
    _, _, he_re, he_im = lax.associative_scan(
        _linrec_combine,
        (pw_re[:, :, -1], pw_im[:, :, -1], hl_re[:, :, -1], hl_im[:, :, -1]), axis=1)
    pad = ((0, 0), (1, 0), (0, 0), (0, 0))
    prev_re = jnp.pad(he_re, pad)[:, :-1][:, :, None]
    prev_im = jnp.pad(he_im, pad)[:, :-1][:, :, None]
    cr, ci = cmul(pw_re, pw_im, prev_re, prev_im)
    h_re, h_im = hl_re + cr, hl_im + ci
    y = (jnp.einsum('bncgp,ghp->bncgh', h_re, c_re.astype(f32))
         - jnp.einsum('bncgp,ghp->bncgh', h_im, c_im.astype(f32)))
    y = y.reshape(bsz, s, MIX_W) + d.astype(f32) * u32.reshape(bsz, s, MIX_W)
    g = jax.nn.gelu(y)
    out = g * jax.nn.sigmoid(g @ w_glu.astype(f32))
    return out.astype(u.dtype)


def pool_mixer(u, w, scale):
    f32 = jnp.float32
    bsz, s, _ = u.shape
    u32 = u.astype(f32)
    cs = jnp.cumsum(u32, axis=1)
    t = jnp.arange(1, s + 1, dtype=f32)[None, :, None]
    outs = []
    for gi, win in enumerate(POOL_WINDOWS):
        sl = slice(gi * POOL_GROUP_CH, (gi + 1) * POOL_GROUP_CH)
        c = cs[..., sl]
        lagged = jnp.pad(c, ((0, 0), (win, 0), (0, 0)))[:, :s]
        mean = (c - lagged) / jnp.minimum(t, float(win))
        outs.append(mean - u32[..., sl])
    pooled = jnp.stack(outs, axis=2)
    y = jnp.einsum('bsgc,gcd->bsgd', pooled, w.astype(f32)).reshape(bsz, s, MIX_W)
    return (y * scale.astype(f32)).astype(u.dtype)


def rwkv7_mixer(h, r_p, k_p, v_p, mu_rkv, mu_wag, w0, w1, w2, a0, a1, a2,
                g1, g2, k_k, k_a, r_k, ln_w, ln_b):
    f32 = jnp.float32
    bsz, s, _ = r_p.shape
    r = r_p + (shift1(r_p) - r_p) * mu_rkv[0]
    k = k_p + (shift1(k_p) - k_p) * mu_rkv[1]
    v = v_p + (shift1(v_p) - v_p) * mu_rkv[2]
    hx = shift1(h) - h
    xw = h + hx * mu_wag[0]
    xa = h + hx * mu_wag[1]
    xg = h + hx * mu_wag[2]
    w_log = -jax.nn.softplus(-(w0 + jnp.tanh(xw @ w1) @ w2)) - 0.5
    decay = jnp.exp(-jnp.exp(w_log.astype(f32)))
    a = jax.nn.sigmoid((a0 + (xa @ a1) @ a2).astype(f32))
    g = jax.nn.sigmoid(xg @ g1) @ g2
    k32 = k.astype(f32)
    kk = (k32 * k_k.astype(f32)).reshape(bsz, s, RW_HEADS, RW_HEAD)
    kk = kk / jnp.maximum(jnp.sqrt(jnp.sum(kk * kk, -1, keepdims=True)), 1e-12)
    k32 = k32 * (1.0 + (a - 1.0) * k_a.astype(f32))

    def heads(z):
        return z.astype(f32).reshape(bsz, s, RW_HEADS, RW_HEAD)

    rh, kh, vh = heads(r), heads(k32), heads(v)
    kka = kk * heads(a)
    tm = lambda z: jnp.transpose(z, (1, 0, 2, 3))
    xs = (tm(rh), tm(heads(decay)), tm(kh), tm(vh), tm(kk), tm(kka))

    def step(state, inp):
        r_t, w_t, k_t, v_t, kk_t, b_t = inp
        sa = jnp.einsum('bhvk,bhk->bhv', state, -kk_t)
        state = (state * w_t[:, :, None, :] + sa[..., None] * b_t[:, :, None, :]
                 + v_t[..., None] * k_t[:, :, None, :])
        return state, jnp.einsum('bhvk,bhk->bhv', state, r_t)

    state0 = jnp.zeros((bsz, RW_HEADS, RW_HEAD, RW_HEAD), f32)
    _, o = lax.scan(step, state0, xs)
    o = jnp.transpose(o, (1, 0, 2, 3))
    mu = jnp.mean(o, -1, keepdims=True)
    var = jnp.mean(jnp.square(o - mu), -1, keepdims=True)
    o = ((o - mu) * lax.rsqrt(var + RW_GN_EPS)).reshape(bsz, s, MIX_W)
    o = o * ln_w.astype(f32) + ln_b.astype(f32)
    bonus = jnp.sum(rh * kh * r_k.astype(f32), -1, keepdims=True) * vh
    out = (o + bonus.reshape(bsz, s, MIX_W)) * g.astype(f32)
    return out.astype(h.dtype)


def short_conv_mixer(z_in, b_g, c_g, conv_w):
    z = c_g * z_in
    y = lax.conv_general_dilated(
        z, conv_w.astype(z.dtype)[:, None, :], window_strides=(1,),
        padding=((CONV_W - 1, 0),), dimension_numbers=('NWC', 'WIO', 'NWC'),
        feature_group_count=MIX_W)
    return b_g * y


def setup_inputs(seed: int = 0) -> dict:
    key = jax.random.key(seed)
    ks = iter(jax.random.split(key, 48))
    f32 = jnp.float32
    L, D, F = DEPTH, D_MODEL, D_FF
    G, P, H = S5_GROUPS, S5_STATE, S5_GROUP_CH

    def nrm(shape, scale):
        return jax.random.normal(next(ks), shape, f32) * scale

    def gain(shape):
        return 1.0 + nrm(shape, 0.02)

    def unif(shape, lo, hi):
        return jax.random.uniform(next(ks), shape, f32, lo, hi)

    inp = {}
    inp['x'] = nrm((BATCH, SEQ, D), 1.0)
    inp['ffn1_norm'] = gain((L, D))
    inp['ffn1_w_gate'] = nrm((L, D, F), D ** -0.5)
    inp['ffn1_w_up'] = nrm((L, D, F), D ** -0.5)
    inp['ffn1_w_down'] = nrm((L, F, D), F ** -0.5)
    inp['mix_norm'] = gain((L, D))
    inp['w_in'] = nrm((L, D, IN_WIDTH), D ** -0.5)
    inp['s5_lambda_re'] = -0.5 + nrm((L, G, P), 0.01)
    inp['s5_lambda_im'] = math.pi * jnp.arange(P, dtype=f32)[None, None, :] + nrm((L, G, P), 0.01)
    inp['s5_log_dt'] = unif((L, G), math.log(1e-3), math.log(1e-1))
    inp['s5_b_re'] = nrm((L, G, P, H), (2.0 * H) ** -0.5)
    inp['s5_b_im'] = nrm((L, G, P, H), (2.0 * H) ** -0.5)
    inp['s5_c_re'] = nrm((L, G, H, P), P ** -0.5)
    inp['s5_c_im'] = nrm((L, G, H, P), P ** -0.5)
    inp['s5_d'] = nrm((L, MIX_W), 1.0)
    inp['s5_w_glu'] = nrm((L, MIX_W, MIX_W), MIX_W ** -0.5)
    inp['pool_w'] = nrm((L, len(POOL_WINDOWS), POOL_GROUP_CH, POOL_GROUP_CH), POOL_GROUP_CH ** -0.5)
    inp['pool_scale'] = 1.0 + nrm((L, MIX_W), 0.1)
    inp['rwkv_mu_rkv'] = unif((L, 3, MIX_W), 0.0, 1.0)
    inp['rwkv_mu_wag'] = unif((L, 3, D), 0.0, 1.0)
    inp['rwkv_w0'] = jnp.linspace(-6.5, -1.5, MIX_W, dtype=f32)[None, :] + nrm((L, MIX_W), 0.1)
    inp['rwkv_w1'] = nrm((L, D, RW_W_RANK), D ** -0.5)
    inp['rwkv_w2'] = nrm((L, RW_W_RANK, MIX_W), 0.1 * RW_W_RANK ** -0.5)
    inp['rwkv_a0'] = nrm((L, MIX_W), 0.1)
    inp['rwkv_a1'] = nrm((L, D, RW_A_RANK), D ** -0.5)
    inp['rwkv_a2'] = nrm((L, RW_A_RANK, MIX_W), 0.5 * RW_A_RANK ** -0.5)
    inp['rwkv_g1'] = nrm((L, D, RW_G_RANK), D ** -0.5)
    inp['rwkv_g2'] = nrm((L, RW_G_RANK, MIX_W), RW_G_RANK ** -0.5)
    inp['rwkv_k_k'] = 0.85 + nrm((L, MIX_W), 0.02)
    inp['rwkv_k_a'] = gain((L, MIX_W))
    inp['rwkv_r_k'] = nrm((L, RW_HEADS, RW_HEAD), 0.1)
    inp['rwkv_ln_w'] = gain((L, MIX_W))
    inp['rwkv_ln_b'] = nrm((L, MIX_W), 0.02)
    inp['conv_w'] = nrm((L, CONV_W, MIX_W), CONV_W ** -0.5)
    inp['w_branch'] = nrm((L, N_BRANCH, MIX_W, D), MIX_W ** -0.5)
    inp['w_out'] = nrm((L, D, D), D ** -0.5)
    inp['ffn2_norm'] = gain((L, D))
    inp['ffn2_w_gate'] = nrm((L, D, F), D ** -0.5)
    inp['ffn2_w_up'] = nrm((L, D, F), D ** -0.5)
    inp['ffn2_w_down'] = nrm((L, F, D), F ** -0.5)
    inp['final_norm'] = gain((D,))
    return inp


def reference(x, ffn1_norm, ffn1_w_gate, ffn1_w_up, ffn1_w_down, mix_norm, w_in,
              s5_lambda_re, s5_lambda_im, s5_log_dt, s5_b_re, s5_b_im, s5_c_re, s5_c_im,
              s5_d, s5_w_glu, pool_w, pool_scale, rwkv_mu_rkv, rwkv_mu_wag, rwkv_w0,
              rwkv_w1, rwkv_w2, rwkv_a0, rwkv_a1, rwkv_a2, rwkv_g1, rwkv_g2, rwkv_k_k,
              rwkv_k_a, rwkv_r_k, rwkv_ln_w, rwkv_ln_b, conv_w, w_branch, w_out,
              ffn2_norm, ffn2_w_gate, ffn2_w_up, ffn2_w_down, final_norm):
    bsz, s, _ = x.shape
    for l in range(DEPTH):
        x = x + 0.5 * swiglu(rmsnorm(x, ffn1_norm[l]), ffn1_w_gate[l], ffn1_w_up[l], ffn1_w_down[l])
        h = rmsnorm(x, mix_norm[l])
        p = h @ w_in[l]
        u_a, u_b, r_p, k_p, v_p, z_in, b_g, c_g, gate_pre = jnp.split(p, SPLIT_POINTS, axis=-1)
        y_a = s5_mixer(u_a, s5_lambda_re[l], s5_lambda_im[l], s5_log_dt[l], s5_b_re[l],
                       s5_b_im[l], s5_c_re[l], s5_c_im[l], s5_d[l], s5_w_glu[l])
        y_b = pool_mixer(u_b, pool_w[l], pool_scale[l])
        y_c = rwkv7_mixer(h, r_p, k_p, v_p, rwkv_mu_rkv[l], rwkv_mu_wag[l], rwkv_w0[l],
                          rwkv_w1[l], rwkv_w2[l], rwkv_a0[l], rwkv_a1[l], rwkv_a2[l],
                          rwkv_g1[l], rwkv_g2[l], rwkv_k_k[l], rwkv_k_a[l], rwkv_r_k[l],
                          rwkv_ln_w[l], rwkv_ln_b[l])
        y_d = short_conv_mixer(z_in, b_g, c_g, conv_w[l])
        ys = jnp.stack([y_a, y_b, y_c, y_d], axis=2)
        branches = jnp.einsum('bsgc,gcd->bsgd', ys, w_branch[l])
        gates = jax.nn.sigmoid(gate_pre.reshape(bsz, s, N_BRANCH, D_MODEL))
        merged = jnp.sum(branches * gates, axis=2)
        x = x + merged @ w_out[l]
        x = x + 0.5 * swiglu(rmsnorm(x, ffn2_norm[l]), ffn2_w_gate[l], ffn2_w_up[l], ffn2_w_down[l])
    return rmsnorm(x, final_norm)
```

```python
import functools
import math

import jax
import jax.numpy as jnp
from jax import lax
from jax.experimental import pallas as pl
from jax.experimental.pallas import tpu as pltpu

F32 = jnp.float32
BF16 = jnp.bfloat16
HIGHEST = lax.Precision.HIGHEST

MIX_W = 256
N_BRANCH = 4
S5_GROUPS = 16
S5_STATE = 64
POOL_WINDOWS = (2, 4, 8, 16)
POOL_HALO = 16
RW_HEAD = 64
RW_HEADS = MIX_W // RW_HEAD
RW_CHUNK = 64
RW_GN_EPS = 64e-5
NORM_EPS = 1e-6
SUBLANES = 8
VMEM_LIMIT_BYTES = 56 * 1024 * 1024


def _params(n_parallel, n_arbitrary=0):
    return pltpu.CompilerParams(
        dimension_semantics=("parallel",) * n_parallel + ("arbitrary",) * n_arbitrary,
        vmem_limit_bytes=VMEM_LIMIT_BYTES)


def _const_spec(shape):
    zeros = (0,) * len(shape)
    return pl.BlockSpec(shape, lambda *_: zeros, pipeline_mode=pl.Buffered(1))


def _rmsnorm(x, g):
    return x * lax.rsqrt(jnp.mean(x * x, axis=-1, keepdims=True) + NORM_EPS) * g


def _mm(a, b):
    return jnp.dot(a.astype(BF16), b.astype(BF16), preferred_element_type=F32)


def _mm_nt(a, b):
    return lax.dot_general(a.astype(BF16), b.astype(BF16), (((1,), (1,)), ((), ())),
                           preferred_element_type=F32)


def _mm_tn(a, b):
    return lax.dot_general(a.astype(BF16), b.astype(BF16), (((0,), (0,)), ((), ())),
                           preferred_element_type=F32)


def _shift_rows(cur, prev_row):
    rolled = pltpu.roll(cur, 1, 0)
    row = lax.broadcasted_iota(jnp.int32, cur.shape, 0)
    return jnp.where(row == 0, prev_row, rolled)


def _softplus(z):
    return jnp.maximum(z, 0.0) + jnp.log1p(jnp.exp(-jnp.abs(z)))


def _ffn_body(x_ref, g_ref, wg_ref, wu_ref, wd_ref, *rest, f_chunk, final):
    if final:
        fg_ref, o_ref, acc_ref = rest
    else:
        o_ref, acc_ref = rest
    x = x_ref[...]
    h = _rmsnorm(x, g_ref[...]).astype(BF16)
    d_ff = wg_ref.shape[1]
    for j in range(d_ff // f_chunk):
        sl = slice(j * f_chunk, (j + 1) * f_chunk)
        gate = jnp.dot(h, wg_ref[:, sl], preferred_element_type=F32)
        up = jnp.dot(h, wu_ref[:, sl], preferred_element_type=F32)
        act = (gate * jax.nn.sigmoid(gate) * up).astype(BF16)
        contrib = jnp.dot(act, wd_ref[sl, :], preferred_element_type=F32)
        if j == 0:
            acc_ref[...] = contrib
        else:
            acc_ref[...] += contrib
    y = x + 0.5 * acc_ref[...]
    if final:
        y = _rmsnorm(y, fg_ref[...])
    o_ref[...] = y


def _ffn(x, norm_g, w_gate, w_up, w_down, final_g=None, *, tm=512, f_chunk=256):
    t, d = x.shape
    d_ff = w_gate.shape[1]
    tm = min(tm, t)
    final = final_g is not None
    in_specs = [
        pl.BlockSpec((tm, d), lambda i: (i, 0)),
        _const_spec((1, d)),
        _const_spec((d, d_ff)),
        _const_spec((d, d_ff)),
        _const_spec((d_ff, d)),
    ]
    args = [x, norm_g.reshape(1, d), w_gate, w_up, w_down]
    if final:
        in_specs.append(_const_spec((1, d)))
        args.append(final_g.reshape(1, d))
    return pl.pallas_call(
        functools.partial(_ffn_body, f_chunk=f_chunk, final=final),
        grid=(t // tm,),
        in_specs=in_specs,
        out_specs=pl.BlockSpec((tm, d), lambda i: (i, 0)),
        out_shape=jax.ShapeDtypeStruct((t, d), F32),
        scratch_shapes=[pltpu.VMEM((tm, d), F32)],
        compiler_params=_params(1),
        name="ffn_final" if final else "ffn",
    )(*args)


def _proj_body(x_ref, xprev_ref, g_ref, win_ref, mu_ref, w0_ref, w1_ref, w2_ref, a0_ref, a1_ref,
               a2_ref, g1_ref, g2_ref, p_ref, lora_ref, *, seq):
    i = pl.program_id(0)
    tm = x_ref.shape[0]
    g = g_ref[...]
    h = _rmsnorm(x_ref[...], g)
    at_start = (i * tm) % seq == 0
    h_prev = _rmsnorm(xprev_ref[SUBLANES - 1:SUBLANES, :], g)
    h_prev = jnp.where(at_start, 0.0, h_prev)
    hx = _shift_rows(h, h_prev) - h
    p_ref[...] = _mm(h, win_ref[...])
    xw = h + hx * mu_ref[0:1, :]
    xa = h + hx * mu_ref[1:2, :]
    xg = h + hx * mu_ref[2:3, :]
    w_pre = w0_ref[...] + _mm(jnp.tanh(_mm(xw, w1_ref[...])), w2_ref[...])
    w_log = -_softplus(-w_pre) - 0.5
    lora_ref[:, 0:MIX_W] = -jnp.exp(w_log)
    lora_ref[:, MIX_W:2 * MIX_W] = jax.nn.sigmoid(a0_ref[...] + _mm(_mm(xa, a1_ref[...]), a2_ref[...]))
    lora_ref[:, 2 * MIX_W:3 * MIX_W] = _mm(jax.nn.sigmoid(_mm(xg, g1_ref[...])), g2_ref[...])


def _proj(x, norm_g, w_in_small, mu_wag, w0, w1, w2, a0, a1, a2, g1, g2, *, seq, tm=512):
    t, d = x.shape
    tm = min(tm, seq)
    n_small = w_in_small.shape[1]
    blocks_per_tile = tm // SUBLANES
    in_specs = [
        pl.BlockSpec((tm, d), lambda i: (i, 0)),
        pl.BlockSpec((SUBLANES, d), lambda i: (jnp.maximum(i * blocks_per_tile - 1, 0), 0)),
        _const_spec((1, d)),
        _const_spec((d, n_small)),
        _const_spec(mu_wag.shape),
        _const_spec((1, MIX_W)),
        _const_spec(w1.shape),
        _const_spec(w2.shape),
        _const_spec((1, MIX_W)),
        _const_spec(a1.shape),
        _const_spec(a2.shape),
        _const_spec(g1.shape),
        _const_spec(g2.shape),
    ]
    return pl.pallas_call(
        functools.partial(_proj_body, seq=seq),
        grid=(t // tm,),
        in_specs=in_specs,
        out_specs=[pl.BlockSpec((tm, n_small), lambda i: (i, 0)),
                   pl.BlockSpec((tm, 3 * MIX_W), lambda i: (i, 0))],
        out_shape=[jax.ShapeDtypeStruct((t, n_small), F32),
                   jax.ShapeDtypeStruct((t, 3 * MIX_W), F32)],
        compiler_params=_params(1),
        name="proj",
    )(x, x, norm_g.reshape(1, d), w_in_small, mu_wag, w0.reshape(1, MIX_W), w1, w2,
      a0.reshape(1, MIX_W), a1, a2, g1, g2)


def _s5_body(u_ref, lre_ref, lim_ref, ldt_ref, bre_ref, bim_ref, cre_ref, cim_ref, d_ref, wglu_ref,
             o_ref, wre_s, wim_s, abar_s, hre_s, him_s, st_s, *, unroll):
    @pl.when(pl.program_id(1) == 0)
    def _():
        lr = lre_ref[...]
        li = lim_ref[...]
        dt = jnp.exp(ldt_ref[...])
        mag = jnp.exp(lr * dt)
        ar = mag * jnp.cos(li * dt)
        ai = mag * jnp.sin(li * dt)
        inv = 1.0 / (lr * lr + li * li)
        qr, qi = lr * inv, -li * inv
        coef_re = (ar - 1.0) * qr - ai * qi
        coef_im = (ar - 1.0) * qi + ai * qr
        wre_s[...] = coef_re * bre_ref[...] - coef_im * bim_ref[...]
        wim_s[...] = coef_re * bim_ref[...] + coef_im * bre_ref[...]
        abar_s[0:1, :] = ar
        abar_s[1:2, :] = ai
        st_s[...] = jnp.zeros_like(st_s)

    u = u_ref[...]
    hre_s[...] = _mm(u, wre_s[...])
    him_s[...] = _mm(u, wim_s[...])
    ar = abar_s[0:1, :]
    ai = abar_s[1:2, :]

    def step(t, carry):
        hr, hi = carry
        nr = ar * hr - ai * hi + hre_s[pl.ds(t, 1), :]
        ni = ar * hi + ai * hr + him_s[pl.ds(t, 1), :]
        hre_s[pl.ds(t, 1), :] = nr
        him_s[pl.ds(t, 1), :] = ni
        return nr, ni

    hr, hi = lax.fori_loop(0, u.shape[0], step, (st_s[0:1, :], st_s[1:2, :]), unroll=unroll)
    st_s[0:1, :] = hr
    st_s[1:2, :] = hi

    y = _mm(hre_s[...], cre_ref[...]) - _mm(him_s[...], cim_ref[...]) + d_ref[...] * u
    gl = jax.nn.gelu(y, approximate=True)
    o_ref[...] = gl * jax.nn.sigmoid(_mm(gl, wglu_ref[...]))


def _s5(p, lam_re, lam_im, log_dt, b_re, b_im, c_re, c_im, d_skip, w_glu, *, batch, seq, tl=512):
    n_state = S5_GROUPS * S5_STATE
    tl = min(tl, seq)
    steps = seq // tl
    eye = jnp.eye(S5_GROUPS, dtype=F32)

    def b_blockdiag(b):
        return jnp.einsum("gph,gk->ghkp", b, eye).reshape(MIX_W, n_state)

    def c_blockdiag(c):
        return jnp.einsum("ghp,gk->gpkh", c, eye).reshape(n_state, MIX_W)

    row = lambda v: v.reshape(1, -1)
    in_specs = [
        pl.BlockSpec((tl, MIX_W), lambda b, i: (b * steps + i, 0)),
        _const_spec((1, n_state)), _const_spec((1, n_state)), _const_spec((1, n_state)),
        _const_spec((MIX_W, n_state)), _const_spec((MIX_W, n_state)),
        _const_spec((n_state, MIX_W)), _const_spec((n_state, MIX_W)),
        _const_spec((1, MIX_W)), _const_spec((MIX_W, MIX_W)),
    ]
    return pl.pallas_call(
        functools.partial(_s5_body, unroll=8),
        grid=(batch, steps),
        in_specs=in_specs,
        out_specs=pl.BlockSpec((tl, MIX_W), lambda b, i: (b * steps + i, 0)),
        out_shape=jax.ShapeDtypeStruct((batch * seq, MIX_W), F32),
        scratch_shapes=[pltpu.VMEM((MIX_W, n_state), F32), pltpu.VMEM((MIX_W, n_state), F32),
                        pltpu.VMEM((2, n_state), F32),
                        pltpu.VMEM((tl, n_state), F32), pltpu.VMEM((tl, n_state), F32),
                        pltpu.VMEM((2, n_state), F32)],
        compiler_params=_params(1, 1),
        name="s5",
    )(p, row(lam_re), row(lam_im), row(jnp.broadcast_to(log_dt[:, None], (S5_GROUPS, S5_STATE))),
      b_blockdiag(b_re), b_blockdiag(b_im), c_blockdiag(c_re), c_blockdiag(c_im),
      row(d_skip), w_glu)


def _segment_ones(n, seg):
    r = lax.broadcasted_iota(jnp.int32, (n, n), 0) // seg
    c = lax.broadcasted_iota(jnp.int32, (n, n), 1) // seg
    return (r == c).astype(F32)


def _rwkv_body(rp_ref, kp_ref, vp_ref, rprev_ref, kprev_ref, vprev_ref, lora_ref, mu_ref, kk_ref,
               ka_ref, rk_ref, lnw_ref, lnb_ref, o_ref,
               rdec_s, adec_s, binv_s, kinv_s, bend_s, kend_s, v_s, pend_s, oraw_s, st_s):
    tl = rp_ref.shape[0]
    c = RW_CHUNK
    first = pl.program_id(1) == 0

    @pl.when(first)
    def _():
        st_s[...] = jnp.zeros_like(st_s)

    def mixed(cur_ref, prev_ref, mu):
        cur = cur_ref[...]
        prev_row = jnp.where(first, 0.0, prev_ref[SUBLANES - 1:SUBLANES, :])
        return cur + (_shift_rows(cur, prev_row) - cur) * mu

    r = mixed(rp_ref, rprev_ref, mu_ref[0:1, :])
    k = mixed(kp_ref, kprev_ref, mu_ref[1:2, :])
    v = mixed(vp_ref, vprev_ref, mu_ref[2:3, :])
    logw = lora_ref[:, 0:MIX_W]
    a = lora_ref[:, MIX_W:2 * MIX_W]
    head_ones = _segment_ones(MIX_W, RW_HEAD)
    kraw = k * kk_ref[...]
    kk = kraw / jnp.maximum(jnp.sqrt(_mm(kraw * kraw, head_ones)), 1e-12)
    kmod = k * (1.0 + (a - 1.0) * ka_ref[...])

    ti = lax.broadcasted_iota(jnp.int32, (tl, tl), 0)
    tj = lax.broadcasted_iota(jnp.int32, (tl, tl), 1)
    same_chunk = (ti // c) == (tj // c)
    cum = jnp.dot((same_chunk & (tj <= ti)).astype(F32), logw, precision=HIGHEST,
                  preferred_element_type=F32)
    tot = jnp.dot(same_chunk.astype(F32), logw, precision=HIGHEST, preferred_element_type=F32)
    dec = jnp.exp(cum)
    inv = jnp.exp(-cum)
    to_end = jnp.exp(tot - cum)
    rdec_s[...] = r * dec
    adec_s[...] = -kk * jnp.exp(cum - logw)
    binv_s[...] = kk * a * inv
    kinv_s[...] = kmod * inv
    bend_s[...] = kk * a * to_end
    kend_s[...] = kmod * to_end
    v_s[...] = v
    pend_s[...] = jnp.exp(tot)

    ci = lax.broadcasted_iota(jnp.int32, (c, c), 0)
    cj = lax.broadcasted_iota(jnp.int32, (c, c), 1)
    strict = cj < ci
    incl = cj <= ci
    eye_c = (ci == cj).astype(F32)
    hi_ = lax.broadcasted_iota(jnp.int32, (RW_HEAD, RW_HEAD), 0)
    hj_ = lax.broadcasted_iota(jnp.int32, (RW_HEAD, RW_HEAD), 1)
    eye_h = (hi_ == hj_).astype(F32)

    def chunk(n, carry):
        rows = pl.ds(pl.multiple_of(n * c, c), c)
        outs = []
        for h in range(RW_HEADS):
            ls = slice(h * RW_HEAD, (h + 1) * RW_HEAD)
            a_d = adec_s[rows, ls]
            r_d = rdec_s[rows, ls]
            b_i = binv_s[rows, ls]
            k_i = kinv_s[rows, ls]
            vv = v_s[rows, ls]
            g = _mm_nt(jnp.concatenate([a_d, r_d], axis=0), jnp.concatenate([b_i, k_i], axis=0))
            a_ab = jnp.where(strict, g[0:c, 0:c], 0.0)
            a_ak = jnp.where(strict, g[0:c, c:2 * c], 0.0)
            m_rb = jnp.where(incl, g[c:2 * c, 0:c], 0.0)
            m_rk = jnp.where(incl, g[c:2 * c, c:2 * c], 0.0)
            tinv = eye_c + a_ab
            pw = a_ab
            span = 1
            while span < c:
                pw = _mm(pw, pw)
                tinv = tinv + _mm(tinv, pw)
                span *= 2
            w1 = _mm(tinv, a_d)
            u0 = _mm(tinv, _mm(a_ak, vv))
            st = st_s[h]
            u = _mm(w1, st) + u0
            outs.append(_mm(r_d, st) + _mm(m_rb, u) + _mm(m_rk, vv))
            b_e = bend_s[rows, ls]
            k_e = kend_s[rows, ls]
            p_end = pend_s[pl.ds(n * c, 1), ls]
            trans = eye_h * p_end + _mm_tn(b_e, w1)
            st_s[h] = _mm(trans, st) + _mm_tn(b_e, u0) + _mm_tn(k_e, vv)
        oraw_s[rows, :] = jnp.concatenate(outs, axis=1)
        return carry

    lax.fori_loop(0, tl // c, chunk, 0)

    o = oraw_s[...]
    inv_n = 1.0 / RW_HEAD
    mean = _mm(o, head_ones) * inv_n
    dlt = o - mean
    var = _mm(dlt * dlt, head_ones) * inv_n
    o_n = dlt * lax.rsqrt(var + RW_GN_EPS) * lnw_ref[...] + lnb_ref[...]
    bonus = _mm(r * kmod * rk_ref[...], head_ones) * v
    o_ref[...] = (o_n + bonus) * lora_ref[:, 2 * MIX_W:3 * MIX_W]


def _rwkv(p, lora, mu_rkv, k_k, k_a, r_k, ln_w, ln_b, *, batch, seq, tl=256):
    tl = min(tl, seq)
    steps = seq // tl
    blocks_per_tile = tl // SUBLANES
    row = lambda v: v.reshape(1, MIX_W)

    def cur(col):
        return pl.BlockSpec((tl, MIX_W), lambda b, i: (b * steps + i, col))

    def prev(col):
        return pl.BlockSpec(
            (SUBLANES, MIX_W),
            lambda b, i: (jnp.maximum((b * steps + i) * blocks_per_tile - 1, 0), col))

    in_specs = [cur(2), cur(3), cur(4), prev(2), prev(3), prev(4),
                pl.BlockSpec((tl, 3 * MIX_W), lambda b, i: (b * steps + i, 0)),
                _const_spec((3, MIX_W))] + [_const_spec((1, MIX_W))] * 5
    tile = pltpu.VMEM((tl, MIX_W), F32)
    return pl.pallas_call(
        _rwkv_body,
        grid=(batch, steps),
        in_specs=in_specs,
        out_specs=pl.BlockSpec((tl, MIX_W), lambda b, i: (b * steps + i, 0)),
        out_shape=jax.ShapeDtypeStruct((batch * seq, MIX_W), F32),
        scratch_shapes=[tile] * 9 + [pltpu.VMEM((RW_HEADS, RW_HEAD, RW_HEAD), F32)],
        compiler_params=_params(1, 1),
        name="rwkv",
    )(p, p, p, p, p, p, lora, mu_rkv, row(k_k), row(k_a), row(r_k), row(ln_w), row(ln_b))


def _merge_body(x_ref, ub_ref, ubprev_ref, z_ref, zprev_ref, bg_ref, cg_ref, cgprev_ref, ya_ref,
                yc_ref, g_ref, wgate_ref, wbr_ref, wout_ref, poolw_ref, pscale_ref, convw_ref, o_ref,
                *, seq):
    i = pl.program_id(0)
    tm, d = x_ref.shape
    at_start = (i * tm) % seq == 0
    x = x_ref[...]
    h = _rmsnorm(x, g_ref[...]).astype(BF16)

    u = ub_ref[...]
    ext = jnp.concatenate([jnp.where(at_start, 0.0, ubprev_ref[...]), u], axis=0)
    sums = {1: ext}
    span = 1
    while span < POOL_WINDOWS[-1]:
        sums[2 * span] = sums[span] + pltpu.roll(sums[span], span, 0)
        span *= 2
    lane = lax.broadcasted_iota(jnp.int32, (tm, MIX_W), 1)
    group = lane // (MIX_W // len(POOL_WINDOWS))
    win_sum = sums[POOL_WINDOWS[-1]][POOL_HALO:, :]
    win = jnp.full((tm, MIX_W), float(POOL_WINDOWS[-1]), F32)
    for gi in range(len(POOL_WINDOWS) - 2, -1, -1):
        win_sum = jnp.where(group == gi, sums[POOL_WINDOWS[gi]][POOL_HALO:, :], win_sum)
        win = jnp.where(group == gi, float(POOL_WINDOWS[gi]), win)
    pos = (i * tm) % seq + lax.broadcasted_iota(jnp.int32, (tm, MIX_W), 0)
    count = jnp.minimum((pos + 1).astype(F32), win)
    y_pool = _mm(win_sum / count - u, poolw_ref[...]) * pscale_ref[...]

    z = cg_ref[...] * z_ref[...]
    z_prev = jnp.where(at_start, 0.0, cgprev_ref[...] * zprev_ref[...])
    zext = jnp.concatenate([z_prev, z], axis=0)
    conv = (convw_ref[2:3, :] * zext
            + convw_ref[1:2, :] * pltpu.roll(zext, 1, 0)
            + convw_ref[0:1, :] * pltpu.roll(zext, 2, 0))
    y_conv = bg_ref[...] * conv[SUBLANES:, :]

    merged = None
    for gi, y in enumerate((ya_ref[...], y_pool, yc_ref[...], y_conv)):
        gate = jax.nn.sigmoid(jnp.dot(h, wgate_ref[:, gi * d:(gi + 1) * d], preferred_element_type=F32))
        term = _mm(y, wbr_ref[gi]) * gate
        merged = term if merged is None else merged + term
    o_ref[...] = x + _mm(merged, wout_ref[...])


def _merge(x, p, y_s5, y_rwkv, norm_g, w_gates, w_branch, w_out, pool_w, pool_scale, conv_w, *,
           seq, tm=512):
    t, d = x.shape
    tm = min(tm, seq)
    pool_groups = len(POOL_WINDOWS)
    eye = jnp.eye(pool_groups, dtype=F32)
    pool_bd = jnp.einsum("gcd,gk->gckd", pool_w, eye).reshape(MIX_W, MIX_W)

    def cur(col):
        return pl.BlockSpec((tm, MIX_W), lambda i: (i, col))

    def prev(col, rows):
        per_tile = tm // rows
        return pl.BlockSpec((rows, MIX_W), lambda i: (jnp.maximum(i * per_tile - 1, 0), col))

    in_specs = [
        pl.BlockSpec((tm, d), lambda i: (i, 0)),
        cur(1), prev(1, POOL_HALO), cur(5), prev(5, SUBLANES), cur(6), cur(7), prev(7, SUBLANES),
        pl.BlockSpec((tm, MIX_W), lambda i: (i, 0)),
        pl.BlockSpec((tm, MIX_W), lambda i: (i, 0)),
        _const_spec((1, d)),
        _const_spec(w_gates.shape), _const_spec(w_branch.shape), _const_spec(w_out.shape),
        _const_spec((MIX_W, MIX_W)), _const_spec((1, MIX_W)), _const_spec(conv_w.shape),
    ]
    return pl.pallas_call(
        functools.partial(_merge_body, seq=seq),
        grid=(t // tm,),
        in_specs=in_specs,
        out_specs=pl.BlockSpec((tm, d), lambda i: (i, 0)),
        out_shape=jax.ShapeDtypeStruct((t, d), F32),
        compiler_params=_params(1),
        name="merge",
    )(x, p, p, p, p, p, p, p, y_s5, y_rwkv, norm_g.reshape(1, d), w_gates, w_branch, w_out,
      pool_bd, pool_scale.reshape(1, MIX_W), conv_w)


def kernel(x, ffn1_norm, ffn1_w_gate, ffn1_w_up, ffn1_w_down, mix_norm, w_in, s5_lambda_re, s5_lambda_im, s5_log_dt, s5_b_re, s5_b_im, s5_c_re, s5_c_im, s5_d, s5_w_glu, pool_w, pool_scale, rwkv_mu_rkv, rwkv_mu_wag, rwkv_w0, rwkv_w1, rwkv_w2, rwkv_a0, rwkv_a1, rwkv_a2, rwkv_g1, rwkv_g2, rwkv_k_k, rwkv_k_a, rwkv_r_k, rwkv_ln_w, rwkv_ln_b, conv_w, w_branch, w_out, ffn2_norm, ffn2_w_gate, ffn2_w_up, ffn2_w_down, final_norm):
    batch, seq, d = x.shape
    depth = w_in.shape[0]
    n_small = 8 * MIX_W
    bf = lambda w: w.astype(BF16)
    xf = x.reshape(batch * seq, d)
    for l in range(depth):
        xf = _ffn(xf, ffn1_norm[l], bf(ffn1_w_gate[l]), bf(ffn1_w_up[l]), bf(ffn1_w_down[l]))
        p, lora = _proj(xf, mix_norm[l], bf(w_in[l, :, :n_small]), rwkv_mu_wag[l], rwkv_w0[l],
                        bf(rwkv_w1[l]), rwkv_w2[l], rwkv_a0[l], bf(rwkv_a1[l]), rwkv_a2[l],
                        bf(rwkv_g1[l]), rwkv_g2[l], seq=seq)
        y_s5 = _s5(p, s5_lambda_re[l], s5_lambda_im[l], s5_log_dt[l], s5_b_re[l], s5_b_im[l],
                   s5_c_re[l], s5_c_im[l], s5_d[l], s5_w_glu[l], batch=batch, seq=seq)
        y_rwkv = _rwkv(p, lora, rwkv_mu_rkv[l], rwkv_k_k[l], rwkv_k_a[l], rwkv_r_k[l].reshape(-1),
                       rwkv_ln_w[l], rwkv_ln_b[l], batch=batch, seq=seq)
        xf = _merge(xf, p, y_s5, y_rwkv, mix_norm[l], bf(w_in[l, :, n_small:]), bf(w_branch[l]),
                    bf(w_out[l]), pool_w[l], pool_scale[l], conv_w[l], seq=seq)
        xf = _ffn(xf, ffn2_norm[l], bf(ffn2_w_gate[l]), bf(ffn2_w_up[l]), bf(ffn2_w_down[l]),
                  final_norm if l == depth - 1 else None)
    return xf.reshape(batch, seq, d)
```

```python
import functools
import math

import jax
import jax.numpy as jnp
from jax import lax
from jax.experimental import pallas as pl
from jax.experimental.pallas import tpu as pltpu

F32 = jnp.float32
BF16 = jnp.bfloat16
HIGHEST = lax.Precision.HIGHEST

MIX_W = 256
N_BRANCH = 4
S5_GROUPS = 16
S5_STATE = 64
POOL_WINDOWS = (2, 4, 8, 16)
POOL_HALO = 16
RW_HEAD = 64
RW_HEADS = MIX_W // RW_HEAD
RW_CHUNK = 64
RW_GN_EPS = 64e-5
NORM_EPS = 1e-6
SUBLANES = 8
VMEM_LIMIT_BYTES = 56 * 1024 * 1024


def _params(n_parallel, n_arbitrary=0):
    return pltpu.CompilerParams(
        dimension_semantics=("parallel",) * n_parallel + ("arbitrary",) * n_arbitrary,
        vmem_limit_bytes=VMEM_LIMIT_BYTES)


def _const_spec(shape):
    zeros = (0,) * len(shape)
    return pl.BlockSpec(shape, lambda *_: zeros, pipeline_mode=pl.Buffered(1))


def _rmsnorm(x, g):
    return x * lax.rsqrt(jnp.mean(x * x, axis=-1, keepdims=True) + NORM_EPS) * g


def _mm(a, b):
    return jnp.dot(a.astype(BF16), b.astype(BF16), preferred_element_type=F32)


def _mm_nt(a, b):
    return lax.dot_general(a.astype(BF16), b.astype(BF16), (((1,), (1,)), ((), ())),
                           preferred_element_type=F32)


def _mm_tn(a, b):
    return lax.dot_general(a.astype(BF16), b.astype(BF16), (((0,), (0,)), ((), ())),
                           preferred_element_type=F32)


def _shift_rows(cur, prev_row):
    rolled = pltpu.roll(cur, 1, 0)
    row = lax.broadcasted_iota(jnp.int32, cur.shape, 0)
    return jnp.where(row == 0, prev_row, rolled)


def _softplus(z):
    return jnp.maximum(z, 0.0) + jnp.log1p(jnp.exp(-jnp.abs(z)))


def _ffn_body(x_ref, g_ref, wg_ref, wu_ref, wd_ref, *rest, f_chunk, final):
    if final:
        fg_ref, o_ref, acc_ref = rest
    else:
        o_ref, acc_ref = rest
    x = x_ref[...]
    h = _rmsnorm(x, g_ref[...]).astype(BF16)
    d_ff = wg_ref.shape[1]
    for j in range(d_ff // f_chunk):
        sl = slice(j * f_chunk, (j + 1) * f_chunk)
        gate = jnp.dot(h, wg_ref[:, sl], preferred_element_type=F32)
        up = jnp.dot(h, wu_ref[:, sl], preferred_element_type=F32)
        act = (gate * jax.nn.sigmoid(gate) * up).astype(BF16)
        contrib = jnp.dot(act, wd_ref[sl, :], preferred_element_type=F32)
        if j == 0:
            acc_ref[...] = contrib
        else:
            acc_ref[...] += contrib
    y = x + 0.5 * acc_ref[...]
    if final:
        y = _rmsnorm(y, fg_ref[...])
    o_ref[...] = y


def _ffn(x, norm_g, w_gate, w_up, w_down, final_g=None, *, tm=512, f_chunk=256):
    t, d = x.shape
    d_ff = w_gate.shape[1]
    tm = min(tm, t)
    final = final_g is not None
    in_specs = [
        pl.BlockSpec((tm, d), lambda i: (i, 0)),
        _const_spec((1, d)),
        _const_spec((d, d_ff)),
        _const_spec((d, d_ff)),
        _const_spec((d_ff, d)),
    ]
    args = [x, norm_g.reshape(1, d), w_gate, w_up, w_down]
    if final:
        in_specs.append(_const_spec((1, d)))
        args.append(final_g.reshape(1, d))
    return pl.pallas_call(
        functools.partial(_ffn_body, f_chunk=f_chunk, final=final),
        grid=(t // tm,),
        in_specs=in_specs,
        out_specs=pl.BlockSpec((tm, d), lambda i: (i, 0)),
        out_shape=jax.ShapeDtypeStruct((t, d), F32),
        scratch_shapes=[pltpu.VMEM((tm, d), F32)],
        compiler_params=_params(1),
        name="ffn_final" if final else "ffn",
    )(*args)


def _proj_body(x_ref, xprev_ref, g_ref, win_ref, mu_ref, w0_ref, w1_ref, w2_ref, a0_ref, a1_ref,
               a2_ref, g1_ref, g2_ref, p_ref, lora_ref, *, seq):
    i = pl.program_id(0)
    tm = x_ref.shape[0]
    g = g_ref[...]
    h = _rmsnorm(x_ref[...], g)
    at_start = (i * tm) % seq == 0
    h_prev = _rmsnorm(xprev_ref[SUBLANES - 1:SUBLANES, :], g)
    h_prev = jnp.where(at_start, 0.0, h_prev)
    hx = _shift_rows(h, h_prev) - h
    p_ref[...] = _mm(h, win_ref[...])
    xw = h + hx * mu_ref[0:1, :]
    xa = h + hx * mu_ref[1:2, :]
    xg = h + hx * mu_ref[2:3, :]
    w_pre = w0_ref[...] + _mm(jnp.tanh(_mm(xw, w1_ref[...])), w2_ref[...])
    w_log = -_softplus(-w_pre) - 0.5
    lora_ref[:, 0:MIX_W] = -jnp.exp(w_log)
    lora_ref[:, MIX_W:2 * MIX_W] = jax.nn.sigmoid(a0_ref[...] + _mm(_mm(xa, a1_ref[...]), a2_ref[...]))
    lora_ref[:, 2 * MIX_W:3 * MIX_W] = _mm(jax.nn.sigmoid(_mm(xg, g1_ref[...])), g2_ref[...])


def _proj(x, norm_g, w_in_small, mu_wag, w0, w1, w2, a0, a1, a2, g1, g2, *, seq, tm=512):
    t, d = x.shape
    tm = min(tm, seq)
    n_small = w_in_small.shape[1]
    blocks_per_tile = tm // SUBLANES
    in_specs = [
        pl.BlockSpec((tm, d), lambda i: (i, 0)),
        pl.BlockSpec((SUBLANES, d), lambda i: (jnp.maximum(i * blocks_per_tile - 1, 0), 0)),
        _const_spec((1, d)),
        _const_spec((d, n_small)),
        _const_spec(mu_wag.shape),
        _const_spec((1, MIX_W)),
        _const_spec(w1.shape),
        _const_spec(w2.shape),
        _const_spec((1, MIX_W)),
        _const_spec(a1.shape),
        _const_spec(a2.shape),
        _const_spec(g1.shape),
        _const_spec(g2.shape),
    ]
    return pl.pallas_call(
        functools.partial(_proj_body, seq=seq),
        grid=(t // tm,),
        in_specs=in_specs,
        out_specs=[pl.BlockSpec((tm, n_small), lambda i: (i, 0)),
                   pl.BlockSpec((tm, 3 * MIX_W), lambda i: (i, 0))],
        out_shape=[jax.ShapeDtypeStruct((t, n_small), F32),
                   jax.ShapeDtypeStruct((t, 3 * MIX_W), F32)],
        compiler_params=_params(1),
        name="proj",
    )(x, x, norm_g.reshape(1, d), w_in_small, mu_wag, w0.reshape(1, MIX_W), w1, w2,
      a0.reshape(1, MIX_W), a1, a2, g1, g2)


def _s5_body(u_ref, lre_ref, lim_ref, ldt_ref, bre_ref, bim_ref, cre_ref, cim_ref, d_ref, wglu_ref,
             o_ref, wre_s, wim_s, abar_s, hre_s, him_s, st_s, *, unroll):
    @pl.when(pl.program_id(1) == 0)
    def _():
        lr = lre_ref[...]
        li = lim_ref[...]
        dt = jnp.exp(ldt_ref[...])
        mag = jnp.exp(lr * dt)
        ar = mag * jnp.cos(li * dt)
        ai = mag * jnp.sin(li * dt)
        inv = 1.0 / (lr * lr + li * li)
        qr, qi = lr * inv, -li * inv
        coef_re = (ar - 1.0) * qr - ai * qi
        coef_im = (ar - 1.0) * qi + ai * qr
        wre_s[...] = coef_re * bre_ref[...] - coef_im * bim_ref[...]
        wim_s[...] = coef_re * bim_ref[...] + coef_im * bre_ref[...]
        abar_s[0:1, :] = ar
        abar_s[1:2, :] = ai
        st_s[...] = jnp.zeros_like(st_s)

    u = u_ref[...]
    hre_s[...] = _mm(u, wre_s[...])
    him_s[...] = _mm(u, wim_s[...])
    ar = abar_s[0:1, :]
    ai = abar_s[1:2, :]

    def step(t, carry):
        hr, hi = carry
        nr = ar * hr - ai * hi + hre_s[pl.ds(t, 1), :]
        ni = ar * hi + ai * hr + him_s[pl.ds(t, 1), :]
        hre_s[pl.ds(t, 1), :] = nr
        him_s[pl.ds(t, 1), :] = ni
        return nr, ni

    hr, hi = lax.fori_loop(0, u.shape[0], step, (st_s[0:1, :], st_s[1:2, :]), unroll=unroll)
    st_s[0:1, :] = hr
    st_s[1:2, :] = hi

    y = _mm(hre_s[...], cre_ref[...]) - _mm(him_s[...], cim_ref[...]) + d_ref[...] * u
    gl = jax.nn.gelu(y, approximate=True)
    o_ref[...] = gl * jax.nn.sigmoid(_mm(gl, wglu_ref[...]))


def _s5(p, lam_re, lam_im, log_dt, b_re, b_im, c_re, c_im, d_skip, w_glu, *, batch, seq, tl=512):
    n_state = S5_GROUPS * S5_STATE
    tl = min(tl, seq)
    steps = seq // tl
    eye = jnp.eye(S5_GROUPS, dtype=F32)

    def b_blockdiag(b):
        return jnp.einsum("gph,gk->ghkp", b, eye).reshape(MIX_W, n_state)

    def c_blockdiag(c):
        return jnp.einsum("ghp,gk->gpkh", c, eye).reshape(n_state, MIX_W)

    row = lambda v: v.reshape(1, -1)
    in_specs = [
        pl.BlockSpec((tl, MIX_W), lambda b, i: (b * steps + i, 0)),
        _const_spec((1, n_state)), _const_spec((1, n_state)), _const_spec((1, n_state)),
        _const_spec((MIX_W, n_state)), _const_spec((MIX_W, n_state)),
        _const_spec((n_state, MIX_W)), _const_spec((n_state, MIX_W)),
        _const_spec((1, MIX_W)), _const_spec((MIX_W, MIX_W)),
    ]
    return pl.pallas_call(
        functools.partial(_s5_body, unroll=8),
        grid=(batch, steps),
        in_specs=in_specs,
        out_specs=pl.BlockSpec((tl, MIX_W), lambda b, i: (b * steps + i, 0)),
        out_shape=jax.ShapeDtypeStruct((batch * seq, MIX_W), F32),
        scratch_shapes=[pltpu.VMEM((MIX_W, n_state), F32), pltpu.VMEM((MIX_W, n_state), F32),
                        pltpu.VMEM((2, n_state), F32),
                        pltpu.VMEM((tl, n_state), F32), pltpu.VMEM((tl, n_state), F32),
                        pltpu.VMEM((2, n_state), F32)],
        compiler_params=_params(1, 1),
        name="s5",
    )(p, row(lam_re), row(lam_im), row(jnp.broadcast_to(log_dt[:, None], (S5_GROUPS, S5_STATE))),
      b_blockdiag(b_re), b_blockdiag(b_im), c_blockdiag(c_re), c_blockdiag(c_im),
      row(d_skip), w_glu)


def _segment_ones(n, seg):
    r = lax.broadcasted_iota(jnp.int32, (n, n), 0) // seg
    c = lax.broadcasted_iota(jnp.int32, (n, n), 1) // seg
    return r == c


def _rwkv_body(rp_ref, kp_ref, vp_ref, rprev_ref, kprev_ref, vprev_ref, lora_ref, mu_ref, kk_ref,
               ka_ref, rk_ref, lnw_ref, lnb_ref, o_ref, st_s):
    tl = rp_ref.shape[0]
    c = RW_CHUNK
    first = pl.program_id(1) == 0

    @pl.when(first)
    def _():
        st_s[...] = jnp.zeros_like(st_s)

    def mixed(cur_ref, prev_ref, mu):
        cur = cur_ref[...]
        prev_row = jnp.where(first, 0.0, prev_ref[SUBLANES - 1:SUBLANES, :])
        return cur + (_shift_rows(cur, prev_row) - cur) * mu

    r = mixed(rp_ref, rprev_ref, mu_ref[0:1, :])
    k = mixed(kp_ref, kprev_ref, mu_ref[1:2, :])
    v = mixed(vp_ref, vprev_ref, mu_ref[2:3, :])
    logw = lora_ref[:, 0:MIX_W]
    a = lora_ref[:, MIX_W:2 * MIX_W]
    head_ones = _segment_ones(MIX_W, RW_HEAD).astype(BF16)
    kraw = k * kk_ref[...]
    kk = kraw / jnp.maximum(jnp.sqrt(_mm(kraw * kraw, head_ones)), 1e-12)
    kmod = k * (1.0 + (a - 1.0) * ka_ref[...])

    ti = lax.broadcasted_iota(jnp.int32, (tl, tl), 0)
    tj = lax.broadcasted_iota(jnp.int32, (tl, tl), 1)
    same_chunk = (ti // c) == (tj // c)
    cum = jnp.dot((same_chunk & (tj <= ti)).astype(F32), logw, precision=HIGHEST,
                  preferred_element_type=F32)
    tot = jnp.dot(same_chunk.astype(F32), logw, precision=HIGHEST, preferred_element_type=F32)
    inv = jnp.exp(-cum)
    to_end = jnp.exp(tot - cum)
    r_dec = (r * jnp.exp(cum)).astype(BF16)
    a_dec = (-kk * jnp.exp(cum - logw)).astype(BF16)
    b_inv = (kk * a * inv).astype(BF16)
    k_inv = (kmod * inv).astype(BF16)
    b_end = (kk * a * to_end).astype(BF16)
    k_end = (kmod * to_end).astype(BF16)
    v_b = v.astype(BF16)
    p_end = jnp.exp(tot)

    ci = lax.broadcasted_iota(jnp.int32, (c, c), 0)
    cj = lax.broadcasted_iota(jnp.int32, (c, c), 1)
    strict = cj < ci
    incl = cj <= ci
    eye_c = (ci == cj).astype(F32)
    eye_h = (lax.broadcasted_iota(jnp.int32, (RW_HEAD, RW_HEAD), 0)
             == lax.broadcasted_iota(jnp.int32, (RW_HEAD, RW_HEAD), 1)).astype(F32)
    bdot = functools.partial(jnp.dot, preferred_element_type=F32)

    n_chunks = tl // c
    probs = [(h, n) for h in range(RW_HEADS) for n in range(n_chunks)]
    n_probs = len(probs)

    def piece(x, h, n):
        return x[n * c:(n + 1) * c, h * RW_HEAD:(h + 1) * RW_HEAD]

    a_d = [piece(a_dec, h, n) for h, n in probs]
    r_d = [piece(r_dec, h, n) for h, n in probs]
    vv = [piece(v_b, h, n) for h, n in probs]
    g = [_mm_nt(jnp.concatenate([a_d[i], r_d[i]], axis=0),
                jnp.concatenate([piece(b_inv, h, n), piece(k_inv, h, n)], axis=0))
         for i, (h, n) in enumerate(probs)]
    a_ab = [jnp.where(strict, x[0:c, 0:c], 0.0) for x in g]
    a_ak = [jnp.where(strict, x[0:c, c:2 * c], 0.0).astype(BF16) for x in g]
    lhs_o = [jnp.concatenate([r_d[i], jnp.where(incl, x[c:2 * c, 0:c], 0.0).astype(BF16),
                              jnp.where(incl, x[c:2 * c, c:2 * c], 0.0).astype(BF16)], axis=1)
             for i, x in enumerate(g)]
    tinv = [eye_c + x for x in a_ab]
    pw = [x.astype(BF16) for x in a_ab]
    pw = [bdot(x, x).astype(BF16) for x in pw]
    akv = [bdot(a_ak[i], vv[i]).astype(BF16) for i in range(n_probs)]
    n_steps = int(math.log2(c)) - 1
    for s in range(n_steps):
        if s + 1 < n_steps:
            both = [bdot(jnp.concatenate([tinv[i].astype(BF16), pw[i]], axis=0), pw[i])
                    for i in range(n_probs)]
            tinv = [tinv[i] + both[i][0:c] for i in range(n_probs)]
            pw = [x[c:2 * c].astype(BF16) for x in both]
        else:
            tinv = [tinv[i] + bdot(tinv[i].astype(BF16), pw[i]) for i in range(n_probs)]
    w1u0 = [bdot(tinv[i].astype(BF16), jnp.concatenate([a_d[i], akv[i]], axis=1))
            for i in range(n_probs)]
    tn = [_mm_tn(piece(b_end, h, n), w1u0[i]) for i, (h, n) in enumerate(probs)]
    kv = [_mm_tn(piece(k_end, h, n), vv[i]) for i, (h, n) in enumerate(probs)]
    trans = [(tn[i][:, 0:RW_HEAD]
              + eye_h * p_end[n * c:n * c + 1, h * RW_HEAD:(h + 1) * RW_HEAD]).astype(BF16)
             for i, (h, n) in enumerate(probs)]
    add = [tn[i][:, RW_HEAD:2 * RW_HEAD] + kv[i] for i in range(n_probs)]
    w1 = [x[:, 0:RW_HEAD].astype(BF16) for x in w1u0]
    u0 = [x[:, RW_HEAD:2 * RW_HEAD] for x in w1u0]

    st = [st_s[h] for h in range(RW_HEADS)]
    st_at = {}
    for n in range(n_chunks):
        for h in range(RW_HEADS):
            i = h * n_chunks + n
            st_b = st[h].astype(BF16)
            st_at[i] = st_b
            st[h] = bdot(trans[i], st_b) + add[i]
    for h in range(RW_HEADS):
        st_s[h] = st[h]
    u = [(bdot(w1[i], st_at[i]) + u0[i]).astype(BF16) for i in range(n_probs)]
    o_p = [bdot(lhs_o[i], jnp.concatenate([st_at[i], u[i], vv[i]], axis=0)) for i in range(n_probs)]
    o = jnp.concatenate(
        [jnp.concatenate(o_p[h * n_chunks:(h + 1) * n_chunks], axis=0) for h in range(RW_HEADS)], axis=1)

    inv_n = 1.0 / RW_HEAD
    mean = _mm(o, head_ones) * inv_n
    dlt = o - mean
    var = _mm(dlt * dlt, head_ones) * inv_n
    o_n = dlt * lax.rsqrt(var + RW_GN_EPS) * lnw_ref[...] + lnb_ref[...]
    bonus = _mm(r * kmod * rk_ref[...], head_ones) * v
    o_ref[...] = (o_n + bonus) * lora_ref[:, 2 * MIX_W:3 * MIX_W]


def _rwkv(p, lora, mu_rkv, k_k, k_a, r_k, ln_w, ln_b, *, batch, seq, tl=256):
    tl = min(tl, seq)
    steps = seq // tl
    blocks_per_tile = tl // SUBLANES
    row = lambda v: v.reshape(1, MIX_W)

    def cur(col):
        return pl.BlockSpec((tl, MIX_W), lambda b, i: (b * steps + i, col))

    def prev(col):
        return pl.BlockSpec(
            (SUBLANES, MIX_W),
            lambda b, i: (jnp.maximum((b * steps + i) * blocks_per_tile - 1, 0), col))

    in_specs = [cur(2), cur(3), cur(4), prev(2), prev(3), prev(4),
                pl.BlockSpec((tl, 3 * MIX_W), lambda b, i: (b * steps + i, 0)),
                _const_spec((3, MIX_W))] + [_const_spec((1, MIX_W))] * 5
    return pl.pallas_call(
        _rwkv_body,
        grid=(batch, steps),
        in_specs=in_specs,
        out_specs=pl.BlockSpec((tl, MIX_W), lambda b, i: (b * steps + i, 0)),
        out_shape=jax.ShapeDtypeStruct((batch * seq, MIX_W), F32),
        scratch_shapes=[pltpu.VMEM((RW_HEADS, RW_HEAD, RW_HEAD), F32)],
        compiler_params=_params(1, 1),
        name="rwkv",
    )(p, p, p, p, p, p, lora, mu_rkv, row(k_k), row(k_a), row(r_k), row(ln_w), row(ln_b))


def _merge_body(x_ref, ub_ref, ubprev_ref, z_ref, zprev_ref, bg_ref, cg_ref, cgprev_ref, ya_ref,
                yc_ref, g_ref, wgate_ref, wbr_ref, wout_ref, poolw_ref, pscale_ref, convw_ref, o_ref,
                *, seq):
    i = pl.program_id(0)
    tm, d = x_ref.shape
    at_start = (i * tm) % seq == 0
    x = x_ref[...]
    h = _rmsnorm(x, g_ref[...]).astype(BF16)

    u = ub_ref[...]
    ext = jnp.concatenate([jnp.where(at_start, 0.0, ubprev_ref[...]), u], axis=0)
    sums = {1: ext}
    span = 1
    while span < POOL_WINDOWS[-1]:
        sums[2 * span] = sums[span] + pltpu.roll(sums[span], span, 0)
        span *= 2
    lane = lax.broadcasted_iota(jnp.int32, (tm, MIX_W), 1)
    group = lane // (MIX_W // len(POOL_WINDOWS))
    win_sum = sums[POOL_WINDOWS[-1]][POOL_HALO:, :]
    win = jnp.full((tm, MIX_W), float(POOL_WINDOWS[-1]), F32)
    for gi in range(len(POOL_WINDOWS) - 2, -1, -1):
        win_sum = jnp.where(group == gi, sums[POOL_WINDOWS[gi]][POOL_HALO:, :], win_sum)
        win = jnp.where(group == gi, float(POOL_WINDOWS[gi]), win)
    pos = (i * tm) % seq + lax.broadcasted_iota(jnp.int32, (tm, MIX_W), 0)
    count = jnp.minimum((pos + 1).astype(F32), win)
    y_pool = _mm(win_sum / count - u, poolw_ref[...]) * pscale_ref[...]

    z = cg_ref[...] * z_ref[...]
    z_prev = jnp.where(at_start, 0.0, cgprev_ref[...] * zprev_ref[...])
    zext = jnp.concatenate([z_prev, z], axis=0)
    conv = (convw_ref[2:3, :] * zext
            + convw_ref[1:2, :] * pltpu.roll(zext, 1, 0)
            + convw_ref[0:1, :] * pltpu.roll(zext, 2, 0))
    y_conv = bg_ref[...] * conv[SUBLANES:, :]

    merged = None
    for gi, y in enumerate((ya_ref[...], y_pool, yc_ref[...], y_conv)):
        gate = jax.nn.sigmoid(jnp.dot(h, wgate_ref[:, gi * d:(gi + 1) * d], preferred_element_type=F32))
        term = _mm(y, wbr_ref[gi]) * gate
        merged = term if merged is None else merged + term
    o_ref[...] = x + _mm(merged, wout_ref[...])


def _merge(x, p, y_s5, y_rwkv, norm_g, w_gates, w_branch, w_out, pool_w, pool_scale, conv_w, *,
           seq, tm=512):
    t, d = x.shape
    tm = min(tm, seq)
    pool_groups = len(POOL_WINDOWS)
    eye = jnp.eye(pool_groups, dtype=F32)
    pool_bd = jnp.einsum("gcd,gk->gckd", pool_w, eye).reshape(MIX_W, MIX_W)

    def cur(col):
        return pl.BlockSpec((tm, MIX_W), lambda i: (i, col))

    def prev(col, rows):
        per_tile = tm // rows
        return pl.BlockSpec((rows, MIX_W), lambda i: (jnp.maximum(i * per_tile - 1, 0), col))

    in_specs = [
        pl.BlockSpec((tm, d), lambda i: (i, 0)),
        cur(1), prev(1, POOL_HALO), cur(5), prev(5, SUBLANES), cur(6), cur(7), prev(7, SUBLANES),
        pl.BlockSpec((tm, MIX_W), lambda i: (i, 0)),
        pl.BlockSpec((tm, MIX_W), lambda i: (i, 0)),
        _const_spec((1, d)),
        _const_spec(w_gates.shape), _const_spec(w_branch.shape), _const_spec(w_out.shape),
        _const_spec((MIX_W, MIX_W)), _const_spec((1, MIX_W)), _const_spec(conv_w.shape),
    ]
    return pl.pallas_call(
        functools.partial(_merge_body, seq=seq),
        grid=(t // tm,),
        in_specs=in_specs,
        out_specs=pl.BlockSpec((tm, d), lambda i: (i, 0)),
        out_shape=jax.ShapeDtypeStruct((t, d), F32),
        compiler_params=_params(1),
        name="merge",
    )(x, p, p, p, p, p, p, p, y_s5, y_rwkv, norm_g.reshape(1, d), w_gates, w_branch, w_out,
      pool_bd, pool_scale.reshape(1, MIX_W), conv_w)


def kernel(x, ffn1_norm, ffn1_w_gate, ffn1_w_up, ffn1_w_down, mix_norm, w_in, s5_lambda_re, s5_lambda_im, s5_log_dt, s5_b_re, s5_b_im, s5_c_re, s5_c_im, s5_d, s5_w_glu, pool_w, pool_scale, rwkv_mu_rkv, rwkv_mu_wag, rwkv_w0, rwkv_w1, rwkv_w2, rwkv_a0, rwkv_a1, rwkv_a2, rwkv_g1, rwkv_g2, rwkv_k_k, rwkv_k_a, rwkv_r_k, rwkv_ln_w, rwkv_ln_b, conv_w, w_branch, w_out, ffn2_norm, ffn2_w_gate, ffn2_w_up, ffn2_w_down, final_norm):
    batch, seq, d = x.shape
    depth = w_in.shape[0]
    n_small = 8 * MIX_W
    bf = lambda w: w.astype(BF16)
    xf = x.reshape(batch * seq, d)
    for l in range(depth):
        xf = _ffn(xf, ffn1_norm[l], bf(ffn1_w_gate[l]), bf(ffn1_w_up[l]), bf(ffn1_w_down[l]))
        p, lora = _proj(xf, mix_norm[l], bf(w_in[l, :, :n_small]), rwkv_mu_wag[l], rwkv_w0[l],
                        bf(rwkv_w1[l]), rwkv_w2[l], rwkv_a0[l], bf(rwkv_a1[l]), rwkv_a2[l],
                        bf(rwkv_g1[l]), rwkv_g2[l], seq=seq)
        y_s5 = _s5(p, s5_lambda_re[l], s5_lambda_im[l], s5_log_dt[l], s5_b_re[l], s5_b_im[l],
                   s5_c_re[l], s5_c_im[l], s5_d[l], s5_w_glu[l], batch=batch, seq=seq)
        y_rwkv = _rwkv(p, lora, rwkv_mu_rkv[l], rwkv_k_k[l], rwkv_k_a[l], rwkv_r_k[l].reshape(-1),
                       rwkv_ln_w[l], rwkv_ln_b[l], batch=batch, seq=seq)
        xf = _merge(xf, p, y_s5, y_rwkv, mix_norm[l], bf(w_in[l, :, n_small:]), bf(w_branch[l]),
                    bf(w_out[l]), pool_w[l], pool_scale[l], conv_w[l], seq=seq)
        xf = _ffn(xf, ffn2_norm[l], bf(ffn2_w_gate[l]), bf(ffn2_w_up[l]), bf(ffn2_w_down[l]),
                  final_norm if l == depth - 1 else None)
    return xf.reshape(batch, seq, d)
```

```python
import functools
import math

import jax
import jax.numpy as jnp
from jax import lax
from jax.experimental import pallas as pl
from jax.experimental.pallas import tpu as pltpu

F32 = jnp.float32
BF16 = jnp.bfloat16

MIX_W = 256
N_BRANCH = 4
S5_GROUPS = 16
S5_STATE = 64
POOL_WINDOWS = (2, 4, 8, 16)
POOL_HALO = 16
RW_HEAD = 64
RW_HEADS = MIX_W // RW_HEAD
RW_CHUNK = 64
RW_GN_EPS = 64e-5
NORM_EPS = 1e-6
SUBLANES = 8
VMEM_LIMIT_BYTES = 56 * 1024 * 1024


def _params(n_parallel, n_arbitrary=0):
    return pltpu.CompilerParams(
        dimension_semantics=("parallel",) * n_parallel + ("arbitrary",) * n_arbitrary,
        vmem_limit_bytes=VMEM_LIMIT_BYTES)


def _const_spec(shape):
    zeros = (0,) * len(shape)
    return pl.BlockSpec(shape, lambda *_: zeros, pipeline_mode=pl.Buffered(1))


def _layer_spec(stacked_shape, layer, col_block=None):
    shape = tuple(stacked_shape[1:])
    tail = (0,) * (len(shape) - 1)
    if col_block is None:
        index = (layer,) + tail + (0,)
    else:
        width, col = col_block
        shape = shape[:-1] + (width,)
        index = (layer,) + tail + (col,)
    return pl.BlockSpec((None,) + shape, lambda *_: index, pipeline_mode=pl.Buffered(1))


def _rmsnorm(x, g):
    return x * lax.rsqrt(jnp.mean(x * x, axis=-1, keepdims=True) + NORM_EPS) * g


def _mm(a, b):
    return jnp.dot(a.astype(BF16), b.astype(BF16), preferred_element_type=F32)


def _mm_nt(a, b):
    return lax.dot_general(a.astype(BF16), b.astype(BF16), (((1,), (1,)), ((), ())),
                           preferred_element_type=F32)


def _mm_tn(a, b):
    return lax.dot_general(a.astype(BF16), b.astype(BF16), (((0,), (0,)), ((), ())),
                           preferred_element_type=F32)


def _shift_rows(cur, prev_row):
    rolled = pltpu.roll(cur, 1, 0)
    row = lax.broadcasted_iota(jnp.int32, cur.shape, 0)
    return jnp.where(row == 0, prev_row, rolled)


def _softplus(z):
    return jnp.maximum(z, 0.0) + jnp.log1p(jnp.exp(-jnp.abs(z)))


def _ffn_body(x_ref, g_ref, wg_ref, wu_ref, wd_ref, *rest, f_chunk, final):
    if final:
        fg_ref, o_ref, acc_ref = rest
    else:
        o_ref, acc_ref = rest
    x = x_ref[...]
    h = _rmsnorm(x, g_ref[...]).astype(BF16)
    d_ff = wg_ref.shape[1]
    for j in range(d_ff // f_chunk):
        sl = slice(j * f_chunk, (j + 1) * f_chunk)
        gate = jnp.dot(h, wg_ref[:, sl], preferred_element_type=F32)
        up = jnp.dot(h, wu_ref[:, sl], preferred_element_type=F32)
        act = (gate * jax.nn.sigmoid(gate) * up).astype(BF16)
        contrib = jnp.dot(act, wd_ref[sl, :], preferred_element_type=F32)
        if j == 0:
            acc_ref[...] = contrib
        else:
            acc_ref[...] += contrib
    y = x + 0.5 * acc_ref[...]
    if final:
        y = _rmsnorm(y, fg_ref[...])
    o_ref[...] = y


def _ffn(x, norm_g, w_gate, w_up, w_down, layer, final_g=None, *, tm=512, f_chunk=256):
    t, d = x.shape
    tm = min(tm, t)
    final = final_g is not None
    in_specs = [
        pl.BlockSpec((tm, d), lambda i: (i, 0)),
        _const_spec((1, d)),
        _layer_spec(w_gate.shape, layer),
        _layer_spec(w_up.shape, layer),
        _layer_spec(w_down.shape, layer),
    ]
    args = [x, norm_g.reshape(1, d), w_gate, w_up, w_down]
    if final:
        in_specs.append(_const_spec((1, d)))
        args.append(final_g.reshape(1, d))
    return pl.pallas_call(
        functools.partial(_ffn_body, f_chunk=f_chunk, final=final),
        grid=(t // tm,),
        in_specs=in_specs,
        out_specs=pl.BlockSpec((tm, d), lambda i: (i, 0)),
        out_shape=jax.ShapeDtypeStruct((t, d), F32),
        scratch_shapes=[pltpu.VMEM((tm, d), F32)],
        compiler_params=_params(1),
        name="ffn_final" if final else "ffn",
    )(*args)


def _proj_body(x_ref, xprev_ref, g_ref, win_ref, mu_ref, w0_ref, w1_ref, w2_ref, a0_ref, a1_ref,
               a2_ref, g1_ref, g2_ref, p_ref, lora_ref, *, seq):
    i = pl.program_id(0)
    tm = x_ref.shape[0]
    g = g_ref[...]
    h = _rmsnorm(x_ref[...], g)
    at_start = (i * tm) % seq == 0
    h_prev = _rmsnorm(xprev_ref[SUBLANES - 1:SUBLANES, :], g)
    h_prev = jnp.where(at_start, 0.0, h_prev)
    hx = _shift_rows(h, h_prev) - h
    p_ref[...] = _mm(h, win_ref[...])
    xw = h + hx * mu_ref[0:1, :]
    xa = h + hx * mu_ref[1:2, :]
    xg = h + hx * mu_ref[2:3, :]
    w_pre = w0_ref[...] + _mm(jnp.tanh(_mm(xw, w1_ref[...])), w2_ref[...])
    w_log = -_softplus(-w_pre) - 0.5
    lora_ref[:, 0:MIX_W] = -jnp.exp(w_log)
    lora_ref[:, MIX_W:2 * MIX_W] = jax.nn.sigmoid(a0_ref[...] + _mm(_mm(xa, a1_ref[...]), a2_ref[...]))
    lora_ref[:, 2 * MIX_W:3 * MIX_W] = _mm(jax.nn.sigmoid(_mm(xg, g1_ref[...])), g2_ref[...])


def _proj(x, norm_g, w_in, layer, mu_wag, w0, w1, w2, a0, a1, a2, g1, g2, *, seq, tm=512):
    t, d = x.shape
    tm = min(tm, seq)
    n_small = 8 * MIX_W
    blocks_per_tile = tm // SUBLANES
    in_specs = [
        pl.BlockSpec((tm, d), lambda i: (i, 0)),
        pl.BlockSpec((SUBLANES, d), lambda i: (jnp.maximum(i * blocks_per_tile - 1, 0), 0)),
        _const_spec((1, d)),
        _layer_spec(w_in.shape, layer, col_block=(n_small, 0)),
        _const_spec(mu_wag.shape),
        _const_spec((1, MIX_W)),
        _const_spec(w1.shape),
        _const_spec(w2.shape),
        _const_spec((1, MIX_W)),
        _const_spec(a1.shape),
        _const_spec(a2.shape),
        _const_spec(g1.shape),
        _const_spec(g2.shape),
    ]
    return pl.pallas_call(
        functools.partial(_proj_body, seq=seq),
        grid=(t // tm,),
        in_specs=in_specs,
        out_specs=[pl.BlockSpec((tm, n_small), lambda i: (i, 0)),
                   pl.BlockSpec((tm, 3 * MIX_W), lambda i: (i, 0))],
        out_shape=[jax.ShapeDtypeStruct((t, n_small), F32),
                   jax.ShapeDtypeStruct((t, 3 * MIX_W), F32)],
        compiler_params=_params(1),
        name="proj",
    )(x, x, norm_g.reshape(1, d), w_in, mu_wag, w0.reshape(1, MIX_W), w1, w2,
      a0.reshape(1, MIX_W), a1, a2, g1, g2)


def _s5_body(u_ref, lre_ref, lim_ref, ldt_ref, bre_ref, bim_ref, cre_ref, cim_ref, d_ref, wglu_ref,
             o_ref, wre_s, wim_s, abar_s, xre_s, xim_s, hre_s, him_s, st_s, *, unroll):
    @pl.when(pl.program_id(1) == 0)
    def _():
        lr = lre_ref[...]
        li = lim_ref[...]
        dt = jnp.exp(ldt_ref[...])
        mag = jnp.exp(lr * dt)
        ar = mag * jnp.cos(li * dt)
        ai = mag * jnp.sin(li * dt)
        inv = 1.0 / (lr * lr + li * li)
        qr, qi = lr * inv, -li * inv
        coef_re = (ar - 1.0) * qr - ai * qi
        coef_im = (ar - 1.0) * qi + ai * qr
        wre_s[...] = (coef_re * bre_ref[...] - coef_im * bim_ref[...]).astype(BF16)
        wim_s[...] = (coef_re * bim_ref[...] + coef_im * bre_ref[...]).astype(BF16)
        abar_s[0:1, :] = ar
        abar_s[1:2, :] = ai
        abar_s[2:3, :] = ar * ar - ai * ai
        abar_s[3:4, :] = 2.0 * ar * ai
        st_s[...] = jnp.zeros_like(st_s)

    u = u_ref[...]
    ub = u.astype(BF16)
    tl, n_state = u.shape[0], wre_s.shape[1]
    blocks = (tl // SUBLANES, SUBLANES, n_state)
    xre_s[...] = jnp.dot(ub, wre_s[...], preferred_element_type=F32).reshape(blocks)
    xim_s[...] = jnp.dot(ub, wim_s[...], preferred_element_type=F32).reshape(blocks)
    ar = abar_s[0:1, :]
    ai = abar_s[1:2, :]
    a2r = abar_s[2:3, :]
    a2i = abar_s[3:4, :]

    def step(j, carry):
        hr, hi = carry
        for q in range(0, SUBLANES, 2):
            x1r, x1i = xre_s[j, q:q + 1, :], xim_s[j, q:q + 1, :]
            x2r, x2i = xre_s[j, q + 1:q + 2, :], xim_s[j, q + 1:q + 2, :]
            yr = ar * x1r - ai * x1i + x2r
            yi = ar * x1i + ai * x1r + x2i
            hre_s[j, q:q + 1, :] = ar * hr - ai * hi + x1r
            him_s[j, q:q + 1, :] = ar * hi + ai * hr + x1i
            hr, hi = a2r * hr - a2i * hi + yr, a2r * hi + a2i * hr + yi
            hre_s[j, q + 1:q + 2, :] = hr
            him_s[j, q + 1:q + 2, :] = hi
        return hr, hi

    hr, hi = lax.fori_loop(0, tl // SUBLANES, step, (st_s[0:1, :], st_s[1:2, :]), unroll=unroll)
    st_s[0:1, :] = hr
    st_s[1:2, :] = hi

    y = (_mm(hre_s[...].reshape(tl, n_state), cre_ref[...])
         - _mm(him_s[...].reshape(tl, n_state), cim_ref[...]) + d_ref[...] * u)
    gl = jax.nn.gelu(y, approximate=True)
    o_ref[...] = gl * jax.nn.sigmoid(_mm(gl, wglu_ref[...]))


def _s5(p, lam_re, lam_im, log_dt, b_re, b_im, c_re, c_im, d_skip, w_glu, *, batch, seq, tl=512):
    n_state = S5_GROUPS * S5_STATE
    tl = min(tl, seq)
    steps = seq // tl
    eye = jnp.eye(S5_GROUPS, dtype=F32)

    def b_blockdiag(b):
        return jnp.einsum("gph,gk->ghkp", b, eye).reshape(MIX_W, n_state)

    def c_blockdiag(c):
        return jnp.einsum("ghp,gk->gpkh", c, eye).reshape(n_state, MIX_W)

    row = lambda v: v.reshape(1, -1)
    in_specs = [
        pl.BlockSpec((tl, MIX_W), lambda b, i: (b * steps + i, 0)),
        _const_spec((1, n_state)), _const_spec((1, n_state)), _const_spec((1, n_state)),
        _const_spec((MIX_W, n_state)), _const_spec((MIX_W, n_state)),
        _const_spec((n_state, MIX_W)), _const_spec((n_state, MIX_W)),
        _const_spec((1, MIX_W)), _const_spec((MIX_W, MIX_W)),
    ]
    return pl.pallas_call(
        functools.partial(_s5_body, unroll=2),
        grid=(batch, steps),
        in_specs=in_specs,
        out_specs=pl.BlockSpec((tl, MIX_W), lambda b, i: (b * steps + i, 0)),
        out_shape=jax.ShapeDtypeStruct((batch * seq, MIX_W), F32),
        scratch_shapes=[pltpu.VMEM((MIX_W, n_state), BF16), pltpu.VMEM((MIX_W, n_state), BF16),
                        pltpu.VMEM((4, n_state), F32)]
                       + [pltpu.VMEM((tl // SUBLANES, SUBLANES, n_state), F32)] * 4
                       + [pltpu.VMEM((2, n_state), F32)],
        compiler_params=_params(1, 1),
        name="s5",
    )(p, row(lam_re), row(lam_im), row(jnp.broadcast_to(log_dt[:, None], (S5_GROUPS, S5_STATE))),
      b_blockdiag(b_re), b_blockdiag(b_im), c_blockdiag(c_re).astype(BF16),
      c_blockdiag(c_im).astype(BF16), row(d_skip), w_glu.astype(BF16))


def _segment_ones(n, seg):
    r = lax.broadcasted_iota(jnp.int32, (n, n), 0) // seg
    c = lax.broadcasted_iota(jnp.int32, (n, n), 1) // seg
    return r == c


def _rwkv_body(rp_ref, kp_ref, vp_ref, rprev_ref, kprev_ref, vprev_ref, lora_ref, mu_ref, kk_ref,
               ka_ref, rk_ref, lnw_ref, lnb_ref, o_ref, st_s):
    tl = rp_ref.shape[0]
    c = RW_CHUNK
    first = pl.program_id(1) == 0

    @pl.when(first)
    def _():
        st_s[...] = jnp.zeros_like(st_s)

    def mixed(cur_ref, prev_ref, mu):
        cur = cur_ref[...]
        prev_row = jnp.where(first, 0.0, prev_ref[SUBLANES - 1:SUBLANES, :])
        return cur + (_shift_rows(cur, prev_row) - cur) * mu

    r = mixed(rp_ref, rprev_ref, mu_ref[0:1, :])
    k = mixed(kp_ref, kprev_ref, mu_ref[1:2, :])
    v = mixed(vp_ref, vprev_ref, mu_ref[2:3, :])
    logw = lora_ref[:, 0:MIX_W]
    a = lora_ref[:, MIX_W:2 * MIX_W]
    head_ones = _segment_ones(MIX_W, RW_HEAD).astype(BF16)
    kraw = k * kk_ref[...]
    kk = kraw / jnp.maximum(jnp.sqrt(_mm(kraw * kraw, head_ones)), 1e-12)
    kmod = k * (1.0 + (a - 1.0) * ka_ref[...])

    bdot = functools.partial(jnp.dot, preferred_element_type=F32)
    n_chunks = tl // c

    ti = lax.broadcasted_iota(jnp.int32, (tl, tl), 0)
    tj = lax.broadcasted_iota(jnp.int32, (tl, tl), 1)
    tri = (((ti // c) == (tj // c)) & (tj <= ti)).astype(BF16)
    w_hi = logw.astype(BF16)
    rem = logw - w_hi.astype(F32)
    w_mid = rem.astype(BF16)
    w_lo = (rem - w_mid.astype(F32)).astype(BF16)
    cum = bdot(tri, w_hi) + bdot(tri, w_mid) + bdot(tri, w_lo)
    tot = jnp.concatenate(
        [jnp.broadcast_to(cum[(n + 1) * c - 1:(n + 1) * c, :], (c, MIX_W)) for n in range(n_chunks)],
        axis=0)
    inv = jnp.exp(-cum)
    to_end = jnp.exp(tot - cum)
    r_dec = (r * jnp.exp(cum)).astype(BF16)
    a_dec = (-kk * jnp.exp(cum - logw)).astype(BF16)
    b_inv = (kk * a * inv).astype(BF16)
    k_inv = (kmod * inv).astype(BF16)
    b_end = (kk * a * to_end).astype(BF16)
    k_end = (kmod * to_end).astype(BF16)
    v_b = v.astype(BF16)
    p_end = jnp.exp(tot)

    ci = lax.broadcasted_iota(jnp.int32, (c, c), 0)
    cj = lax.broadcasted_iota(jnp.int32, (c, c), 1)
    strict = cj < ci
    incl = cj <= ci
    eye_c = (ci == cj).astype(F32)
    eye_h = (lax.broadcasted_iota(jnp.int32, (RW_HEAD, RW_HEAD), 0)
             == lax.broadcasted_iota(jnp.int32, (RW_HEAD, RW_HEAD), 1)).astype(F32)

    probs =[(h, n) for h in range(RW_HEADS) for n in range(n_chunks)]
    n_probs = len(probs)

    def piece(x, h, n):
        return x[n * c:(n + 1) * c, h * RW_HEAD:(h + 1) * RW_HEAD]

    a_d = [piece(a_dec, h, n) for h, n in probs]
    r_d = [piece(r_dec, h, n) for h, n in probs]
    vv = [piece(v_b, h, n) for h, n in probs]
    g = [_mm_nt(jnp.concatenate([a_d[i], r_d[i]], axis=0),
                jnp.concatenate([piece(b_inv, h, n), piece(k_inv, h, n)], axis=0))
         for i, (h, n) in enumerate(probs)]
    a_ab = [jnp.where(strict, x[0:c, 0:c], 0.0) for x in g]
    a_ak = [jnp.where(strict, x[0:c, c:2 * c], 0.0).astype(BF16) for x in g]
    lhs_o = [jnp.concatenate([r_d[i], jnp.where(incl, x[c:2 * c, 0:c], 0.0).astype(BF16),
                              jnp.where(incl, x[c:2 * c, c:2 * c], 0.0).astype(BF16)], axis=1)
             for i, x in enumerate(g)]
    tinv = [eye_c + x for x in a_ab]
    pw = [x.astype(BF16) for x in a_ab]
    pw = [bdot(x, x).astype(BF16) for x in pw]
    akv = [bdot(a_ak[i], vv[i]).astype(BF16) for i in range(n_probs)]
    n_steps = int(math.log2(c)) - 1
    for s in range(n_steps):
        if s + 1 < n_steps:
            both = [bdot(jnp.concatenate([tinv[i].astype(BF16), pw[i]], axis=0), pw[i])
                    for i in range(n_probs)]
            tinv = [tinv[i] + both[i][0:c] for i in range(n_probs)]
            pw = [x[c:2 * c].astype(BF16) for x in both]
        else:
            tinv = [tinv[i] + bdot(tinv[i].astype(BF16), pw[i]) for i in range(n_probs)]
    w1u0 = [bdot(tinv[i].astype(BF16), jnp.concatenate([a_d[i], akv[i]], axis=1))
            for i in range(n_probs)]
    tn = [_mm_tn(piece(b_end, h, n), w1u0[i]) for i, (h, n) in enumerate(probs)]
    kv = [_mm_tn(piece(k_end, h, n), vv[i]) for i, (h, n) in enumerate(probs)]
    trans = [(tn[i][:, 0:RW_HEAD]
              + eye_h * p_end[n * c:n * c + 1, h * RW_HEAD:(h + 1) * RW_HEAD]).astype(BF16)
             for i, (h, n) in enumerate(probs)]
    add = [tn[i][:, RW_HEAD:2 * RW_HEAD] + kv[i] for i in range(n_probs)]
    w1 = [x[:, 0:RW_HEAD].astype(BF16) for x in w1u0]
    u0 = [x[:, RW_HEAD:2 * RW_HEAD] for x in w1u0]

    st = [st_s[h] for h in range(RW_HEADS)]
    st_at = {}
    for n in range(n_chunks):
        for h in range(RW_HEADS):
            i = h * n_chunks + n
            st_b = st[h].astype(BF16)
            st_at[i] = st_b
            st[h] = bdot(trans[i], st_b) + add[i]
    for h in range(RW_HEADS):
        st_s[h] = st[h]
    u = [(bdot(w1[i], st_at[i]) + u0[i]).astype(BF16) for i in range(n_probs)]
    o_p = [bdot(lhs_o[i], jnp.concatenate([st_at[i], u[i], vv[i]], axis=0)) for i in range(n_probs)]
    o = jnp.concatenate(
        [jnp.concatenate(o_p[h * n_chunks:(h + 1) * n_chunks], axis=0) for h in range(RW_HEADS)], axis=1)

    inv_n = 1.0 / RW_HEAD
    mean = _mm(o, head_ones) * inv_n
    dlt = o - mean
    var = _mm(dlt * dlt, head_ones) * inv_n
    o_n = dlt * lax.rsqrt(var + RW_GN_EPS) * lnw_ref[...] + lnb_ref[...]
    bonus = _mm(r * kmod * rk_ref[...], head_ones) * v
    o_ref[...] = (o_n + bonus) * lora_ref[:, 2 * MIX_W:3 * MIX_W]


def _rwkv(p, lora, mu_rkv, k_k, k_a, r_k, ln_w, ln_b, *, batch, seq, tl=256):
    tl = min(tl, seq)
    steps = seq // tl
    blocks_per_tile = tl // SUBLANES
    row = lambda v: v.reshape(1, MIX_W)

    def cur(col):
        return pl.BlockSpec((tl, MIX_W), lambda b, i: (b * steps + i, col))

    def prev(col):
        return pl.BlockSpec(
            (SUBLANES, MIX_W),
            lambda b, i: (jnp.maximum((b * steps + i) * blocks_per_tile - 1, 0), col))

    in_specs = [cur(2), cur(3), cur(4), prev(2), prev(3), prev(4),
                pl.BlockSpec((tl, 3 * MIX_W), lambda b, i: (b * steps + i, 0)),
                _const_spec((3, MIX_W))] + [_const_spec((1, MIX_W))] * 5
    return pl.pallas_call(
        _rwkv_body,
        grid=(batch, steps),
        in_specs=in_specs,
        out_specs=pl.BlockSpec((tl, MIX_W), lambda b, i: (b * steps + i, 0)),
        out_shape=jax.ShapeDtypeStruct((batch * seq, MIX_W), F32),
        scratch_shapes=[pltpu.VMEM((RW_HEADS, RW_HEAD, RW_HEAD), F32)],
        compiler_params=_params(1, 1),
        name="rwkv",
    )(p, p, p, p, p, p, lora, mu_rkv, row(k_k), row(k_a), row(r_k), row(ln_w), row(ln_b))


def _merge_body(x_ref, ub_ref, ubprev_ref, z_ref, zprev_ref, bg_ref, cg_ref, cgprev_ref, ya_ref,
                yc_ref, g_ref, wgate01_ref, wgate23_ref, wbr_ref, wout_ref, poolw_ref, pscale_ref,
                convw_ref, o_ref, *, seq):
    i = pl.program_id(0)
    tm, d = x_ref.shape
    at_start = (i * tm) % seq == 0
    x = x_ref[...]
    h = _rmsnorm(x, g_ref[...]).astype(BF16)

    u = ub_ref[...]
    ext = jnp.concatenate([jnp.where(at_start, 0.0, ubprev_ref[...]), u], axis=0)
    sums = {1: ext}
    span = 1
    while span < POOL_WINDOWS[-1]:
        sums[2 * span] = sums[span] + pltpu.roll(sums[span], span, 0)
        span *= 2
    lane = lax.broadcasted_iota(jnp.int32, (tm, MIX_W), 1)
    group = lane // (MIX_W // len(POOL_WINDOWS))
    win_sum = sums[POOL_WINDOWS[-1]][POOL_HALO:, :]
    win = jnp.full((tm, MIX_W), float(POOL_WINDOWS[-1]), F32)
    for gi in range(len(POOL_WINDOWS) - 2, -1, -1):
        win_sum = jnp.where(group == gi, sums[POOL_WINDOWS[gi]][POOL_HALO:, :], win_sum)
        win = jnp.where(group == gi, float(POOL_WINDOWS[gi]), win)
    pos = (i * tm) % seq + lax.broadcasted_iota(jnp.int32, (tm, MIX_W), 0)
    count = jnp.minimum((pos + 1).astype(F32), win)
    y_pool = _mm(win_sum / count - u, poolw_ref[...]) * pscale_ref[...]

    z = cg_ref[...] * z_ref[...]
    z_prev = jnp.where(at_start, 0.0, cgprev_ref[...] * zprev_ref[...])
    zext = jnp.concatenate([z_prev, z], axis=0)
    conv = (convw_ref[2:3, :] * zext
            + convw_ref[1:2, :] * pltpu.roll(zext, 1, 0)
            + convw_ref[0:1, :] * pltpu.roll(zext, 2, 0))
    y_conv = bg_ref[...] * conv[SUBLANES:, :]

    merged = None
    for gi, y in enumerate((ya_ref[...], y_pool, yc_ref[...], y_conv)):
        wgate_ref = (wgate01_ref, wgate23_ref)[gi // 2]
        cols = slice((gi % 2) * d, (gi % 2 + 1) * d)
        gate = jax.nn.sigmoid(jnp.dot(h, wgate_ref[:, cols], preferred_element_type=F32))
        term = _mm(y, wbr_ref[gi]) * gate
        merged = term if merged is None else merged + term
    o_ref[...] = x + _mm(merged, wout_ref[...])


def _merge(x, p, y_s5, y_rwkv, norm_g, w_in, w_branch, w_out, layer, pool_w, pool_scale, conv_w, *,
           seq, tm=512):
    t, d = x.shape
    tm = min(tm, seq)
    gate_block = 2 * d
    first_gate_block = 8 * MIX_W // gate_block
    pool_groups = len(POOL_WINDOWS)
    eye = jnp.eye(pool_groups, dtype=F32)
    pool_bd = jnp.einsum("gcd,gk->gckd", pool_w, eye).reshape(MIX_W, MIX_W)

    def cur(col):
        return pl.BlockSpec((tm, MIX_W), lambda i: (i, col))

    def prev(col, rows):
        per_tile = tm // rows
        return pl.BlockSpec((rows, MIX_W), lambda i: (jnp.maximum(i * per_tile - 1, 0), col))

    in_specs = [
        pl.BlockSpec((tm, d), lambda i: (i, 0)),
        cur(1), prev(1, POOL_HALO), cur(5), prev(5, SUBLANES), cur(6), cur(7), prev(7, SUBLANES),
        pl.BlockSpec((tm, MIX_W), lambda i: (i, 0)),
        pl.BlockSpec((tm, MIX_W), lambda i: (i, 0)),
        _const_spec((1, d)),
        _layer_spec(w_in.shape, layer, col_block=(gate_block, first_gate_block)),
        _layer_spec(w_in.shape, layer, col_block=(gate_block, first_gate_block + 1)),
        _layer_spec(w_branch.shape, layer), _layer_spec(w_out.shape, layer),
        _const_spec((MIX_W, MIX_W)), _const_spec((1, MIX_W)), _const_spec(conv_w.shape),
    ]
    return pl.pallas_call(
        functools.partial(_merge_body, seq=seq),
        grid=(t // tm,),
        in_specs=in_specs,
        out_specs=pl.BlockSpec((tm, d), lambda i: (i, 0)),
        out_shape=jax.ShapeDtypeStruct((t, d), F32),
        compiler_params=_params(1),
        name="merge",
    )(x, p, p, p, p, p, p, p, y_s5, y_rwkv, norm_g.reshape(1, d), w_in, w_in, w_branch, w_out,
      pool_bd, pool_scale.reshape(1, MIX_W), conv_w)


def kernel(x, ffn1_norm, ffn1_w_gate, ffn1_w_up, ffn1_w_down, mix_norm, w_in, s5_lambda_re, s5_lambda_im, s5_log_dt, s5_b_re, s5_b_im, s5_c_re, s5_c_im, s5_d, s5_w_glu, pool_w, pool_scale, rwkv_mu_rkv, rwkv_mu_wag, rwkv_w0, rwkv_w1, rwkv_w2, rwkv_a0, rwkv_a1, rwkv_a2, rwkv_g1, rwkv_g2, rwkv_k_k, rwkv_k_a, rwkv_r_k, rwkv_ln_w, rwkv_ln_b, conv_w, w_branch, w_out, ffn2_norm, ffn2_w_gate, ffn2_w_up, ffn2_w_down, final_norm):
    batch, seq, d = x.shape
    depth = w_in.shape[0]
    bf = lambda w: w.astype(BF16)
    ffn1 = (bf(ffn1_w_gate), bf(ffn1_w_up), bf(ffn1_w_down))
    ffn2 = (bf(ffn2_w_gate), bf(ffn2_w_up), bf(ffn2_w_down))
    w_in_b, w_branch_b, w_out_b = bf(w_in), bf(w_branch), bf(w_out)
    xf = x.reshape(batch * seq, d)
    for l in range(depth):
        xf = _ffn(xf, ffn1_norm[l], *ffn1, l)
        p, lora = _proj(xf, mix_norm[l], w_in_b, l, rwkv_mu_wag[l], rwkv_w0[l],
                        bf(rwkv_w1[l]), rwkv_w2[l], rwkv_a0[l], bf(rwkv_a1[l]), rwkv_a2[l],
                        bf(rwkv_g1[l]), rwkv_g2[l], seq=seq)
        y_s5 = _s5(p, s5_lambda_re[l], s5_lambda_im[l], s5_log_dt[l], s5_b_re[l], s5_b_im[l],
                   s5_c_re[l], s5_c_im[l], s5_d[l], s5_w_glu[l], batch=batch, seq=seq)
        y_rwkv = _rwkv(p, lora, rwkv_mu_rkv[l], rwkv_k_k[l], rwkv_k_a[l], rwkv_r_k[l].reshape(-1),
                       rwkv_ln_w[l], rwkv_ln_b[l], batch=batch, seq=seq)
        xf = _merge(xf, p, y_s5, y_rwkv, mix_norm[l], w_in_b, w_branch_b, w_out_b, l,
                    pool_w[l], pool_scale[l], conv_w[l], seq=seq)
        xf = _ffn(xf, ffn2_norm[l], *ffn2, l, final_norm if l == depth - 1 else None)
    return xf.reshape(batch, seq, d)
```

```python
import functools
import math

import jax
import jax.numpy as jnp
from jax import lax
from jax.experimental import pallas as pl
from jax.experimental.pallas import tpu as pltpu

F32 = jnp.float32
BF16 = jnp.bfloat16

MIX_W = 256
N_BRANCH = 4
S5_GROUPS = 16
S5_STATE = 64
POOL_WINDOWS = (2, 4, 8, 16)
POOL_HALO = 16
RW_HEAD = 64
RW_CHUNK = 64
RW_GN_EPS = 64e-5
NORM_EPS = 1e-6
SUBLANES = 8
VMEM_LIMIT_BYTES = 56 * 1024 * 1024


def _params(n_parallel, n_arbitrary=0):
    return pltpu.CompilerParams(
        dimension_semantics=("parallel",) * n_parallel + ("arbitrary",) * n_arbitrary,
        vmem_limit_bytes=VMEM_LIMIT_BYTES)


def _const_spec(shape):
    zeros = (0,) * len(shape)
    return pl.BlockSpec(shape, lambda *_: zeros, pipeline_mode=pl.Buffered(1))


def _layer_spec(stacked_shape, layer, col_block=None):
    shape = tuple(stacked_shape[1:])
    tail = (0,) * (len(shape) - 1)
    if col_block is None:
        index = (layer,) + tail + (0,)
    else:
        width, col = col_block
        shape = shape[:-1] + (width,)
        index = (layer,) + tail + (col,)
    return pl.BlockSpec((None,) + shape, lambda *_: index, pipeline_mode=pl.Buffered(1))


def _rmsnorm(x, g):
    return x * lax.rsqrt(jnp.mean(x * x, axis=-1, keepdims=True) + NORM_EPS) * g


def _mm(a, b):
    return jnp.dot(a.astype(BF16), b.astype(BF16), preferred_element_type=F32)


def _mm_nt(a, b):
    return lax.dot_general(a.astype(BF16), b.astype(BF16), (((1,), (1,)), ((), ())),
                           preferred_element_type=F32)


def _mm_tn(a, b):
    return lax.dot_general(a.astype(BF16), b.astype(BF16), (((0,), (0,)), ((), ())),
                           preferred_element_type=F32)


def _shift_rows(cur, prev_row):
    rolled = pltpu.roll(cur, 1, 0)
    row = lax.broadcasted_iota(jnp.int32, cur.shape, 0)
    return jnp.where(row == 0, prev_row, rolled)


def _softplus(z):
    return jnp.maximum(z, 0.0) + jnp.log1p(jnp.exp(-jnp.abs(z)))


def _ffn_body(x_ref, g_ref, wg_ref, wu_ref, wd_ref, *rest, f_chunk, final):
    if final:
        fg_ref, o_ref, acc_ref = rest
    else:
        o_ref, acc_ref = rest
    x = x_ref[...]
    h = _rmsnorm(x, g_ref[...]).astype(BF16)
    d_ff = wg_ref.shape[1]
    for j in range(d_ff // f_chunk):
        sl = slice(j * f_chunk, (j + 1) * f_chunk)
        gate = jnp.dot(h, wg_ref[:, sl], preferred_element_type=F32)
        up = jnp.dot(h, wu_ref[:, sl], preferred_element_type=F32)
        act = (gate * jax.nn.sigmoid(gate) * up).astype(BF16)
        contrib = jnp.dot(act, wd_ref[sl, :], preferred_element_type=F32)
        if j == 0:
            acc_ref[...] = contrib
        else:
            acc_ref[...] += contrib
    y = x + 0.5 * acc_ref[...]
    if final:
        y = _rmsnorm(y, fg_ref[...])
    o_ref[...] = y


def _ffn(x, norm_g, w_gate, w_up, w_down, layer, final_g=None, *, tm=512, f_chunk=256):
    t, d = x.shape
    tm = min(tm, t)
    final = final_g is not None
    in_specs = [
        pl.BlockSpec((tm, d), lambda i: (i, 0)),
        _const_spec((1, d)),
        _layer_spec(w_gate.shape, layer),
        _layer_spec(w_up.shape, layer),
        _layer_spec(w_down.shape, layer),
    ]
    args = [x, norm_g.reshape(1, d), w_gate, w_up, w_down]
    if final:
        in_specs.append(_const_spec((1, d)))
        args.append(final_g.reshape(1, d))
    return pl.pallas_call(
        functools.partial(_ffn_body, f_chunk=f_chunk, final=final),
        grid=(t // tm,),
        in_specs=in_specs,
        out_specs=pl.BlockSpec((tm, d), lambda i: (i, 0)),
        out_shape=jax.ShapeDtypeStruct((t, d), F32),
        scratch_shapes=[pltpu.VMEM((tm, d), F32)],
        compiler_params=_params(1),
        name="ffn_final" if final else "ffn",
    )(*args)


def _proj_body(x_ref, xprev_ref, g_ref, win_ref, mu_ref, w0_ref, w1_ref, w2_ref, a0_ref, a1_ref,
               a2_ref, g1_ref, g2_ref, p_ref, lora_ref, *, seq):
    i = pl.program_id(0)
    tm = x_ref.shape[0]
    g = g_ref[...]
    h = _rmsnorm(x_ref[...], g)
    at_start = (i * tm) % seq == 0
    h_prev = _rmsnorm(xprev_ref[SUBLANES - 1:SUBLANES, :], g)
    h_prev = jnp.where(at_start, 0.0, h_prev)
    hx = _shift_rows(h, h_prev) - h
    hb = h.astype(BF16)

    def main_cols(lo, hi):
        for j in range(lo, hi):
            cols = slice(j * MIX_W, (j + 1) * MIX_W)
            p_ref[:, cols] = jnp.dot(hb, win_ref[:, cols], preferred_element_type=F32)

    main_cols(0, 3)
    xw = (h + hx * mu_ref[0:1, :]).astype(BF16)
    t_w = jnp.tanh(jnp.dot(xw, w1_ref[...], preferred_element_type=F32))
    main_cols(3, 5)
    xa = (h + hx * mu_ref[1:2, :]).astype(BF16)
    t_a = jnp.dot(xa, a1_ref[...], preferred_element_type=F32)
    main_cols(5, 7)
    xg = (h + hx * mu_ref[2:3, :]).astype(BF16)
    t_g = jax.nn.sigmoid(jnp.dot(xg, g1_ref[...], preferred_element_type=F32))
    main_cols(7, 8)
    w_log = -_softplus(-(w0_ref[...] + _mm(t_w, w2_ref[...]))) - 0.5
    lora_ref[:, 0:MIX_W] = -jnp.exp(w_log)
    lora_ref[:, MIX_W:2 * MIX_W] = jax.nn.sigmoid(a0_ref[...] + _mm(t_a, a2_ref[...]))
    lora_ref[:, 2 * MIX_W:3 * MIX_W] = _mm(t_g, g2_ref[...])


def _proj(x, norm_g, w_in, layer, mu_wag, w0, w1, w2, a0, a1, a2, g1, g2, *, seq, tm=512):
    t, d = x.shape
    tm = min(tm, seq)
    n_small = 8 * MIX_W
    blocks_per_tile = tm // SUBLANES
    in_specs = [
        pl.BlockSpec((tm, d), lambda i: (i, 0)),
        pl.BlockSpec((SUBLANES, d), lambda i: (jnp.maximum(i * blocks_per_tile - 1, 0), 0)),
        _const_spec((1, d)),
        _layer_spec(w_in.shape, layer, col_block=(n_small, 0)),
        _const_spec(mu_wag.shape),
        _const_spec((1, MIX_W)),
        _const_spec(w1.shape),
        _const_spec(w2.shape),
        _const_spec((1, MIX_W)),
        _const_spec(a1.shape),
        _const_spec(a2.shape),
        _const_spec(g1.shape),
        _const_spec(g2.shape),
    ]
    return pl.pallas_call(
        functools.partial(_proj_body, seq=seq),
        grid=(t // tm,),
        in_specs=in_specs,
        out_specs=[pl.BlockSpec((tm, n_small), lambda i: (i, 0)),
                   pl.BlockSpec((tm, 3 * MIX_W), lambda i: (i, 0))],
        out_shape=[jax.ShapeDtypeStruct((t, n_small), F32),
                   jax.ShapeDtypeStruct((t, 3 * MIX_W), F32)],
        compiler_params=_params(1),
        name="proj",
    )(x, x, norm_g.reshape(1, d), w_in, mu_wag, w0.reshape(1, MIX_W), w1, w2,
      a0.reshape(1, MIX_W), a1, a2, g1, g2)


def _s5_body(u_ref, lre_ref, lim_ref, ldt_ref, bre_ref, bim_ref, cre_ref, cim_ref, d_ref, wglu_ref,
             o_ref, wre_s, wim_s, abar_s, xre_s, xim_s, hre_s, him_s, st_s, *, unroll):
    @pl.when(pl.program_id(1) == 0)
    def _():
        lr = lre_ref[...]
        li = lim_ref[...]
        dt = jnp.exp(ldt_ref[...])
        mag = jnp.exp(lr * dt)
        ar = mag * jnp.cos(li * dt)
        ai = mag * jnp.sin(li * dt)
        inv = 1.0 / (lr * lr + li * li)
        qr, qi = lr * inv, -li * inv
        coef_re = (ar - 1.0) * qr - ai * qi
        coef_im = (ar - 1.0) * qi + ai * qr
        wre_s[...] = (coef_re * bre_ref[...] - coef_im * bim_ref[...]).astype(BF16)
        wim_s[...] = (coef_re * bim_ref[...] + coef_im * bre_ref[...]).astype(BF16)
        abar_s[0:1, :] = ar
        abar_s[1:2, :] = ai
        abar_s[2:3, :] = ar * ar - ai * ai
        abar_s[3:4, :] = 2.0 * ar * ai
        st_s[...] = jnp.zeros_like(st_s)

    u = u_ref[...]
    ub = u.astype(BF16)
    tl, n_state = u.shape[0], wre_s.shape[1]
    blocks = (tl // SUBLANES, SUBLANES, n_state)
    xre_s[...] = jnp.dot(ub, wre_s[...], preferred_element_type=F32).reshape(blocks)
    xim_s[...] = jnp.dot(ub, wim_s[...], preferred_element_type=F32).reshape(blocks)
    ar = abar_s[0:1, :]
    ai = abar_s[1:2, :]
    a2r = abar_s[2:3, :]
    a2i = abar_s[3:4, :]

    def step(j, carry):
        hr, hi = carry
        for q in range(0, SUBLANES, 2):
            x1r, x1i = xre_s[j, q:q + 1, :], xim_s[j, q:q + 1, :]
            x2r, x2i = xre_s[j, q + 1:q + 2, :], xim_s[j, q + 1:q + 2, :]
            yr = ar * x1r - ai * x1i + x2r
            yi = ar * x1i + ai * x1r + x2i
            hre_s[j, q:q + 1, :] = ar * hr - ai * hi + x1r
            him_s[j, q:q + 1, :] = ar * hi + ai * hr + x1i
            hr, hi = a2r * hr - a2i * hi + yr, a2r * hi + a2i * hr + yi
            hre_s[j, q + 1:q + 2, :] = hr
            him_s[j, q + 1:q + 2, :] = hi
        return hr, hi

    hr, hi = lax.fori_loop(0, tl // SUBLANES, step, (st_s[0:1, :], st_s[1:2, :]), unroll=unroll)
    st_s[0:1, :] = hr
    st_s[1:2, :] = hi

    y = (_mm(hre_s[...].reshape(tl, n_state), cre_ref[...])
         - _mm(him_s[...].reshape(tl, n_state), cim_ref[...]) + d_ref[...] * u)
    gl = jax.nn.gelu(y, approximate=True)
    o_ref[...] = gl * jax.nn.sigmoid(_mm(gl, wglu_ref[...]))


def _s5(p, lam_re, lam_im, log_dt, b_re, b_im, c_re, c_im, d_skip, w_glu, *, batch, seq, tl=512):
    n_state = S5_GROUPS * S5_STATE
    tl = min(tl, seq)
    steps = seq // tl
    eye = jnp.eye(S5_GROUPS, dtype=F32)

    def b_blockdiag(b):
        return jnp.einsum("gph,gk->ghkp", b, eye).reshape(MIX_W, n_state)

    def c_blockdiag(c):
        return jnp.einsum("ghp,gk->gpkh", c, eye).reshape(n_state, MIX_W)

    row = lambda v: v.reshape(1, -1)
    in_specs = [
        pl.BlockSpec((tl, MIX_W), lambda b, i: (b * steps + i, 0)),
        _const_spec((1, n_state)), _const_spec((1, n_state)), _const_spec((1, n_state)),
        _const_spec((MIX_W, n_state)), _const_spec((MIX_W, n_state)),
        _const_spec((n_state, MIX_W)), _const_spec((n_state, MIX_W)),
        _const_spec((1, MIX_W)), _const_spec((MIX_W, MIX_W)),
    ]
    return pl.pallas_call(
        functools.partial(_s5_body, unroll=2),
        grid=(batch, steps),
        in_specs=in_specs,
        out_specs=pl.BlockSpec((tl, MIX_W), lambda b, i: (b * steps + i, 0)),
        out_shape=jax.ShapeDtypeStruct((batch * seq, MIX_W), F32),
        scratch_shapes=[pltpu.VMEM((MIX_W, n_state), BF16), pltpu.VMEM((MIX_W, n_state), BF16),
                        pltpu.VMEM((4, n_state), F32)]
                       + [pltpu.VMEM((tl // SUBLANES, SUBLANES, n_state), F32)] * 4
                       + [pltpu.VMEM((2, n_state), F32)],
        compiler_params=_params(1, 1),
        name="s5",
    )(p, row(lam_re), row(lam_im), row(jnp.broadcast_to(log_dt[:, None], (S5_GROUPS, S5_STATE))),
      b_blockdiag(b_re), b_blockdiag(b_im), c_blockdiag(c_re).astype(BF16),
      c_blockdiag(c_im).astype(BF16), row(d_skip), w_glu.astype(BF16))


def _segment_ones(n, seg):
    r = lax.broadcasted_iota(jnp.int32, (n, n), 0) // seg
    c = lax.broadcasted_iota(jnp.int32, (n, n), 1) // seg
    return r == c


def _rwkv_body(rp_ref, kp_ref, vp_ref, rprev_ref, kprev_ref, vprev_ref, lora_ref, mu_ref, kk_ref,
               ka_ref, rk_ref, lnw_ref, lnb_ref, o_ref, st_s):
    nb, tl = rp_ref.shape[0], rp_ref.shape[1]
    rows = nb * tl
    c = RW_CHUNK
    first = pl.program_id(0) == 0

    @pl.when(first)
    def _():
        st_s[...] = jnp.zeros_like(st_s)

    row_id = lax.broadcasted_iota(jnp.int32, (rows, MIX_W), 0)

    def mixed(cur_ref, prev_ref, mu):
        cur = cur_ref[...].reshape(rows, MIX_W)
        shifted = pltpu.roll(cur, 1, 0)
        for b in range(nb):
            prev_row = jnp.where(first, 0.0, prev_ref[b, SUBLANES - 1:SUBLANES, :])
            shifted = jnp.where(row_id == b * tl, prev_row, shifted)
        return cur + (shifted - cur) * mu

    r = mixed(rp_ref, rprev_ref, mu_ref[0:1, :])
    k = mixed(kp_ref, kprev_ref, mu_ref[1:2, :])
    v = mixed(vp_ref, vprev_ref, mu_ref[2:3, :])
    lora = lora_ref[...].reshape(rows, 3 * MIX_W)
    logw = lora[:, 0:MIX_W]
    a = lora[:, MIX_W:2 * MIX_W]
    head_ones = _segment_ones(MIX_W, RW_HEAD).astype(BF16)
    kraw = k * kk_ref[...]
    kk = kraw / jnp.maximum(jnp.sqrt(_mm(kraw * kraw, head_ones)), 1e-12)
    kmod = k * (1.0 + (a - 1.0) * ka_ref[...])

    bdot = functools.partial(jnp.dot, preferred_element_type=F32)
    n_chunks = rows // c
    chunks_per_row = tl // c

    ti = lax.broadcasted_iota(jnp.int32, (rows, rows), 0)
    tj = lax.broadcasted_iota(jnp.int32, (rows, rows), 1)
    tri = (((ti // c) == (tj // c)) & (tj <= ti)).astype(BF16)
    w_hi = logw.astype(BF16)
    rem = logw - w_hi.astype(F32)
    w_mid = rem.astype(BF16)
    w_lo = (rem - w_mid.astype(F32)).astype(BF16)
    cum = bdot(tri, w_hi) + bdot(tri, w_mid) + bdot(tri, w_lo)
    tot = jnp.concatenate(
        [jnp.broadcast_to(cum[(n + 1) * c - 1:(n + 1) * c, :], (c, MIX_W)) for n in range(n_chunks)],
        axis=0)
    inv = jnp.exp(-cum)
    to_end = jnp.exp(tot - cum)
    r_dec = (r * jnp.exp(cum)).astype(BF16)
    a_dec = (-kk * jnp.exp(cum - logw)).astype(BF16)
    b_inv = (kk * a * inv).astype(BF16)
    k_inv = (kmod * inv).astype(BF16)
    b_end = (kk * a * to_end).astype(BF16)
    k_end = (kmod * to_end).astype(BF16)
    v_b = v.astype(BF16)
    p_end = jnp.exp(tot)

    pair_w = 2 * RW_HEAD
    n_pairs = MIX_W // pair_w
    row_i = lax.broadcasted_iota(jnp.int32, (c, pair_w), 0)
    lane_i = lax.broadcasted_iota(jnp.int32, (c, pair_w), 1)
    left = lane_i < RW_HEAD
    strict = (lane_i % RW_HEAD) < row_i
    incl = (lane_i % RW_HEAD) <= row_i
    eye_fam = ((lane_i % RW_HEAD) == row_i).astype(F32)
    pr = lax.broadcasted_iota(jnp.int32, (pair_w, pair_w), 0)
    pc = lax.broadcasted_iota(jnp.int32, (pair_w, pair_w), 1)
    same_head = (pr // RW_HEAD) == (pc // RW_HEAD)
    eye_pair = (pr == pc).astype(F32)

    def halves(x):
        z = jnp.zeros_like(x)
        return jnp.where(left, x, z), jnp.where(left, z, x)

    def blockdiag(x):
        return jnp.concatenate(halves(x), axis=0)

    probs = [(q, n) for q in range(n_pairs) for n in range(n_chunks)]
    n_probs = len(probs)

    def piece(x, q, n):
        return x[n * c:(n + 1) * c, q * pair_w:(q + 1) * pair_w]

    a_d = [piece(a_dec, q, n) for q, n in probs]
    r_d = [piece(r_dec, q, n) for q, n in probs]
    vv = [piece(v_b, q, n) for q, n in probs]
    vv_bd = [blockdiag(x) for x in vv]
    g = [_mm_nt(jnp.concatenate([a_d[i], r_d[i]], axis=0),
                jnp.concatenate(halves(piece(b_inv, q, n)) + halves(piece(k_inv, q, n)), axis=0))
         for i, (q, n) in enumerate(probs)]
    a_ab = [jnp.where(strict, x[0:c, 0:pair_w], 0.0) for x in g]
    a_ak = [jnp.where(strict, x[0:c, pair_w:2 * pair_w], 0.0).astype(BF16) for x in g]
    lhs_o = [jnp.concatenate([r_d[i], jnp.where(incl, x[c:2 * c, 0:pair_w], 0.0).astype(BF16),
                              jnp.where(incl, x[c:2 * c, pair_w:2 * pair_w], 0.0).astype(BF16)], axis=1)
             for i, x in enumerate(g)]
    tinv = [eye_fam + x for x in a_ab]
    pw = [x.astype(BF16) for x in a_ab]
    pw = [bdot(x, blockdiag(x)).astype(BF16) for x in pw]
    akv = [bdot(a_ak[i], vv_bd[i]).astype(BF16) for i in range(n_probs)]
    n_steps = int(math.log2(c)) - 1
    for s in range(n_steps):
        if s + 1 < n_steps:
            both = [bdot(jnp.concatenate([tinv[i].astype(BF16), pw[i]], axis=0), blockdiag(pw[i]))
                    for i in range(n_probs)]
            tinv = [tinv[i] + both[i][0:c] for i in range(n_probs)]
            pw = [x[c:2 * c].astype(BF16) for x in both]
        else:
            tinv = [tinv[i] + bdot(tinv[i].astype(BF16), blockdiag(pw[i])) for i in range(n_probs)]
    w1u0 = [bdot(tinv[i].astype(BF16), jnp.concatenate([blockdiag(a_d[i]), blockdiag(akv[i])], axis=1))
            for i in range(n_probs)]
    tn = [_mm_tn(piece(b_end, q, n), w1u0[i]) for i, (q, n) in enumerate(probs)]
    kv = [_mm_tn(piece(k_end, q, n), vv[i]) for i, (q, n) in enumerate(probs)]
    trans = [(jnp.where(same_head, tn[i][:, 0:pair_w], 0.0)
              + eye_pair * p_end[n * c:n * c + 1, q * pair_w:(q + 1) * pair_w]).astype(BF16)
             for i, (q, n) in enumerate(probs)]
    add = [jnp.where(same_head, tn[i][:, pair_w:2 * pair_w] + kv[i], 0.0) for i in range(n_probs)]
    w1 = [x[:, 0:pair_w].astype(BF16) for x in w1u0]
    u0 = [x[:, pair_w:2 * pair_w] for x in w1u0]

    st_at = {}
    chains = [(q, b) for q in range(n_pairs) for b in range(nb)]
    st = {qb: st_s[qb[0] * nb + qb[1]] for qb in chains}
    for step in range(chunks_per_row):
        for q, b in chains:
            i = q * n_chunks + b * chunks_per_row + step
            st_b = st[q, b].astype(BF16)
            st_at[i] = st_b
            st[q, b] = bdot(trans[i], st_b) + add[i]
    for q, b in chains:
        st_s[q * nb + b] = st[q, b]
    u = [(bdot(w1[i], st_at[i]) + u0[i]).astype(BF16) for i in range(n_probs)]
    o_p = [bdot(lhs_o[i], jnp.concatenate([st_at[i], blockdiag(u[i]), vv_bd[i]], axis=0))
           for i in range(n_probs)]
    o = jnp.concatenate(
        [jnp.concatenate(o_p[q * n_chunks:(q + 1) * n_chunks], axis=0) for q in range(n_pairs)], axis=1)

    inv_n = 1.0 / RW_HEAD
    mean = _mm(o, head_ones) * inv_n
    dlt = o - mean
    var = _mm(dlt * dlt, head_ones) * inv_n
    o_n = dlt * lax.rsqrt(var + RW_GN_EPS) * lnw_ref[...] + lnb_ref[...]
    bonus = _mm(r * kmod * rk_ref[...], head_ones) * v
    o_ref[...] = ((o_n + bonus) * lora[:, 2 * MIX_W:3 * MIX_W]).reshape(nb, tl, MIX_W)


def _rwkv(p, lora, mu_rkv, k_k, k_a, r_k, ln_w, ln_b, *, batch, seq, tl=256):
    tl = min(tl, seq)
    blocks_per_tile = tl // SUBLANES
    row = lambda v: v.reshape(1, MIX_W)
    p3 = p.reshape(batch, seq, p.shape[1])
    lora3 = lora.reshape(batch, seq, lora.shape[1])

    def cur(col):
        return pl.BlockSpec((batch, tl, MIX_W), lambda i: (0, i, col))

    def prev(col):
        return pl.BlockSpec((batch, SUBLANES, MIX_W),
                            lambda i: (0, jnp.maximum(i * blocks_per_tile - 1, 0), col))

    in_specs = [cur(2), cur(3), cur(4), prev(2), prev(3), prev(4),
                pl.BlockSpec((batch, tl, 3 * MIX_W), lambda i: (0, i, 0)),
                _const_spec((3, MIX_W))] + [_const_spec((1, MIX_W))] * 5
    pair_w = 2 * RW_HEAD
    out = pl.pallas_call(
        _rwkv_body,
        grid=(seq // tl,),
        in_specs=in_specs,
        out_specs=pl.BlockSpec((batch, tl, MIX_W), lambda i: (0, i, 0)),
        out_shape=jax.ShapeDtypeStruct((batch, seq, MIX_W), F32),
        scratch_shapes=[pltpu.VMEM((batch * MIX_W // pair_w, pair_w, pair_w), F32)],
        compiler_params=_params(0, 1),
        name="rwkv",
    )(p3, p3, p3, p3, p3, p3, lora3, mu_rkv, row(k_k), row(k_a), row(r_k), row(ln_w), row(ln_b))
    return out.reshape(batch * seq, MIX_W)


def _merge_body(x_ref, ub_ref, ubprev_ref, z_ref, zprev_ref, bg_ref, cg_ref, cgprev_ref, ya_ref,
                yc_ref, g_ref, wgate01_ref, wgate23_ref, wbr_ref, wout_ref, poolw_ref, pscale_ref,
                convw_ref, o_ref, *, seq):
    i = pl.program_id(0)
    tm, d = x_ref.shape
    at_start = (i * tm) % seq == 0
    x = x_ref[...]
    h = _rmsnorm(x, g_ref[...]).astype(BF16)

    u = ub_ref[...]
    ext = jnp.concatenate([jnp.where(at_start, 0.0, ubprev_ref[...]), u], axis=0)
    sums = {1: ext}
    span = 1
    while span < POOL_WINDOWS[-1]:
        sums[2 * span] = sums[span] + pltpu.roll(sums[span], span, 0)
        span *= 2
    lane = lax.broadcasted_iota(jnp.int32, (tm, MIX_W), 1)
    group = lane // (MIX_W // len(POOL_WINDOWS))
    win_sum = sums[POOL_WINDOWS[-1]][POOL_HALO:, :]
    win = jnp.full((tm, MIX_W), float(POOL_WINDOWS[-1]), F32)
    for gi in range(len(POOL_WINDOWS) - 2, -1, -1):
        win_sum = jnp.where(group == gi, sums[POOL_WINDOWS[gi]][POOL_HALO:, :], win_sum)
        win = jnp.where(group == gi, float(POOL_WINDOWS[gi]), win)
    pos = (i * tm) % seq + lax.broadcasted_iota(jnp.int32, (tm, MIX_W), 0)
    count = jnp.minimum((pos + 1).astype(F32), win)
    y_pool = _mm(win_sum / count - u, poolw_ref[...]) * pscale_ref[...]

    z = cg_ref[...] * z_ref[...]
    z_prev = jnp.where(at_start, 0.0, cgprev_ref[...] * zprev_ref[...])
    zext = jnp.concatenate([z_prev, z], axis=0)
    conv = (convw_ref[2:3, :] * zext
            + convw_ref[1:2, :] * pltpu.roll(zext, 1, 0)
            + convw_ref[0:1, :] * pltpu.roll(zext, 2, 0))
    y_conv = bg_ref[...] * conv[SUBLANES:, :]

    ys = [y.astype(BF16) for y in (ya_ref[...], y_pool, yc_ref[...], y_conv)]
    merged = []
    for j in range(d // MIX_W):
        cols = slice(j * MIX_W, (j + 1) * MIX_W)
        acc = None
        for gi in range(N_BRANCH):
            wgate_ref = (wgate01_ref, wgate23_ref)[gi // 2]
            gcols = slice((gi % 2) * d + j * MIX_W, (gi % 2) * d + (j + 1) * MIX_W)
            gate = jax.nn.sigmoid(jnp.dot(h, wgate_ref[:, gcols], preferred_element_type=F32))
            term = jnp.dot(ys[gi], wbr_ref[gi, :, cols], preferred_element_type=F32) * gate
            acc = term if acc is None else acc + term
        merged.append(acc.astype(BF16))
    o_ref[...] = x + jnp.dot(jnp.concatenate(merged, axis=1), wout_ref[...], preferred_element_type=F32)


def _merge(x, p, y_s5, y_rwkv, norm_g, w_in, w_branch, w_out, layer, pool_w, pool_scale, conv_w, *,
           seq, tm=512):
    t, d = x.shape
    tm = min(tm, seq)
    gate_block = 2 * d
    first_gate_block = 8 * MIX_W // gate_block
    pool_groups = len(POOL_WINDOWS)
    eye = jnp.eye(pool_groups, dtype=F32)
    pool_bd = jnp.einsum("gcd,gk->gckd", pool_w, eye).reshape(MIX_W, MIX_W)

    def cur(col):
        return pl.BlockSpec((tm, MIX_W), lambda i: (i, col))

    def prev(col, rows):
        per_tile = tm // rows
        return pl.BlockSpec((rows, MIX_W), lambda i: (jnp.maximum(i * per_tile - 1, 0), col))

    in_specs = [
        pl.BlockSpec((tm, d), lambda i: (i, 0)),
        cur(1), prev(1, POOL_HALO), cur(5), prev(5, SUBLANES), cur(6), cur(7), prev(7, SUBLANES),
        pl.BlockSpec((tm, MIX_W), lambda i: (i, 0)),
        pl.BlockSpec((tm, MIX_W), lambda i: (i, 0)),
        _const_spec((1, d)),
        _layer_spec(w_in.shape, layer, col_block=(gate_block, first_gate_block)),
        _layer_spec(w_in.shape, layer, col_block=(gate_block, first_gate_block + 1)),
        _layer_spec(w_branch.shape, layer), _layer_spec(w_out.shape, layer),
        _const_spec((MIX_W, MIX_W)), _const_spec((1, MIX_W)), _const_spec(conv_w.shape),
    ]
    return pl.pallas_call(
        functools.partial(_merge_body, seq=seq),
        grid=(t // tm,),
        in_specs=in_specs,
        out_specs=pl.BlockSpec((tm, d), lambda i: (i, 0)),
        out_shape=jax.ShapeDtypeStruct((t, d), F32),
        compiler_params=_params(1),
        name="merge",
    )(x, p, p, p, p, p, p, p, y_s5, y_rwkv, norm_g.reshape(1, d), w_in, w_in, w_branch, w_out,
      pool_bd, pool_scale.reshape(1, MIX_W), conv_w)


def kernel(x, ffn1_norm, ffn1_w_gate, ffn1_w_up, ffn1_w_down, mix_norm, w_in, s5_lambda_re, s5_lambda_im, s5_log_dt, s5_b_re, s5_b_im, s5_c_re, s5_c_im, s5_d, s5_w_glu, pool_w, pool_scale, rwkv_mu_rkv, rwkv_mu_wag, rwkv_w0, rwkv_w1, rwkv_w2, rwkv_a0, rwkv_a1, rwkv_a2, rwkv_g1, rwkv_g2, rwkv_k_k, rwkv_k_a, rwkv_r_k, rwkv_ln_w, rwkv_ln_b, conv_w, w_branch, w_out, ffn2_norm, ffn2_w_gate, ffn2_w_up, ffn2_w_down, final_norm):
    batch, seq, d = x.shape
    depth = w_in.shape[0]
    bf = lambda w: w.astype(BF16)
    ffn1 = (bf(ffn1_w_gate), bf(ffn1_w_up), bf(ffn1_w_down))
    ffn2 = (bf(ffn2_w_gate), bf(ffn2_w_up), bf(ffn2_w_down))
    w_in_b, w_branch_b, w_out_b = bf(w_in), bf(w_branch), bf(w_out)
    xf = x.reshape(batch * seq, d)
    for l in range(depth):
        xf = _ffn(xf, ffn1_norm[l], *ffn1, l)
        p, lora = _proj(xf, mix_norm[l], w_in_b, l, rwkv_mu_wag[l], rwkv_w0[l],
                        bf(rwkv_w1[l]), rwkv_w2[l], rwkv_a0[l], bf(rwkv_a1[l]), rwkv_a2[l],
                        bf(rwkv_g1[l]), rwkv_g2[l], seq=seq)
        y_s5 = _s5(p, s5_lambda_re[l], s5_lambda_im[l], s5_log_dt[l], s5_b_re[l], s5_b_im[l],
                   s5_c_re[l], s5_c_im[l], s5_d[l], s5_w_glu[l], batch=batch, seq=seq)
        y_rwkv = _rwkv(p, lora, rwkv_mu_rkv[l], rwkv_k_k[l], rwkv_k_a[l], rwkv_r_k[l].reshape(-1),
                       rwkv_ln_w[l], rwkv_ln_b[l], batch=batch, seq=seq)
        xf = _merge(xf, p, y_s5, y_rwkv, mix_norm[l], w_in_b, w_branch_b, w_out_b, l,
                    pool_w[l], pool_scale[l], conv_w[l], seq=seq)
        xf = _ffn(xf, ffn2_norm[l], *ffn2, l, final_norm if l == depth - 1 else None)
    return xf.reshape(batch, seq, d)
```

```python
import functools
import math

import jax
import jax.numpy as jnp
from jax import lax
from jax.experimental import pallas as pl
from jax.experimental.pallas import tpu as pltpu

F32 = jnp.float32
BF16 = jnp.bfloat16

MIX_W = 256
N_BRANCH = 4
S5_GROUPS = 16
S5_STATE = 64
POOL_WINDOWS = (2, 4, 8, 16)
POOL_HALO = 16
RW_HEAD = 64
RW_CHUNK = 64
RW_GN_EPS = 64e-5
NORM_EPS = 1e-6
SUBLANES = 8
VMEM_LIMIT_BYTES = 56 * 1024 * 1024


def _params(n_parallel, n_arbitrary=0):
    return pltpu.CompilerParams(
        dimension_semantics=("parallel",) * n_parallel + ("arbitrary",) * n_arbitrary,
        vmem_limit_bytes=VMEM_LIMIT_BYTES)


def _const_spec(shape):
    zeros = (0,) * len(shape)
    return pl.BlockSpec(shape, lambda *_: zeros, pipeline_mode=pl.Buffered(1))


def _layer_spec(stacked_shape, layer, col_block=None):
    shape = tuple(stacked_shape[1:])
    tail = (0,) * (len(shape) - 1)
    if col_block is None:
        index = (layer,) + tail + (0,)
    else:
        width, col = col_block
        shape = shape[:-1] + (width,)
        index = (layer,) + tail + (col,)
    return pl.BlockSpec((None,) + shape, lambda *_: index, pipeline_mode=pl.Buffered(1))


def _rmsnorm(x, g):
    return x * lax.rsqrt(jnp.mean(x * x, axis=-1, keepdims=True) + NORM_EPS) * g


def _mm(a, b):
    return jnp.dot(a.astype(BF16), b.astype(BF16), preferred_element_type=F32)


def _mm_nt(a, b):
    return lax.dot_general(a.astype(BF16), b.astype(BF16), (((1,), (1,)), ((), ())),
                           preferred_element_type=F32)


def _mm_tn(a, b):
    return lax.dot_general(a.astype(BF16), b.astype(BF16), (((0,), (0,)), ((), ())),
                           preferred_element_type=F32)


def _shift_rows(cur, prev_row):
    rolled = pltpu.roll(cur, 1, 0)
    row = lax.broadcasted_iota(jnp.int32, cur.shape, 0)
    return jnp.where(row == 0, prev_row, rolled)


def _softplus(z):
    return jnp.maximum(z, 0.0) + jnp.log1p(jnp.exp(-jnp.abs(z)))


def _ffn_body(x_ref, g_ref, wg_ref, wu_ref, wd_ref, *rest, f_chunk, final):
    if final:
        fg_ref, o_ref, acc_ref = rest
    else:
        o_ref, acc_ref = rest
    x = x_ref[...]
    h = _rmsnorm(x, g_ref[...]).astype(BF16)
    d_ff = wg_ref.shape[1]
    for j in range(d_ff // f_chunk):
        sl = slice(j * f_chunk, (j + 1) * f_chunk)
        gate = jnp.dot(h, wg_ref[:, sl], preferred_element_type=F32)
        up = jnp.dot(h, wu_ref[:, sl], preferred_element_type=F32)
        act = (gate * jax.nn.sigmoid(gate) * up).astype(BF16)
        contrib = jnp.dot(act, wd_ref[sl, :], preferred_element_type=F32)
        if j == 0:
            acc_ref[...] = contrib
        else:
            acc_ref[...] += contrib
    y = x + 0.5 * acc_ref[...]
    if final:
        y = _rmsnorm(y, fg_ref[...])
    o_ref[...] = y


def _ffn(x, norm_g, w_gate, w_up, w_down, layer, final_g=None, *, tm=1024, f_chunk=256):
    t, d = x.shape
    tm = min(tm, t)
    final = final_g is not None
    in_specs = [
        pl.BlockSpec((tm, d), lambda i: (i, 0)),
        _const_spec((1, d)),
        _layer_spec(w_gate.shape, layer),
        _layer_spec(w_up.shape, layer),
        _layer_spec(w_down.shape, layer),
    ]
    args = [x, norm_g.reshape(1, d), w_gate, w_up, w_down]
    if final:
        in_specs.append(_const_spec((1, d)))
        args.append(final_g.reshape(1, d))
    return pl.pallas_call(
        functools.partial(_ffn_body, f_chunk=f_chunk, final=final),
        grid=(t // tm,),
        in_specs=in_specs,
        out_specs=pl.BlockSpec((tm, d), lambda i: (i, 0)),
        out_shape=jax.ShapeDtypeStruct((t, d), F32),
        scratch_shapes=[pltpu.VMEM((tm, d), F32)],
        compiler_params=_params(1),
        name="ffn_final" if final else "ffn",
    )(*args)


def _proj_body(x_ref, xprev_ref, g_ref, win_ref, mu_ref, w0_ref, w1_ref, w2_ref, a0_ref, a1_ref,
               a2_ref, g1_ref, g2_ref, p_ref, lora_ref, *, seq):
    i = pl.program_id(0)
    tm = x_ref.shape[0]
    g = g_ref[...]
    h = _rmsnorm(x_ref[...], g)
    at_start = (i * tm) % seq == 0
    h_prev = _rmsnorm(xprev_ref[SUBLANES - 1:SUBLANES, :], g)
    h_prev = jnp.where(at_start, 0.0, h_prev)
    hx = _shift_rows(h, h_prev) - h
    hb = h.astype(BF16)

    def main_cols(lo, hi):
        for j in range(lo, hi):
            cols = slice(j * MIX_W, (j + 1) * MIX_W)
            p_ref[:, cols] = jnp.dot(hb, win_ref[:, cols], preferred_element_type=F32)

    main_cols(0, 3)
    xw = (h + hx * mu_ref[0:1, :]).astype(BF16)
    t_w = jnp.tanh(jnp.dot(xw, w1_ref[...], preferred_element_type=F32))
    main_cols(3, 5)
    xa = (h + hx * mu_ref[1:2, :]).astype(BF16)
    t_a = jnp.dot(xa, a1_ref[...], preferred_element_type=F32)
    main_cols(5, 7)
    xg = (h + hx * mu_ref[2:3, :]).astype(BF16)
    t_g = jax.nn.sigmoid(jnp.dot(xg, g1_ref[...], preferred_element_type=F32))
    main_cols(7, 8)
    w_log = -_softplus(-(w0_ref[...] + _mm(t_w, w2_ref[...]))) - 0.5
    lora_ref[:, 0:MIX_W] = -jnp.exp(w_log)
    lora_ref[:, MIX_W:2 * MIX_W] = jax.nn.sigmoid(a0_ref[...] + _mm(t_a, a2_ref[...]))
    lora_ref[:, 2 * MIX_W:3 * MIX_W] = _mm(t_g, g2_ref[...])


def _proj(x, norm_g, w_in, layer, mu_wag, w0, w1, w2, a0, a1, a2, g1, g2, *, seq, tm=1024):
    t, d = x.shape
    tm = min(tm, seq)
    n_small = 8 * MIX_W
    blocks_per_tile = tm // SUBLANES
    in_specs = [
        pl.BlockSpec((tm, d), lambda i: (i, 0)),
        pl.BlockSpec((SUBLANES, d), lambda i: (jnp.maximum(i * blocks_per_tile - 1, 0), 0)),
        _const_spec((1, d)),
        _layer_spec(w_in.shape, layer, col_block=(n_small, 0)),
        _const_spec(mu_wag.shape),
        _const_spec((1, MIX_W)),
        _const_spec(w1.shape),
        _const_spec(w2.shape),
        _const_spec((1, MIX_W)),
        _const_spec(a1.shape),
        _const_spec(a2.shape),
        _const_spec(g1.shape),
        _const_spec(g2.shape),
    ]
    return pl.pallas_call(
        functools.partial(_proj_body, seq=seq),
        grid=(t // tm,),
        in_specs=in_specs,
        out_specs=[pl.BlockSpec((tm, n_small), lambda i: (i, 0)),
                   pl.BlockSpec((tm, 3 * MIX_W), lambda i: (i, 0))],
        out_shape=[jax.ShapeDtypeStruct((t, n_small), F32),
                   jax.ShapeDtypeStruct((t, 3 * MIX_W), F32)],
        compiler_params=_params(1),
        name="proj",
    )(x, x, norm_g.reshape(1, d), w_in, mu_wag, w0.reshape(1, MIX_W), w1, w2,
      a0.reshape(1, MIX_W), a1, a2, g1, g2)


def _s5_body(u_ref, lre_ref, lim_ref, ldt_ref, bre_ref, bim_ref, cre_ref, cim_ref, d_ref, wglu_ref,
             o_ref, wre_s, wim_s, abar_s, xre_s, xim_s, hre_s, him_s, st_s, *, unroll):
    @pl.when(pl.program_id(0) == 0)
    def _():
        lr = lre_ref[...]
        li = lim_ref[...]
        dt = jnp.exp(ldt_ref[...])
        mag = jnp.exp(lr * dt)
        ar = mag * jnp.cos(li * dt)
        ai = mag * jnp.sin(li * dt)
        inv = 1.0 / (lr * lr + li * li)
        qr, qi = lr * inv, -li * inv
        coef_re = (ar - 1.0) * qr - ai * qi
        coef_im = (ar - 1.0) * qi + ai * qr
        wre_s[...] = (coef_re * bre_ref[...] - coef_im * bim_ref[...]).astype(BF16)
        wim_s[...] = (coef_re * bim_ref[...] + coef_im * bre_ref[...]).astype(BF16)
        abar_s[0:1, :] = ar
        abar_s[1:2, :] = ai
        st_s[...] = jnp.zeros_like(st_s)

    nb, tl = u_ref.shape[0], u_ref.shape[1]
    rows, n_state = nb * tl, wre_s.shape[1]
    u = u_ref[...].reshape(rows, MIX_W)
    ub = u.astype(BF16)
    blocks_per_row = tl // SUBLANES
    blocks = (nb * blocks_per_row, SUBLANES, n_state)
    xre_s[...] = jnp.dot(ub, wre_s[...], preferred_element_type=F32).reshape(blocks)
    xim_s[...] = jnp.dot(ub, wim_s[...], preferred_element_type=F32).reshape(blocks)
    ar = abar_s[0:1, :]
    ai = abar_s[1:2, :]

    def step(j, carry):
        carry = list(carry)
        for q in range(SUBLANES):
            for b in range(nb):
                blk = b * blocks_per_row + j
                hr, hi = carry[b]
                nr = ar * hr - ai * hi + xre_s[blk, q:q + 1, :]
                ni = ar * hi + ai * hr + xim_s[blk, q:q + 1, :]
                hre_s[blk, q:q + 1, :] = nr
                him_s[blk, q:q + 1, :] = ni
                carry[b] = (nr, ni)
        return tuple(carry)

    init = tuple((st_s[2 * b:2 * b + 1, :], st_s[2 * b + 1:2 * b + 2, :]) for b in range(nb))
    final = lax.fori_loop(0, blocks_per_row, step, init, unroll=unroll)
    for b in range(nb):
        st_s[2 * b:2 * b + 1, :] = final[b][0]
        st_s[2 * b + 1:2 * b + 2, :] = final[b][1]

    y = (_mm(hre_s[...].reshape(rows, n_state), cre_ref[...])
         - _mm(him_s[...].reshape(rows, n_state), cim_ref[...]) + d_ref[...] * u)
    gl = jax.nn.gelu(y, approximate=True)
    o_ref[...] = (gl * jax.nn.sigmoid(_mm(gl, wglu_ref[...]))).reshape(nb, tl, MIX_W)


def _s5(p, lam_re, lam_im, log_dt, b_re, b_im, c_re, c_im, d_skip, w_glu, *, batch, seq, tl=256):
    n_state = S5_GROUPS * S5_STATE
    tl = min(tl, seq)
    eye = jnp.eye(S5_GROUPS, dtype=F32)

    def b_blockdiag(b):
        return jnp.einsum("gph,gk->ghkp", b, eye).reshape(MIX_W, n_state)

    def c_blockdiag(c):
        return jnp.einsum("ghp,gk->gpkh", c, eye).reshape(n_state, MIX_W)

    row = lambda v: v.reshape(1, -1)
    in_specs = [
        pl.BlockSpec((batch, tl, MIX_W), lambda i: (0, i, 0)),
        _const_spec((1, n_state)), _const_spec((1, n_state)), _const_spec((1, n_state)),
        _const_spec((MIX_W, n_state)), _const_spec((MIX_W, n_state)),
        _const_spec((n_state, MIX_W)), _const_spec((n_state, MIX_W)),
        _const_spec((1, MIX_W)), _const_spec((MIX_W, MIX_W)),
    ]
    scan_buf = pltpu.VMEM((batch * tl // SUBLANES, SUBLANES, n_state), F32)
    out = pl.pallas_call(
        functools.partial(_s5_body, unroll=2),
        grid=(seq // tl,),
        in_specs=in_specs,
        out_specs=pl.BlockSpec((batch, tl, MIX_W), lambda i: (0, i, 0)),
        out_shape=jax.ShapeDtypeStruct((batch, seq, MIX_W), F32),
        scratch_shapes=[pltpu.VMEM((MIX_W, n_state), BF16), pltpu.VMEM((MIX_W, n_state), BF16),
                        pltpu.VMEM((2, n_state), F32)]
                       + [scan_buf] * 4 + [pltpu.VMEM((2 * batch, n_state), F32)],
        compiler_params=_params(0, 1),
        name="s5",
    )(p.reshape(batch, seq, p.shape[1]), row(lam_re), row(lam_im),
      row(jnp.broadcast_to(log_dt[:, None], (S5_GROUPS, S5_STATE))),
      b_blockdiag(b_re), b_blockdiag(b_im), c_blockdiag(c_re).astype(BF16),
      c_blockdiag(c_im).astype(BF16), row(d_skip), w_glu.astype(BF16))
    return out.reshape(batch * seq, MIX_W)


def _segment_ones(n, seg):
    r = lax.broadcasted_iota(jnp.int32, (n, n), 0) // seg
    c = lax.broadcasted_iota(jnp.int32, (n, n), 1) // seg
    return r == c


def _rwkv_body(rp_ref, kp_ref, vp_ref, rprev_ref, kprev_ref, vprev_ref, lora_ref, mu_ref, kk_ref,
               ka_ref, rk_ref, lnw_ref, lnb_ref, o_ref, st_s):
    nb, tl = rp_ref.shape[0], rp_ref.shape[1]
    rows = nb * tl
    c = RW_CHUNK
    first = pl.program_id(0) == 0

    @pl.when(first)
    def _():
        st_s[...] = jnp.zeros_like(st_s)

    row_id = lax.broadcasted_iota(jnp.int32, (rows, MIX_W), 0)

    def mixed(cur_ref, prev_ref, mu):
        cur = cur_ref[...].reshape(rows, MIX_W)
        shifted = pltpu.roll(cur, 1, 0)
        for b in range(nb):
            prev_row = jnp.where(first, 0.0, prev_ref[b, SUBLANES - 1:SUBLANES, :])
            shifted = jnp.where(row_id == b * tl, prev_row, shifted)
        return cur + (shifted - cur) * mu

    r = mixed(rp_ref, rprev_ref, mu_ref[0:1, :])
    k = mixed(kp_ref, kprev_ref, mu_ref[1:2, :])
    v = mixed(vp_ref, vprev_ref, mu_ref[2:3, :])
    lora = lora_ref[...].reshape(rows, 3 * MIX_W)
    logw = lora[:, 0:MIX_W]
    a = lora[:, MIX_W:2 * MIX_W]
    head_ones = _segment_ones(MIX_W, RW_HEAD).astype(BF16)
    kraw = k * kk_ref[...]
    kk = kraw / jnp.maximum(jnp.sqrt(_mm(kraw * kraw, head_ones)), 1e-12)
    kmod = k * (1.0 + (a - 1.0) * ka_ref[...])

    bdot = functools.partial(jnp.dot, preferred_element_type=F32)
    n_chunks = rows // c
    chunks_per_row = tl // c

    ti = lax.broadcasted_iota(jnp.int32, (rows, rows), 0)
    tj = lax.broadcasted_iota(jnp.int32, (rows, rows), 1)
    tri = (((ti // c) == (tj // c)) & (tj <= ti)).astype(BF16)
    w_hi = logw.astype(BF16)
    rem = logw - w_hi.astype(F32)
    w_mid = rem.astype(BF16)
    w_lo = (rem - w_mid.astype(F32)).astype(BF16)
    cum = bdot(tri, w_hi) + bdot(tri, w_mid) + bdot(tri, w_lo)
    tot = jnp.concatenate(
        [jnp.broadcast_to(cum[(n + 1) * c - 1:(n + 1) * c, :], (c, MIX_W)) for n in range(n_chunks)],
        axis=0)
    inv = jnp.exp(-cum)
    to_end = jnp.exp(tot - cum)
    r_dec = (r * jnp.exp(cum)).astype(BF16)
    a_dec = (-kk * jnp.exp(cum - logw)).astype(BF16)
    b_inv = (kk * a * inv).astype(BF16)
    k_inv = (kmod * inv).astype(BF16)
    b_end = (kk * a * to_end).astype(BF16)
    k_end = (kmod * to_end).astype(BF16)
    v_b = v.astype(BF16)
    p_end = jnp.exp(tot)

    pair_w = 2 * RW_HEAD
    n_pairs = MIX_W // pair_w
    row_i = lax.broadcasted_iota(jnp.int32, (c, pair_w), 0)
    lane_i = lax.broadcasted_iota(jnp.int32, (c, pair_w), 1)
    left = lane_i < RW_HEAD
    strict = (lane_i % RW_HEAD) < row_i
    incl = (lane_i % RW_HEAD) <= row_i
    eye_fam = ((lane_i % RW_HEAD) == row_i).astype(F32)
    pr = lax.broadcasted_iota(jnp.int32, (pair_w, pair_w), 0)
    pc = lax.broadcasted_iota(jnp.int32, (pair_w, pair_w), 1)
    same_head = (pr // RW_HEAD) == (pc // RW_HEAD)
    eye_pair = (pr == pc).astype(F32)

    def halves(x):
        z = jnp.zeros_like(x)
        return jnp.where(left, x, z), jnp.where(left, z, x)

    def blockdiag(x):
        return jnp.concatenate(halves(x), axis=0)

    probs = [(q, n) for q in range(n_pairs) for n in range(n_chunks)]
    n_probs = len(probs)

    def piece(x, q, n):
        return x[n * c:(n + 1) * c, q * pair_w:(q + 1) * pair_w]

    a_d = [piece(a_dec, q, n) for q, n in probs]
    r_d = [piece(r_dec, q, n) for q, n in probs]
    vv = [piece(v_b, q, n) for q, n in probs]
    vv_bd = [blockdiag(x) for x in vv]
    g = [_mm_nt(jnp.concatenate([a_d[i], r_d[i]], axis=0),
                jnp.concatenate(halves(piece(b_inv, q, n)) + halves(piece(k_inv, q, n)), axis=0))
         for i, (q, n) in enumerate(probs)]
    a_ab = [jnp.where(strict, x[0:c, 0:pair_w], 0.0) for x in g]
    a_ak = [jnp.where(strict, x[0:c, pair_w:2 * pair_w], 0.0).astype(BF16) for x in g]
    lhs_o = [jnp.concatenate([r_d[i], jnp.where(incl, x[c:2 * c, 0:pair_w], 0.0).astype(BF16),
                              jnp.where(incl, x[c:2 * c, pair_w:2 * pair_w], 0.0).astype(BF16)], axis=1)
             for i, x in enumerate(g)]
    tinv = [eye_fam + x for x in a_ab]
    pw = [x.astype(BF16) for x in a_ab]
    pw = [bdot(x, blockdiag(x)).astype(BF16) for x in pw]
    akv = [bdot(a_ak[i], vv_bd[i]).astype(BF16) for i in range(n_probs)]
    n_steps = int(math.log2(c)) - 1
    for s in range(n_steps):
        if s + 1 < n_steps:
            both = [bdot(jnp.concatenate([tinv[i].astype(BF16), pw[i]], axis=0), blockdiag(pw[i]))
                    for i in range(n_probs)]
            tinv = [tinv[i] + both[i][0:c] for i in range(n_probs)]
            pw = [x[c:2 * c].astype(BF16) for x in both]
        else:
            tinv = [tinv[i] + bdot(tinv[i].astype(BF16), blockdiag(pw[i])) for i in range(n_probs)]
    w1u0 = [bdot(tinv[i].astype(BF16), jnp.concatenate([blockdiag(a_d[i]), blockdiag(akv[i])], axis=1))
            for i in range(n_probs)]
    tn = [_mm_tn(piece(b_end, q, n), w1u0[i]) for i, (q, n) in enumerate(probs)]
    kv = [_mm_tn(piece(k_end, q, n), vv[i]) for i, (q, n) in enumerate(probs)]
    trans = [(jnp.where(same_head, tn[i][:, 0:pair_w], 0.0)
              + eye_pair * p_end[n * c:n * c + 1, q * pair_w:(q + 1) * pair_w]).astype(BF16)
             for i, (q, n) in enumerate(probs)]
    add = [jnp.where(same_head, tn[i][:, pair_w:2 * pair_w] + kv[i], 0.0) for i in range(n_probs)]
    w1 = [x[:, 0:pair_w].astype(BF16) for x in w1u0]
    u0 = [x[:, pair_w:2 * pair_w] for x in w1u0]

    st_at = {}
    chains = [(q, b) for q in range(n_pairs) for b in range(nb)]
    st = {qb: st_s[qb[0] * nb + qb[1]] for qb in chains}
    for step in range(chunks_per_row):
        for q, b in chains:
            i = q * n_chunks + b * chunks_per_row + step
            st_b = st[q, b].astype(BF16)
            st_at[i] = st_b
            st[q, b] = bdot(trans[i], st_b) + add[i]
    for q, b in chains:
        st_s[q * nb + b] = st[q, b]
    u = [(bdot(w1[i], st_at[i]) + u0[i]).astype(BF16) for i in range(n_probs)]
    o_p = [bdot(lhs_o[i], jnp.concatenate([st_at[i], blockdiag(u[i]), vv_bd[i]], axis=0))
           for i in range(n_probs)]
    o = jnp.concatenate(
        [jnp.concatenate(o_p[q * n_chunks:(q + 1) * n_chunks], axis=0) for q in range(n_pairs)], axis=1)

    inv_n = 1.0 / RW_HEAD
    mean = _mm(o, head_ones) * inv_n
    dlt = o - mean
    var = _mm(dlt * dlt, head_ones) * inv_n
    o_n = dlt * lax.rsqrt(var + RW_GN_EPS) * lnw_ref[...] + lnb_ref[...]
    bonus = _mm(r * kmod * rk_ref[...], head_ones) * v
    o_ref[...] = ((o_n + bonus) * lora[:, 2 * MIX_W:3 * MIX_W]).reshape(nb, tl, MIX_W)


def _rwkv(p, lora, mu_rkv, k_k, k_a, r_k, ln_w, ln_b, *, batch, seq, tl=256):
    tl = min(tl, seq)
    blocks_per_tile = tl // SUBLANES
    row = lambda v: v.reshape(1, MIX_W)
    p3 = p.reshape(batch, seq, p.shape[1])
    lora3 = lora.reshape(batch, seq, lora.shape[1])

    def cur(col):
        return pl.BlockSpec((batch, tl, MIX_W), lambda i: (0, i, col))

    def prev(col):
        return pl.BlockSpec((batch, SUBLANES, MIX_W),
                            lambda i: (0, jnp.maximum(i * blocks_per_tile - 1, 0), col))

    in_specs = [cur(2), cur(3), cur(4), prev(2), prev(3), prev(4),
                pl.BlockSpec((batch, tl, 3 * MIX_W), lambda i: (0, i, 0)),
                _const_spec((3, MIX_W))] + [_const_spec((1, MIX_W))] * 5
    pair_w = 2 * RW_HEAD
    out = pl.pallas_call(
        _rwkv_body,
        grid=(seq // tl,),
        in_specs=in_specs,
        out_specs=pl.BlockSpec((batch, tl, MIX_W), lambda i: (0, i, 0)),
        out_shape=jax.ShapeDtypeStruct((batch, seq, MIX_W), F32),
        scratch_shapes=[pltpu.VMEM((batch * MIX_W // pair_w, pair_w, pair_w), F32)],
        compiler_params=_params(0, 1),
        name="rwkv",
    )(p3, p3, p3, p3, p3, p3, lora3, mu_rkv, row(k_k), row(k_a), row(r_k), row(ln_w), row(ln_b))
    return out.reshape(batch * seq, MIX_W)


def _merge_body(x_ref, ub_ref, ubprev_ref, z_ref, zprev_ref, bg_ref, cg_ref, cgprev_ref, ya_ref,
                yc_ref, g_ref, wgate01_ref, wgate23_ref, wbr_ref, wout_ref, poolw_ref, pscale_ref,
                convw_ref, o_ref, *, seq):
    i = pl.program_id(0)
    tm, d = x_ref.shape
    at_start = (i * tm) % seq == 0
    x = x_ref[...]
    h = _rmsnorm(x, g_ref[...]).astype(BF16)

    u = ub_ref[...]
    ext = jnp.concatenate([jnp.where(at_start, 0.0, ubprev_ref[...]), u], axis=0)
    sums = {1: ext}
    span = 1
    while span < POOL_WINDOWS[-1]:
        sums[2 * span] = sums[span] + pltpu.roll(sums[span], span, 0)
        span *= 2
    lane = lax.broadcasted_iota(jnp.int32, (tm, MIX_W), 1)
    group = lane // (MIX_W // len(POOL_WINDOWS))
    win_sum = sums[POOL_WINDOWS[-1]][POOL_HALO:, :]
    win = jnp.full((tm, MIX_W), float(POOL_WINDOWS[-1]), F32)
    for gi in range(len(POOL_WINDOWS) - 2, -1, -1):
        win_sum = jnp.where(group == gi, sums[POOL_WINDOWS[gi]][POOL_HALO:, :], win_sum)
        win = jnp.where(group == gi, float(POOL_WINDOWS[gi]), win)
    pos = (i * tm) % seq + lax.broadcasted_iota(jnp.int32, (tm, MIX_W), 0)
    count = jnp.minimum((pos + 1).astype(F32), win)
    y_pool = _mm(win_sum / count - u, poolw_ref[...]) * pscale_ref[...]

    z = cg_ref[...] * z_ref[...]
    z_prev = jnp.where(at_start, 0.0, cgprev_ref[...] * zprev_ref[...])
    zext = jnp.concatenate([z_prev, z], axis=0)
    conv = (convw_ref[2:3, :] * zext
            + convw_ref[1:2, :] * pltpu.roll(zext, 1, 0)
            + convw_ref[0:1, :] * pltpu.roll(zext, 2, 0))
    y_conv = bg_ref[...] * conv[SUBLANES:, :]

    ys = [y.astype(BF16) for y in (ya_ref[...], y_pool, yc_ref[...], y_conv)]
    merged = []
    for j in range(d // MIX_W):
        cols = slice(j * MIX_W, (j + 1) * MIX_W)
        acc = None
        for gi in range(N_BRANCH):
            wgate_ref = (wgate01_ref, wgate23_ref)[gi // 2]
            gcols = slice((gi % 2) * d + j * MIX_W, (gi % 2) * d + (j + 1) * MIX_W)
            gate = jax.nn.sigmoid(jnp.dot(h, wgate_ref[:, gcols], preferred_element_type=F32))
            term = jnp.dot(ys[gi], wbr_ref[gi, :, cols], preferred_element_type=F32) * gate
            acc = term if acc is None else acc + term
        merged.append(acc.astype(BF16))
    o_ref[...] = x + jnp.dot(jnp.concatenate(merged, axis=1), wout_ref[...], preferred_element_type=F32)


def _merge(x, p, y_s5, y_rwkv, norm_g, w_in, w_branch, w_out, layer, pool_w, pool_scale, conv_w, *,
           seq, tm=512):
    t, d = x.shape
    tm = min(tm, seq)
    gate_block = 2 * d
    first_gate_block = 8 * MIX_W // gate_block
    pool_groups = len(POOL_WINDOWS)
    eye = jnp.eye(pool_groups, dtype=F32)
    pool_bd = jnp.einsum("gcd,gk->gckd", pool_w, eye).reshape(MIX_W, MIX_W)

    def cur(col):
        return pl.BlockSpec((tm, MIX_W), lambda i: (i, col))

    def prev(col, rows):
        per_tile = tm // rows
        return pl.BlockSpec((rows, MIX_W), lambda i: (jnp.maximum(i * per_tile - 1, 0), col))

    in_specs = [
        pl.BlockSpec((tm, d), lambda i: (i, 0)),
        cur(1), prev(1, POOL_HALO), cur(5), prev(5, SUBLANES), cur(6), cur(7), prev(7, SUBLANES),
        pl.BlockSpec((tm, MIX_W), lambda i: (i, 0)),
        pl.BlockSpec((tm, MIX_W), lambda i: (i, 0)),
        _const_spec((1, d)),
        _layer_spec(w_in.shape, layer, col_block=(gate_block, first_gate_block)),
        _layer_spec(w_in.shape, layer, col_block=(gate_block, first_gate_block + 1)),
        _layer_spec(w_branch.shape, layer), _layer_spec(w_out.shape, layer),
        _const_spec((MIX_W, MIX_W)), _const_spec((1, MIX_W)), _const_spec(conv_w.shape),
    ]
    return pl.pallas_call(
        functools.partial(_merge_body, seq=seq),
        grid=(t // tm,),
        in_specs=in_specs,
        out_specs=pl.BlockSpec((tm, d), lambda i: (i, 0)),
        out_shape=jax.ShapeDtypeStruct((t, d), F32),
        compiler_params=_params(1),
        name="merge",
    )(x, p, p, p, p, p, p, p, y_s5, y_rwkv, norm_g.reshape(1, d), w_in, w_in, w_branch, w_out,
      pool_bd, pool_scale.reshape(1, MIX_W), conv_w)


def kernel(x, ffn1_norm, ffn1_w_gate, ffn1_w_up, ffn1_w_down, mix_norm, w_in, s5_lambda_re, s5_lambda_im, s5_log_dt, s5_b_re, s5_b_im, s5_c_re, s5_c_im, s5_d, s5_w_glu, pool_w, pool_scale, rwkv_mu_rkv, rwkv_mu_wag, rwkv_w0, rwkv_w1, rwkv_w2, rwkv_a0, rwkv_a1, rwkv_a2, rwkv_g1, rwkv_g2, rwkv_k_k, rwkv_k_a, rwkv_r_k, rwkv_ln_w, rwkv_ln_b, conv_w, w_branch, w_out, ffn2_norm, ffn2_w_gate, ffn2_w_up, ffn2_w_down, final_norm):
    batch, seq, d = x.shape
    depth = w_in.shape[0]
    bf = lambda w: w.astype(BF16)
    ffn1 = (bf(ffn1_w_gate), bf(ffn1_w_up), bf(ffn1_w_down))
    ffn2 = (bf(ffn2_w_gate), bf(ffn2_w_up), bf(ffn2_w_down))
    w_in_b, w_branch_b, w_out_b = bf(w_in), bf(w_branch), bf(w_out)
    xf = x.reshape(batch * seq, d)
    for l in range(depth):
        xf = _ffn(xf, ffn1_norm[l], *ffn1, l)
        p, lora = _proj(xf, mix_norm[l], w_in_b, l, rwkv_mu_wag[l], rwkv_w0[l],
                        bf(rwkv_w1[l]), rwkv_w2[l], rwkv_a0[l], bf(rwkv_a1[l]), rwkv_a2[l],
                        bf(rwkv_g1[l]), rwkv_g2[l], seq=seq)
        y_s5 = _s5(p, s5_lambda_re[l], s5_lambda_im[l], s5_log_dt[l], s5_b_re[l], s5_b_im[l],
                   s5_c_re[l], s5_c_im[l], s5_d[l], s5_w_glu[l], batch=batch, seq=seq)
        y_rwkv = _rwkv(p, lora, rwkv_mu_rkv[l], rwkv_k_k[l], rwkv_k_a[l], rwkv_r_k[l].reshape(-1),
                       rwkv_ln_w[l], rwkv_ln_b[l], batch=batch, seq=seq)
        xf = _merge(xf, p, y_s5, y_rwkv, mix_norm[l], w_in_b, w_branch_b, w_out_b, l,
                    pool_w[l], pool_scale[l], conv_w[l], seq=seq)
        xf = _ffn(xf, ffn2_norm[l], *ffn2, l, final_norm if l == depth - 1 else None)
    return xf.reshape(batch, seq, d)
```

```python
import functools
import math

import jax
import jax.numpy as jnp
from jax import lax
from jax.experimental import pallas as pl
from jax.experimental.pallas import tpu as pltpu

F32 = jnp.float32
BF16 = jnp.bfloat16

MIX_W = 256
N_BRANCH = 4
S5_GROUPS = 16
S5_STATE = 64
POOL_WINDOWS = (2, 4, 8, 16)
POOL_HALO = 16
RW_HEAD = 64
RW_CHUNK = 64
RW_GN_EPS = 64e-5
NORM_EPS = 1e-6
SUBLANES = 8
VMEM_LIMIT_BYTES = 56 * 1024 * 1024


def _params(n_parallel, n_arbitrary=0):
    return pltpu.CompilerParams(
        dimension_semantics=("parallel",) * n_parallel + ("arbitrary",) * n_arbitrary,
        vmem_limit_bytes=VMEM_LIMIT_BYTES)


def _const_spec(shape):
    zeros = (0,) * len(shape)
    return pl.BlockSpec(shape, lambda *_: zeros, pipeline_mode=pl.Buffered(1))


def _layer_spec(stacked_shape, layer, col_block=None):
    shape = tuple(stacked_shape[1:])
    tail = (0,) * (len(shape) - 1)
    if col_block is None:
        index = (layer,) + tail + (0,)
    else:
        width, col = col_block
        shape = shape[:-1] + (width,)
        index = (layer,) + tail + (col,)
    return pl.BlockSpec((None,) + shape, lambda *_: index, pipeline_mode=pl.Buffered(1))


def _rmsnorm(x, g):
    return x * lax.rsqrt(jnp.mean(x * x, axis=-1, keepdims=True) + NORM_EPS) * g


def _mm(a, b):
    return jnp.dot(a.astype(BF16), b.astype(BF16), preferred_element_type=F32)


def _mm_nt(a, b):
    return lax.dot_general(a.astype(BF16), b.astype(BF16), (((1,), (1,)), ((), ())),
                           preferred_element_type=F32)


def _mm_tn(a, b):
    return lax.dot_general(a.astype(BF16), b.astype(BF16), (((0,), (0,)), ((), ())),
                           preferred_element_type=F32)


def _shift_rows(cur, prev_row):
    rolled = pltpu.roll(cur, 1, 0)
    row = lax.broadcasted_iota(jnp.int32, cur.shape, 0)
    return jnp.where(row == 0, prev_row, rolled)


def _softplus(z):
    return jnp.maximum(z, 0.0) + jnp.log1p(jnp.exp(-jnp.abs(z)))


def _ffn_body(x_ref, g_ref, wg_ref, wu_ref, wd_ref, *rest, f_chunk, final):
    if final:
        fg_ref, o_ref, acc_ref = rest
    else:
        o_ref, acc_ref = rest
    x = x_ref[...]
    h = _rmsnorm(x, g_ref[...]).astype(BF16)
    d_ff = wg_ref.shape[1]
    for j in range(d_ff // f_chunk):
        sl = slice(j * f_chunk, (j + 1) * f_chunk)
        gate = jnp.dot(h, wg_ref[:, sl], preferred_element_type=F32)
        up = jnp.dot(h, wu_ref[:, sl], preferred_element_type=F32)
        act = (gate * jax.nn.sigmoid(gate) * up).astype(BF16)
        contrib = jnp.dot(act, wd_ref[sl, :], preferred_element_type=F32)
        if j == 0:
            acc_ref[...] = contrib
        else:
            acc_ref[...] += contrib
    y = x + 0.5 * acc_ref[...]
    if final:
        y = _rmsnorm(y, fg_ref[...])
    o_ref[...] = y


def _ffn(x, norm_g, w_gate, w_up, w_down, layer, final_g=None, *, tm=1024, f_chunk=256):
    t, d = x.shape
    tm = min(tm, t)
    final = final_g is not None
    in_specs = [
        pl.BlockSpec((tm, d), lambda i: (i, 0)),
        _const_spec((1, d)),
        _layer_spec(w_gate.shape, layer),
        _layer_spec(w_up.shape, layer),
        _layer_spec(w_down.shape, layer),
    ]
    args = [x, norm_g.reshape(1, d), w_gate, w_up, w_down]
    if final:
        in_specs.append(_const_spec((1, d)))
        args.append(final_g.reshape(1, d))
    return pl.pallas_call(
        functools.partial(_ffn_body, f_chunk=f_chunk, final=final),
        grid=(t // tm,),
        in_specs=in_specs,
        out_specs=pl.BlockSpec((tm, d), lambda i: (i, 0)),
        out_shape=jax.ShapeDtypeStruct((t, d), F32),
        scratch_shapes=[pltpu.VMEM((tm, d), F32)],
        compiler_params=_params(1),
        name="ffn_final" if final else "ffn",
    )(*args)


def _proj_body(x_ref, xprev_ref, g_ref, win_ref, mu_ref, w0_ref, w1_ref, w2_ref, a0_ref, a1_ref,
               a2_ref, g1_ref, g2_ref, p_ref, lora_ref, *, seq):
    i = pl.program_id(0)
    tm = x_ref.shape[0]
    g = g_ref[...]
    h = _rmsnorm(x_ref[...], g)
    at_start = (i * tm) % seq == 0
    h_prev = _rmsnorm(xprev_ref[SUBLANES - 1:SUBLANES, :], g)
    h_prev = jnp.where(at_start, 0.0, h_prev)
    hx = _shift_rows(h, h_prev) - h
    hb = h.astype(BF16)

    def main_cols(lo, hi):
        for j in range(lo, hi):
            cols = slice(j * MIX_W, (j + 1) * MIX_W)
            p_ref[:, cols] = jnp.dot(hb, win_ref[:, cols], preferred_element_type=F32)

    main_cols(0, 3)
    xw = (h + hx * mu_ref[0:1, :]).astype(BF16)
    t_w = jnp.tanh(jnp.dot(xw, w1_ref[...], preferred_element_type=F32))
    main_cols(3, 5)
    xa = (h + hx * mu_ref[1:2, :]).astype(BF16)
    t_a = jnp.dot(xa, a1_ref[...], preferred_element_type=F32)
    main_cols(5, 7)
    xg = (h + hx * mu_ref[2:3, :]).astype(BF16)
    t_g = jax.nn.sigmoid(jnp.dot(xg, g1_ref[...], preferred_element_type=F32))
    main_cols(7, 8)
    w_log = -_softplus(-(w0_ref[...] + _mm(t_w, w2_ref[...]))) - 0.5
    lora_ref[:, 0:MIX_W] = -jnp.exp(w_log)
    lora_ref[:, MIX_W:2 * MIX_W] = jax.nn.sigmoid(a0_ref[...] + _mm(t_a, a2_ref[...]))
    lora_ref[:, 2 * MIX_W:3 * MIX_W] = _mm(t_g, g2_ref[...])


def _proj(x, norm_g, w_in, layer, mu_wag, w0, w1, w2, a0, a1, a2, g1, g2, *, seq, tm=1024):
    t, d = x.shape
    tm = min(tm, seq)
    n_small = 8 * MIX_W
    blocks_per_tile = tm // SUBLANES
    in_specs = [
        pl.BlockSpec((tm, d), lambda i: (i, 0)),
        pl.BlockSpec((SUBLANES, d), lambda i: (jnp.maximum(i * blocks_per_tile - 1, 0), 0)),
        _const_spec((1, d)),
        _layer_spec(w_in.shape, layer, col_block=(n_small, 0)),
        _const_spec(mu_wag.shape),
        _const_spec((1, MIX_W)),
        _const_spec(w1.shape),
        _const_spec(w2.shape),
        _const_spec((1, MIX_W)),
        _const_spec(a1.shape),
        _const_spec(a2.shape),
        _const_spec(g1.shape),
        _const_spec(g2.shape),
    ]
    return pl.pallas_call(
        functools.partial(_proj_body, seq=seq),
        grid=(t // tm,),
        in_specs=in_specs,
        out_specs=[pl.BlockSpec((tm, n_small), lambda i: (i, 0)),
                   pl.BlockSpec((tm, 3 * MIX_W), lambda i: (i, 0))],
        out_shape=[jax.ShapeDtypeStruct((t, n_small), F32),
                   jax.ShapeDtypeStruct((t, 3 * MIX_W), F32)],
        compiler_params=_params(1),
        name="proj",
    )(x, x, norm_g.reshape(1, d), w_in, mu_wag, w0.reshape(1, MIX_W), w1, w2,
      a0.reshape(1, MIX_W), a1, a2, g1, g2)


def _s5_body(u_ref, lre_ref, lim_ref, ldt_ref, bre_ref, bim_ref, cre_ref, cim_ref, d_ref, wglu_ref,
             o_ref, wre_s, wim_s, abar_s, xre_s, xim_s, hre_s, him_s, st_s, *, unroll):
    @pl.when(pl.program_id(0) == 0)
    def _():
        lr = lre_ref[...]
        li = lim_ref[...]
        dt = jnp.exp(ldt_ref[...])
        mag = jnp.exp(lr * dt)
        ar = mag * jnp.cos(li * dt)
        ai = mag * jnp.sin(li * dt)
        inv = 1.0 / (lr * lr + li * li)
        qr, qi = lr * inv, -li * inv
        coef_re = (ar - 1.0) * qr - ai * qi
        coef_im = (ar - 1.0) * qi + ai * qr
        wre_s[...] = (coef_re * bre_ref[...] - coef_im * bim_ref[...]).astype(BF16)
        wim_s[...] = (coef_re * bim_ref[...] + coef_im * bre_ref[...]).astype(BF16)
        abar_s[0:1, :] = ar
        abar_s[1:2, :] = ai
        st_s[...] = jnp.zeros_like(st_s)

    nb, tl = u_ref.shape[0], u_ref.shape[1]
    rows, n_state = nb * tl, wre_s.shape[1]
    u = u_ref[...].reshape(rows, MIX_W)
    ub = u.astype(BF16)
    blocks_per_row = tl // SUBLANES
    blocks = (nb * blocks_per_row, SUBLANES, n_state)
    xre_s[...] = jnp.dot(ub, wre_s[...], preferred_element_type=F32).reshape(blocks)
    xim_s[...] = jnp.dot(ub, wim_s[...], preferred_element_type=F32).reshape(blocks)
    ar = abar_s[0:1, :]
    ai = abar_s[1:2, :]

    def step(j, carry):
        carry = list(carry)
        for q in range(SUBLANES):
            for b in range(nb):
                blk = b * blocks_per_row + j
                hr, hi = carry[b]
                nr = ar * hr - ai * hi + xre_s[blk, q:q + 1, :]
                ni = ar * hi + ai * hr + xim_s[blk, q:q + 1, :]
                hre_s[blk, q:q + 1, :] = nr
                him_s[blk, q:q + 1, :] = ni
                carry[b] = (nr, ni)
        return tuple(carry)

    init = tuple((st_s[2 * b:2 * b + 1, :], st_s[2 * b + 1:2 * b + 2, :]) for b in range(nb))
    final = lax.fori_loop(0, blocks_per_row, step, init, unroll=unroll)
    for b in range(nb):
        st_s[2 * b:2 * b + 1, :] = final[b][0]
        st_s[2 * b + 1:2 * b + 2, :] = final[b][1]

    y = (_mm(hre_s[...].reshape(rows, n_state), cre_ref[...])
         - _mm(him_s[...].reshape(rows, n_state), cim_ref[...]) + d_ref[...] * u)
    gl = jax.nn.gelu(y, approximate=True)
    o_ref[...] = (gl * jax.nn.sigmoid(_mm(gl, wglu_ref[...]))).reshape(nb, tl, MIX_W)


def _s5(p, lam_re, lam_im, log_dt, b_re, b_im, c_re, c_im, d_skip, w_glu, *, batch, seq, tl=256):
    n_state = S5_GROUPS * S5_STATE
    tl = min(tl, seq)
    eye = jnp.eye(S5_GROUPS, dtype=F32)

    def b_blockdiag(b):
        return jnp.einsum("gph,gk->ghkp", b, eye).reshape(MIX_W, n_state)

    def c_blockdiag(c):
        return jnp.einsum("ghp,gk->gpkh", c, eye).reshape(n_state, MIX_W)

    row = lambda v: v.reshape(1, -1)
    in_specs = [
        pl.BlockSpec((batch, tl, MIX_W), lambda i: (0, i, 0)),
        _const_spec((1, n_state)), _const_spec((1, n_state)), _const_spec((1, n_state)),
        _const_spec((MIX_W, n_state)), _const_spec((MIX_W, n_state)),
        _const_spec((n_state, MIX_W)), _const_spec((n_state, MIX_W)),
        _const_spec((1, MIX_W)), _const_spec((MIX_W, MIX_W)),
    ]
    scan_buf = pltpu.VMEM((batch * tl // SUBLANES, SUBLANES, n_state), F32)
    out = pl.pallas_call(
        functools.partial(_s5_body, unroll=2),
        grid=(seq // tl,),
        in_specs=in_specs,
        out_specs=pl.BlockSpec((batch, tl, MIX_W), lambda i: (0, i, 0)),
        out_shape=jax.ShapeDtypeStruct((batch, seq, MIX_W), F32),
        scratch_shapes=[pltpu.VMEM((MIX_W, n_state), BF16), pltpu.VMEM((MIX_W, n_state), BF16),
                        pltpu.VMEM((2, n_state), F32)]
                       + [scan_buf] * 4 + [pltpu.VMEM((2 * batch, n_state), F32)],
        compiler_params=_params(0, 1),
        name="s5",
    )(p.reshape(batch, seq, p.shape[1]), row(lam_re), row(lam_im),
      row(jnp.broadcast_to(log_dt[:, None], (S5_GROUPS, S5_STATE))),
      b_blockdiag(b_re), b_blockdiag(b_im), c_blockdiag(c_re).astype(BF16),
      c_blockdiag(c_im).astype(BF16), row(d_skip), w_glu.astype(BF16))
    return out.reshape(batch * seq, MIX_W)


def _segment_ones(n, seg):
    r = lax.broadcasted_iota(jnp.int32, (n, n), 0) // seg
    c = lax.broadcasted_iota(jnp.int32, (n, n), 1) // seg
    return r == c


def _rwkv_body(rp_ref, kp_ref, vp_ref, rprev_ref, kprev_ref, vprev_ref, lora_ref, mu_ref, kk_ref,
               ka_ref, rk_ref, lnw_ref, lnb_ref, o_ref, st_s):
    nb, tl = rp_ref.shape[0], rp_ref.shape[1]
    rows = nb * tl
    c = RW_CHUNK
    first = pl.program_id(0) == 0

    @pl.when(first)
    def _():
        st_s[...] = jnp.zeros_like(st_s)

    row_id = lax.broadcasted_iota(jnp.int32, (rows, MIX_W), 0)

    def mixed(cur_ref, prev_ref, mu):
        cur = cur_ref[...].reshape(rows, MIX_W)
        shifted = pltpu.roll(cur, 1, 0)
        for b in range(nb):
            prev_row = jnp.where(first, 0.0, prev_ref[b, SUBLANES - 1:SUBLANES, :])
            shifted = jnp.where(row_id == b * tl, prev_row, shifted)
        return cur + (shifted - cur) * mu

    r = mixed(rp_ref, rprev_ref, mu_ref[0:1, :])
    k = mixed(kp_ref, kprev_ref, mu_ref[1:2, :])
    v = mixed(vp_ref, vprev_ref, mu_ref[2:3, :])
    lora = lora_ref[...].reshape(rows, 3 * MIX_W)
    logw = lora[:, 0:MIX_W]
    a = lora[:, MIX_W:2 * MIX_W]
    head_ones = _segment_ones(MIX_W, RW_HEAD).astype(BF16)
    kraw = k * kk_ref[...]
    kk = kraw / jnp.maximum(jnp.sqrt(_mm(kraw * kraw, head_ones)), 1e-12)
    kmod = k * (1.0 + (a - 1.0) * ka_ref[...])

    bdot = functools.partial(jnp.dot, preferred_element_type=F32)
    n_chunks = rows // c
    chunks_per_row = tl // c

    ti = lax.broadcasted_iota(jnp.int32, (rows, rows), 0)
    tj = lax.broadcasted_iota(jnp.int32, (rows, rows), 1)
    tri = (((ti // c) == (tj // c)) & (tj <= ti)).astype(BF16)
    w_hi = logw.astype(BF16)
    rem = logw - w_hi.astype(F32)
    w_mid = rem.astype(BF16)
    w_lo = (rem - w_mid.astype(F32)).astype(BF16)
    cum = bdot(tri, w_hi) + bdot(tri, w_mid) + bdot(tri, w_lo)
    tot = jnp.concatenate(
        [jnp.broadcast_to(cum[(n + 1) * c - 1:(n + 1) * c, :], (c, MIX_W)) for n in range(n_chunks)],
        axis=0)
    inv = jnp.exp(-cum)
    to_end = jnp.exp(tot - cum)
    r_dec = (r * jnp.exp(cum)).astype(BF16)
    a_dec = (-kk * jnp.exp(cum - logw)).astype(BF16)
    b_inv = (kk * a * inv).astype(BF16)
    k_inv = (kmod * inv).astype(BF16)
    b_end = (kk * a * to_end).astype(BF16)
    k_end = (kmod * to_end).astype(BF16)
    v_b = v.astype(BF16)
    p_end = jnp.exp(tot)

    pair_w = 2 * RW_HEAD
    n_pairs = MIX_W // pair_w
    row_i = lax.broadcasted_iota(jnp.int32, (c, pair_w), 0)
    lane_i = lax.broadcasted_iota(jnp.int32, (c, pair_w), 1)
    left = lane_i < RW_HEAD
    strict = (lane_i % RW_HEAD) < row_i
    incl = (lane_i % RW_HEAD) <= row_i
    eye_fam = ((lane_i % RW_HEAD) == row_i).astype(F32)
    pr = lax.broadcasted_iota(jnp.int32, (pair_w, pair_w), 0)
    pc = lax.broadcasted_iota(jnp.int32, (pair_w, pair_w), 1)
    same_head = (pr // RW_HEAD) == (pc // RW_HEAD)
    eye_pair = (pr == pc).astype(F32)

    def halves(x):
        z = jnp.zeros_like(x)
        return jnp.where(left, x, z), jnp.where(left, z, x)

    def blockdiag(x):
        return jnp.concatenate(halves(x), axis=0)

    probs = [(q, n) for q in range(n_pairs) for n in range(n_chunks)]
    n_probs = len(probs)

    def piece(x, q, n):
        return x[n * c:(n + 1) * c, q * pair_w:(q + 1) * pair_w]

    a_d = [piece(a_dec, q, n) for q, n in probs]
    r_d = [piece(r_dec, q, n) for q, n in probs]
    vv = [piece(v_b, q, n) for q, n in probs]
    vv_bd = [blockdiag(x) for x in vv]
    g = [_mm_nt(jnp.concatenate([a_d[i], r_d[i]], axis=0),
                jnp.concatenate(halves(piece(b_inv, q, n)) + halves(piece(k_inv, q, n)), axis=0))
         for i, (q, n) in enumerate(probs)]
    a_ab = [jnp.where(strict, x[0:c, 0:pair_w], 0.0) for x in g]
    a_ak = [jnp.where(strict, x[0:c, pair_w:2 * pair_w], 0.0).astype(BF16) for x in g]
    lhs_o = [jnp.concatenate([r_d[i], jnp.where(incl, x[c:2 * c, 0:pair_w], 0.0).astype(BF16),
                              jnp.where(incl, x[c:2 * c, pair_w:2 * pair_w], 0.0).astype(BF16)], axis=1)
             for i, x in enumerate(g)]
    tinv = [eye_fam + x for x in a_ab]
    pw = [x.astype(BF16) for x in a_ab]
    pw = [bdot(x, blockdiag(x)).astype(BF16) for x in pw]
    akv = [bdot(a_ak[i], vv_bd[i]).astype(BF16) for i in range(n_probs)]
    n_steps = int(math.log2(c)) - 1
    for s in range(n_steps):
        if s + 1 < n_steps:
            both = [bdot(jnp.concatenate([tinv[i].astype(BF16), pw[i]], axis=0), blockdiag(pw[i]))
                    for i in range(n_probs)]
            tinv = [tinv[i] + both[i][0:c] for i in range(n_probs)]
            pw = [x[c:2 * c].astype(BF16) for x in both]
        else:
            tinv = [tinv[i] + bdot(tinv[i].astype(BF16), blockdiag(pw[i])) for i in range(n_probs)]
    w1u0 = [bdot(tinv[i].astype(BF16), jnp.concatenate([blockdiag(a_d[i]), blockdiag(akv[i])], axis=1))
            for i in range(n_probs)]
    tn = [_mm_tn(piece(b_end, q, n), w1u0[i]) for i, (q, n) in enumerate(probs)]
    kv = [_mm_tn(piece(k_end, q, n), vv[i]) for i, (q, n) in enumerate(probs)]
    trans = [(jnp.where(same_head, tn[i][:, 0:pair_w], 0.0)
              + eye_pair * p_end[n * c:n * c + 1, q * pair_w:(q + 1) * pair_w]).astype(BF16)
             for i, (q, n) in enumerate(probs)]
    add = [jnp.where(same_head, tn[i][:, pair_w:2 * pair_w] + kv[i], 0.0) for i in range(n_probs)]
    w1 = [x[:, 0:pair_w].astype(BF16) for x in w1u0]
    u0 = [x[:, pair_w:2 * pair_w] for x in w1u0]

    st_at = {}
    chains = [(q, b) for q in range(n_pairs) for b in range(nb)]
    st = {qb: st_s[qb[0] * nb + qb[1]] for qb in chains}
    for step in range(chunks_per_row):
        for q, b in chains:
            i = q * n_chunks + b * chunks_per_row + step
            st_b = st[q, b].astype(BF16)
            st_at[i] = st_b
            st[q, b] = bdot(trans[i], st_b) + add[i]
    for q, b in chains:
        st_s[q * nb + b] = st[q, b]
    u = [(bdot(w1[i], st_at[i]) + u0[i]).astype(BF16) for i in range(n_probs)]
    o_p = [bdot(lhs_o[i], jnp.concatenate([st_at[i], blockdiag(u[i]), vv_bd[i]], axis=0))
           for i in range(n_probs)]
    o = jnp.concatenate(
        [jnp.concatenate(o_p[q * n_chunks:(q + 1) * n_chunks], axis=0) for q in range(n_pairs)], axis=1)

    inv_n = 1.0 / RW_HEAD
    mean = _mm(o, head_ones) * inv_n
    dlt = o - mean
    var = _mm(dlt * dlt, head_ones) * inv_n
    o_n = dlt * lax.rsqrt(var + RW_GN_EPS) * lnw_ref[...] + lnb_ref[...]
    bonus = _mm(r * kmod * rk_ref[...], head_ones) * v
    o_ref[...] = ((o_n + bonus) * lora[:, 2 * MIX_W:3 * MIX_W]).reshape(nb, tl, MIX_W)


def _rwkv(p, lora, mu_rkv, k_k, k_a, r_k, ln_w, ln_b, *, batch, seq, tl=256):
    tl = min(tl, seq)
    blocks_per_tile = tl // SUBLANES
    row = lambda v: v.reshape(1, MIX_W)
    p3 = p.reshape(batch, seq, p.shape[1])
    lora3 = lora.reshape(batch, seq, lora.shape[1])

    def cur(col):
        return pl.BlockSpec((batch, tl, MIX_W), lambda i: (0, i, col))

    def prev(col):
        return pl.BlockSpec((batch, SUBLANES, MIX_W),
                            lambda i: (0, jnp.maximum(i * blocks_per_tile - 1, 0), col))

    in_specs = [cur(2), cur(3), cur(4), prev(2), prev(3), prev(4),
                pl.BlockSpec((batch, tl, 3 * MIX_W), lambda i: (0, i, 0)),
                _const_spec((3, MIX_W))] + [_const_spec((1, MIX_W))] * 5
    pair_w = 2 * RW_HEAD
    out = pl.pallas_call(
        _rwkv_body,
        grid=(seq // tl,),
        in_specs=in_specs,
        out_specs=pl.BlockSpec((batch, tl, MIX_W), lambda i: (0, i, 0)),
        out_shape=jax.ShapeDtypeStruct((batch, seq, MIX_W), F32),
        scratch_shapes=[pltpu.VMEM((batch * MIX_W // pair_w, pair_w, pair_w), F32)],
        compiler_params=_params(0, 1),
        name="rwkv",
    )(p3, p3, p3, p3, p3, p3, lora3, mu_rkv, row(k_k), row(k_a), row(r_k), row(ln_w), row(ln_b))
    return out.reshape(batch * seq, MIX_W)


def _merge_body(x_ref, ub_ref, ubprev_ref, z_ref, zprev_ref, bg_ref, cg_ref, cgprev_ref, ya_ref,
                yc_ref, g_ref, wgate01_ref, wgate23_ref, wbr_ref, wout_ref, poolw_ref, pscale_ref,
                convw_ref, o_ref, *, seq, sub):
    i = pl.program_id(0)
    tm, d = x_ref.shape
    at_start = (i * tm) % seq == 0
    lane = lax.broadcasted_iota(jnp.int32, (sub, MIX_W), 1)
    group = lane // (MIX_W // len(POOL_WINDOWS))
    row = lax.broadcasted_iota(jnp.int32, (sub, MIX_W), 0)

    for r0 in range(0, tm, sub):
        rows = slice(r0, r0 + sub)
        x = x_ref[rows, :]
        h = _rmsnorm(x, g_ref[...]).astype(BF16)

        u = ub_ref[rows, :]
        if r0 == 0:
            u_halo = jnp.where(at_start, 0.0, ubprev_ref[...])
            z_halo = jnp.where(at_start, 0.0, cgprev_ref[...] * zprev_ref[...])
        else:
            u_halo = ub_ref[r0 - POOL_HALO:r0, :]
            z_halo = cg_ref[r0 - SUBLANES:r0, :] * z_ref[r0 - SUBLANES:r0, :]
        ext = jnp.concatenate([u_halo, u], axis=0)
        sums = {1: ext}
        span = 1
        while span < POOL_WINDOWS[-1]:
            sums[2 * span] = sums[span] + pltpu.roll(sums[span], span, 0)
            span *= 2
        win_sum = sums[POOL_WINDOWS[-1]][POOL_HALO:, :]
        win = jnp.full((sub, MIX_W), float(POOL_WINDOWS[-1]), F32)
        for gi in range(len(POOL_WINDOWS) - 2, -1, -1):
            win_sum = jnp.where(group == gi, sums[POOL_WINDOWS[gi]][POOL_HALO:, :], win_sum)
            win = jnp.where(group == gi, float(POOL_WINDOWS[gi]), win)
        pos = (i * tm + r0) % seq + row
        count = jnp.minimum((pos + 1).astype(F32), win)
        y_pool = _mm(win_sum / count - u, poolw_ref[...]) * pscale_ref[...]

        zext = jnp.concatenate([z_halo, cg_ref[rows, :] * z_ref[rows, :]], axis=0)
        conv = (convw_ref[2:3, :] * zext
                + convw_ref[1:2, :] * pltpu.roll(zext, 1, 0)
                + convw_ref[0:1, :] * pltpu.roll(zext, 2, 0))
        y_conv = bg_ref[rows, :] * conv[SUBLANES:, :]

        ys = [y.astype(BF16) for y in (ya_ref[rows, :], y_pool, yc_ref[rows, :], y_conv)]
        merged = []
        for j in range(d // MIX_W):
            cols = slice(j * MIX_W, (j + 1) * MIX_W)
            acc = None
            for gi in range(N_BRANCH):
                wgate_ref = (wgate01_ref, wgate23_ref)[gi // 2]
                gcols = slice((gi % 2) * d + j * MIX_W, (gi % 2) * d + (j + 1) * MIX_W)
                gate = jax.nn.sigmoid(jnp.dot(h, wgate_ref[:, gcols], preferred_element_type=F32))
                term = jnp.dot(ys[gi], wbr_ref[gi, :, cols], preferred_element_type=F32) * gate
                acc = term if acc is None else acc + term
            merged.append(acc.astype(BF16))
        o_ref[rows, :] = x + jnp.dot(jnp.concatenate(merged, axis=1), wout_ref[...],
                                     preferred_element_type=F32)


def _merge(x, p, y_s5, y_rwkv, norm_g, w_in, w_branch, w_out, layer, pool_w, pool_scale, conv_w, *,
           seq, tm=1024, sub=512):
    t, d = x.shape
    tm = min(tm, seq)
    gate_block = 2 * d
    first_gate_block = 8 * MIX_W // gate_block
    pool_groups = len(POOL_WINDOWS)
    eye = jnp.eye(pool_groups, dtype=F32)
    pool_bd = jnp.einsum("gcd,gk->gckd", pool_w, eye).reshape(MIX_W, MIX_W)

    def cur(col):
        return pl.BlockSpec((tm, MIX_W), lambda i: (i, col))

    def prev(col, rows):
        per_tile = tm // rows
        return pl.BlockSpec((rows, MIX_W), lambda i: (jnp.maximum(i * per_tile - 1, 0), col))

    in_specs = [
        pl.BlockSpec((tm, d), lambda i: (i, 0)),
        cur(1), prev(1, POOL_HALO), cur(5), prev(5, SUBLANES), cur(6), cur(7), prev(7, SUBLANES),
        pl.BlockSpec((tm, MIX_W), lambda i: (i, 0)),
        pl.BlockSpec((tm, MIX_W), lambda i: (i, 0)),
        _const_spec((1, d)),
        _layer_spec(w_in.shape, layer, col_block=(gate_block, first_gate_block)),
        _layer_spec(w_in.shape, layer, col_block=(gate_block, first_gate_block + 1)),
        _layer_spec(w_branch.shape, layer), _layer_spec(w_out.shape, layer),
        _const_spec((MIX_W, MIX_W)), _const_spec((1, MIX_W)), _const_spec(conv_w.shape),
    ]
    return pl.pallas_call(
        functools.partial(_merge_body, seq=seq, sub=min(sub, tm)),
        grid=(t // tm,),
        in_specs=in_specs,
        out_specs=pl.BlockSpec((tm, d), lambda i: (i, 0)),
        out_shape=jax.ShapeDtypeStruct((t, d), F32),
        compiler_params=_params(1),
        name="merge",
    )(x, p, p, p, p, p, p, p, y_s5, y_rwkv, norm_g.reshape(1, d), w_in, w_in, w_branch, w_out,
      pool_bd, pool_scale.reshape(1, MIX_W), conv_w)


def kernel(x, ffn1_norm, ffn1_w_gate, ffn1_w_up, ffn1_w_down, mix_norm, w_in, s5_lambda_re, s5_lambda_im, s5_log_dt, s5_b_re, s5_b_im, s5_c_re, s5_c_im, s5_d, s5_w_glu, pool_w, pool_scale, rwkv_mu_rkv, rwkv_mu_wag, rwkv_w0, rwkv_w1, rwkv_w2, rwkv_a0, rwkv_a1, rwkv_a2, rwkv_g1, rwkv_g2, rwkv_k_k, rwkv_k_a, rwkv_r_k, rwkv_ln_w, rwkv_ln_b, conv_w, w_branch, w_out, ffn2_norm, ffn2_w_gate, ffn2_w_up, ffn2_w_down, final_norm):
    batch, seq, d = x.shape
    depth = w_in.shape[0]
    bf = lambda w: w.astype(BF16)
    ffn1 = (bf(ffn1_w_gate), bf(ffn1_w_up), bf(ffn1_w_down))
    ffn2 = (bf(ffn2_w_gate), bf(ffn2_w_up), bf(ffn2_w_down))
    w_in_b, w_branch_b, w_out_b = bf(w_in), bf(w_branch), bf(w_out)
    xf = x.reshape(batch * seq, d)
    for l in range(depth):
        xf = _ffn(xf, ffn1_norm[l], *ffn1, l)
        p, lora = _proj(xf, mix_norm[l], w_in_b, l, rwkv_mu_wag[l], rwkv_w0[l],
                        bf(rwkv_w1[l]), rwkv_w2[l], rwkv_a0[l], bf(rwkv_a1[l]), rwkv_a2[l],
                        bf(rwkv_g1[l]), rwkv_g2[l], seq=seq)
        y_s5 = _s5(p, s5_lambda_re[l], s5_lambda_im[l], s5_log_dt[l], s5_b_re[l], s5_b_im[l],
                   s5_c_re[l], s5_c_im[l], s5_d[l], s5_w_glu[l], batch=batch, seq=seq)
        y_rwkv = _rwkv(p, lora, rwkv_mu_rkv[l], rwkv_k_k[l], rwkv_k_a[l], rwkv_r_k[l].reshape(-1),
                       rwkv_ln_w[l], rwkv_ln_b[l], batch=batch, seq=seq)
        xf = _merge(xf, p, y_s5, y_rwkv, mix_norm[l], w_in_b, w_branch_b, w_out_b, l,
                    pool_w[l], pool_scale[l], conv_w[l], seq=seq)
        xf = _ffn(xf, ffn2_norm[l], *ffn2, l, final_norm if l == depth - 1 else None)
    return xf.reshape(batch, seq, d)
```

```python
import functools
import math

import jax
import jax.numpy as jnp
from jax import lax
from jax.experimental import pallas as pl
from jax.experimental.pallas import tpu as pltpu

F32 = jnp.float32
BF16 = jnp.bfloat16

MIX_W = 256
N_BRANCH = 4
S5_GROUPS = 16
S5_STATE = 64
POOL_WINDOWS = (2, 4, 8, 16)
POOL_HALO = 16
RW_HEAD = 64
RW_CHUNK = 64
RW_GN_EPS = 64e-5
NORM_EPS = 1e-6
SUBLANES = 8
VMEM_LIMIT_BYTES = 56 * 1024 * 1024


def _params(n_parallel, n_arbitrary=0):
    return pltpu.CompilerParams(
        dimension_semantics=("parallel",) * n_parallel + ("arbitrary",) * n_arbitrary,
        vmem_limit_bytes=VMEM_LIMIT_BYTES)


def _const_spec(shape):
    zeros = (0,) * len(shape)
    return pl.BlockSpec(shape, lambda *_: zeros, pipeline_mode=pl.Buffered(1))


def _layer_spec(stacked_shape, layer, col_block=None):
    shape = tuple(stacked_shape[1:])
    tail = (0,) * (len(shape) - 1)
    if col_block is None:
        index = (layer,) + tail + (0,)
    else:
        width, col = col_block
        shape = shape[:-1] + (width,)
        index = (layer,) + tail + (col,)
    return pl.BlockSpec((None,) + shape, lambda *_: index, pipeline_mode=pl.Buffered(1))


def _rmsnorm(x, g):
    return x * lax.rsqrt(jnp.mean(x * x, axis=-1, keepdims=True) + NORM_EPS) * g


def _mm(a, b):
    return jnp.dot(a.astype(BF16), b.astype(BF16), preferred_element_type=F32)


def _mm_nt(a, b):
    return lax.dot_general(a.astype(BF16), b.astype(BF16), (((1,), (1,)), ((), ())),
                           preferred_element_type=F32)


def _mm_tn(a, b):
    return lax.dot_general(a.astype(BF16), b.astype(BF16), (((0,), (0,)), ((), ())),
                           preferred_element_type=F32)


def _shift_rows(cur, prev_row):
    rolled = pltpu.roll(cur, 1, 0)
    row = lax.broadcasted_iota(jnp.int32, cur.shape, 0)
    return jnp.where(row == 0, prev_row, rolled)


def _softplus(z):
    return jnp.maximum(z, 0.0) + jnp.log1p(jnp.exp(-jnp.abs(z)))


def _ffn_body(x_ref, g_ref, wg_ref, wu_ref, wd_ref, *rest, n_chunks, final):
    if final:
        fg_ref, o_ref, wg_s, wu_s, wd_s, h_s, acc_ref = rest
    else:
        o_ref, wg_s, wu_s, wd_s, h_s, acc_ref = rest
    s = pl.program_id(0)

    def chunk(h, j):
        gate = jnp.dot(h, wg_s[j], preferred_element_type=F32)
        up = jnp.dot(h, wu_s[j], preferred_element_type=F32)
        act = (gate * jax.nn.sigmoid(gate) * up).astype(BF16)
        return jnp.dot(act, wd_s[j], preferred_element_type=F32)

    def finish():
        y = x_ref[...] + 0.5 * acc_ref[...]
        if final:
            y = _rmsnorm(y, fg_ref[...])
        o_ref[...] = y

    @pl.when(s < n_chunks)
    def _():
        wg_s[s] = wg_ref[...].astype(BF16)
        wu_s[s] = wu_ref[...].astype(BF16)
        wd_s[s] = wd_ref[...].astype(BF16)

        @pl.when(s == 0)
        def _():
            h_s[...] = _rmsnorm(x_ref[...], g_ref[...]).astype(BF16)
            acc_ref[...] = jnp.zeros_like(acc_ref)

        acc_ref[...] += chunk(h_s[...], s)

        @pl.when(s == n_chunks - 1)
        def _():
            finish()

    @pl.when(s >= n_chunks)
    def _():
        h = _rmsnorm(x_ref[...], g_ref[...]).astype(BF16)
        for j in range(n_chunks):
            contrib = chunk(h, j)
            if j == 0:
                acc_ref[...] = contrib
            else:
                acc_ref[...] += contrib
        finish()


def _ffn(x, norm_g, w_gate, w_up, w_down, layer, final_g=None, *, tm=1024, f_chunk=256):
    t, d = x.shape
    d_ff = w_gate.shape[2]
    tm = min(tm, t)
    n_chunks = d_ff // f_chunk
    final = final_g is not None
    tile = lambda s: (jnp.maximum(s - (n_chunks - 1), 0), 0)
    in_specs = [
        pl.BlockSpec((tm, d), tile),
        _const_spec((1, d)),
        pl.BlockSpec((None, d, f_chunk), lambda s: (layer, 0, jnp.minimum(s, n_chunks - 1))),
        pl.BlockSpec((None, d, f_chunk), lambda s: (layer, 0, jnp.minimum(s, n_chunks - 1))),
        pl.BlockSpec((None, f_chunk, d), lambda s: (layer, jnp.minimum(s, n_chunks - 1), 0)),
    ]
    args = [x, norm_g.reshape(1, d), w_gate, w_up, w_down]
    if final:
        in_specs.append(_const_spec((1, d)))
        args.append(final_g.reshape(1, d))
    return pl.pallas_call(
        functools.partial(_ffn_body, n_chunks=n_chunks, final=final),
        grid=(n_chunks - 1 + t // tm,),
        in_specs=in_specs,
        out_specs=pl.BlockSpec((tm, d), tile),
        out_shape=jax.ShapeDtypeStruct((t, d), F32),
        scratch_shapes=[pltpu.VMEM((n_chunks, d, f_chunk), BF16), pltpu.VMEM((n_chunks, d, f_chunk), BF16),
                        pltpu.VMEM((n_chunks, f_chunk, d), BF16), pltpu.VMEM((tm, d), BF16),
                        pltpu.VMEM((tm, d), F32)],
        compiler_params=_params(0, 1),
        name="ffn_final" if final else "ffn",
    )(*args)


def _proj_body(x_ref, xprev_ref, g_ref, win_ref, mu_ref, w0_ref, w1_ref, w2_ref, a0_ref, a1_ref,
               a2_ref, g1_ref, g2_ref, p_ref, lora_ref, *, seq):
    i = pl.program_id(0)
    tm = x_ref.shape[0]
    g = g_ref[...]
    h = _rmsnorm(x_ref[...], g)
    at_start = (i * tm) % seq == 0
    h_prev = _rmsnorm(xprev_ref[SUBLANES - 1:SUBLANES, :], g)
    h_prev = jnp.where(at_start, 0.0, h_prev)
    hx = _shift_rows(h, h_prev) - h
    hb = h.astype(BF16)

    def main_cols(lo, hi):
        for j in range(lo, hi):
            cols = slice(j * MIX_W, (j + 1) * MIX_W)
            p_ref[:, cols] = jnp.dot(hb, win_ref[:, cols], preferred_element_type=F32)

    main_cols(0, 3)
    xw = (h + hx * mu_ref[0:1, :]).astype(BF16)
    t_w = jnp.tanh(jnp.dot(xw, w1_ref[...], preferred_element_type=F32))
    main_cols(3, 5)
    xa = (h + hx * mu_ref[1:2, :]).astype(BF16)
    t_a = jnp.dot(xa, a1_ref[...], preferred_element_type=F32)
    main_cols(5, 7)
    xg = (h + hx * mu_ref[2:3, :]).astype(BF16)
    t_g = jax.nn.sigmoid(jnp.dot(xg, g1_ref[...], preferred_element_type=F32))
    main_cols(7, 8)
    w_log = -_softplus(-(w0_ref[...] + _mm(t_w, w2_ref[...]))) - 0.5
    lora_ref[:, 0:MIX_W] = -jnp.exp(w_log)
    lora_ref[:, MIX_W:2 * MIX_W] = jax.nn.sigmoid(a0_ref[...] + _mm(t_a, a2_ref[...]))
    lora_ref[:, 2 * MIX_W:3 * MIX_W] = _mm(t_g, g2_ref[...])


def _proj(x, norm_g, w_in, layer, mu_wag, w0, w1, w2, a0, a1, a2, g1, g2, *, seq, tm=1024):
    t, d = x.shape
    tm = min(tm, seq)
    n_small = 8 * MIX_W
    blocks_per_tile = tm // SUBLANES
    in_specs = [
        pl.BlockSpec((tm, d), lambda i: (i, 0)),
        pl.BlockSpec((SUBLANES, d), lambda i: (jnp.maximum(i * blocks_per_tile - 1, 0), 0)),
        _const_spec((1, d)),
        _layer_spec(w_in.shape, layer, col_block=(n_small, 0)),
        _const_spec(mu_wag.shape),
        _const_spec((1, MIX_W)),
        _const_spec(w1.shape),
        _const_spec(w2.shape),
        _const_spec((1, MIX_W)),
        _const_spec(a1.shape),
        _const_spec(a2.shape),
        _const_spec(g1.shape),
        _const_spec(g2.shape),
    ]
    return pl.pallas_call(
        functools.partial(_proj_body, seq=seq),
        grid=(t // tm,),
        in_specs=in_specs,
        out_specs=[pl.BlockSpec((tm, n_small), lambda i: (i, 0)),
                   pl.BlockSpec((tm, 3 * MIX_W), lambda i: (i, 0))],
        out_shape=[jax.ShapeDtypeStruct((t, n_small), F32),
                   jax.ShapeDtypeStruct((t, 3 * MIX_W), F32)],
        compiler_params=_params(1),
        name="proj",
    )(x, x, norm_g.reshape(1, d), w_in, mu_wag, w0.reshape(1, MIX_W), w1, w2,
      a0.reshape(1, MIX_W), a1, a2, g1, g2)


def _s5_body(u_ref, lre_ref, lim_ref, ldt_ref, bre_ref, bim_ref, cre_ref, cim_ref, d_ref, wglu_ref,
             o_ref, wre_s, wim_s, abar_s, xre_s, xim_s, hre_s, him_s, st_s, *, unroll):
    @pl.when(pl.program_id(0) == 0)
    def _():
        lr = lre_ref[...]
        li = lim_ref[...]
        dt = jnp.exp(ldt_ref[...])
        mag = jnp.exp(lr * dt)
        ar = mag * jnp.cos(li * dt)
        ai = mag * jnp.sin(li * dt)
        inv = 1.0 / (lr * lr + li * li)
        qr, qi = lr * inv, -li * inv
        coef_re = (ar - 1.0) * qr - ai * qi
        coef_im = (ar - 1.0) * qi + ai * qr
        wre_s[...] = (coef_re * bre_ref[...] - coef_im * bim_ref[...]).astype(BF16)
        wim_s[...] = (coef_re * bim_ref[...] + coef_im * bre_ref[...]).astype(BF16)
        abar_s[0:1, :] = ar
        abar_s[1:2, :] = ai
        st_s[...] = jnp.zeros_like(st_s)

    nb, tl = u_ref.shape[0], u_ref.shape[1]
    rows, n_state = nb * tl, wre_s.shape[1]
    u = u_ref[...].reshape(rows, MIX_W)
    ub = u.astype(BF16)
    blocks_per_row = tl // SUBLANES
    blocks = (nb * blocks_per_row, SUBLANES, n_state)
    xre_s[...] = jnp.dot(ub, wre_s[...], preferred_element_type=F32).reshape(blocks)
    xim_s[...] = jnp.dot(ub, wim_s[...], preferred_element_type=F32).reshape(blocks)
    ar = abar_s[0:1, :]
    ai = abar_s[1:2, :]

    def step(j, carry):
        carry = list(carry)
        for q in range(SUBLANES):
            for b in range(nb):
                blk = b * blocks_per_row + j
                hr, hi = carry[b]
                nr = ar * hr - ai * hi + xre_s[blk, q:q + 1, :]
                ni = ar * hi + ai * hr + xim_s[blk, q:q + 1, :]
                hre_s[blk, q:q + 1, :] = nr
                him_s[blk, q:q + 1, :] = ni
                carry[b] = (nr, ni)
        return tuple(carry)

    init = tuple((st_s[2 * b:2 * b + 1, :], st_s[2 * b + 1:2 * b + 2, :]) for b in range(nb))
    final = lax.fori_loop(0, blocks_per_row, step, init, unroll=unroll)
    for b in range(nb):
        st_s[2 * b:2 * b + 1, :] = final[b][0]
        st_s[2 * b + 1:2 * b + 2, :] = final[b][1]

    y = (_mm(hre_s[...].reshape(rows, n_state), cre_ref[...])
         - _mm(him_s[...].reshape(rows, n_state), cim_ref[...]) + d_ref[...] * u)
    gl = jax.nn.gelu(y, approximate=True)
    o_ref[...] = (gl * jax.nn.sigmoid(_mm(gl, wglu_ref[...]))).reshape(nb, tl, MIX_W)


def _s5(p, lam_re, lam_im, log_dt, b_re, b_im, c_re, c_im, d_skip, w_glu, *, batch, seq, tl=256):
    n_state = S5_GROUPS * S5_STATE
    tl = min(tl, seq)
    eye = jnp.eye(S5_GROUPS, dtype=F32)

    def b_blockdiag(b):
        return jnp.einsum("gph,gk->ghkp", b, eye).reshape(MIX_W, n_state)

    def c_blockdiag(c):
        return jnp.einsum("ghp,gk->gpkh", c, eye).reshape(n_state, MIX_W)

    row = lambda v: v.reshape(1, -1)
    in_specs = [
        pl.BlockSpec((batch, tl, MIX_W), lambda i: (0, i, 0)),
        _const_spec((1, n_state)), _const_spec((1, n_state)), _const_spec((1, n_state)),
        _const_spec((MIX_W, n_state)), _const_spec((MIX_W, n_state)),
        _const_spec((n_state, MIX_W)), _const_spec((n_state, MIX_W)),
        _const_spec((1, MIX_W)), _const_spec((MIX_W, MIX_W)),
    ]
    scan_buf = pltpu.VMEM((batch * tl // SUBLANES, SUBLANES, n_state), F32)
    out = pl.pallas_call(
        functools.partial(_s5_body, unroll=2),
        grid=(seq // tl,),
        in_specs=in_specs,
        out_specs=pl.BlockSpec((batch, tl, MIX_W), lambda i: (0, i, 0)),
        out_shape=jax.ShapeDtypeStruct((batch, seq, MIX_W), F32),
        scratch_shapes=[pltpu.VMEM((MIX_W, n_state), BF16), pltpu.VMEM((MIX_W, n_state), BF16),
                        pltpu.VMEM((2, n_state), F32)]
                       + [scan_buf] * 4 + [pltpu.VMEM((2 * batch, n_state), F32)],
        compiler_params=_params(0, 1),
        name="s5",
    )(p.reshape(batch, seq, p.shape[1]), row(lam_re), row(lam_im),
      row(jnp.broadcast_to(log_dt[:, None], (S5_GROUPS, S5_STATE))),
      b_blockdiag(b_re), b_blockdiag(b_im), c_blockdiag(c_re).astype(BF16),
      c_blockdiag(c_im).astype(BF16), row(d_skip), w_glu.astype(BF16))
    return out.reshape(batch * seq, MIX_W)


def _segment_ones(n, seg):
    r = lax.broadcasted_iota(jnp.int32, (n, n), 0) // seg
    c = lax.broadcasted_iota(jnp.int32, (n, n), 1) // seg
    return r == c


def _rwkv_body(rp_ref, kp_ref, vp_ref, rprev_ref, kprev_ref, vprev_ref, lora_ref, mu_ref, kk_ref,
               ka_ref, rk_ref, lnw_ref, lnb_ref, o_ref, st_s):
    nb, tl = rp_ref.shape[0], rp_ref.shape[1]
    rows = nb * tl
    c = RW_CHUNK
    first = pl.program_id(0) == 0

    @pl.when(first)
    def _():
        st_s[...] = jnp.zeros_like(st_s)

    row_id = lax.broadcasted_iota(jnp.int32, (rows, MIX_W), 0)

    def mixed(cur_ref, prev_ref, mu):
        cur = cur_ref[...].reshape(rows, MIX_W)
        shifted = pltpu.roll(cur, 1, 0)
        for b in range(nb):
            prev_row = jnp.where(first, 0.0, prev_ref[b, SUBLANES - 1:SUBLANES, :])
            shifted = jnp.where(row_id == b * tl, prev_row, shifted)
        return cur + (shifted - cur) * mu

    r = mixed(rp_ref, rprev_ref, mu_ref[0:1, :])
    k = mixed(kp_ref, kprev_ref, mu_ref[1:2, :])
    v = mixed(vp_ref, vprev_ref, mu_ref[2:3, :])
    lora = lora_ref[...].reshape(rows, 3 * MIX_W)
    logw = lora[:, 0:MIX_W]
    a = lora[:, MIX_W:2 * MIX_W]
    head_ones = _segment_ones(MIX_W, RW_HEAD).astype(BF16)
    kraw = k * kk_ref[...]
    kk = kraw / jnp.maximum(jnp.sqrt(_mm(kraw * kraw, head_ones)), 1e-12)
    kmod = k * (1.0 + (a - 1.0) * ka_ref[...])

    bdot = functools.partial(jnp.dot, preferred_element_type=F32)
    n_chunks = rows // c
    chunks_per_row = tl // c

    ti = lax.broadcasted_iota(jnp.int32, (rows, rows), 0)
    tj = lax.broadcasted_iota(jnp.int32, (rows, rows), 1)
    tri = (((ti // c) == (tj // c)) & (tj <= ti)).astype(BF16)
    w_hi = logw.astype(BF16)
    rem = logw - w_hi.astype(F32)
    w_mid = rem.astype(BF16)
    w_lo = (rem - w_mid.astype(F32)).astype(BF16)
    cum = bdot(tri, w_hi) + bdot(tri, w_mid) + bdot(tri, w_lo)
    tot = jnp.concatenate(
        [jnp.broadcast_to(cum[(n + 1) * c - 1:(n + 1) * c, :], (c, MIX_W)) for n in range(n_chunks)],
        axis=0)
    inv = jnp.exp(-cum)
    to_end = jnp.exp(tot - cum)
    r_dec = (r * jnp.exp(cum)).astype(BF16)
    a_dec = (-kk * jnp.exp(cum - logw)).astype(BF16)
    b_inv = (kk * a * inv).astype(BF16)
    k_inv = (kmod * inv).astype(BF16)
    b_end = (kk * a * to_end).astype(BF16)
    k_end = (kmod * to_end).astype(BF16)
    v_b = v.astype(BF16)
    p_end = jnp.exp(tot)

    pair_w = 2 * RW_HEAD
    n_pairs = MIX_W // pair_w
    row_i = lax.broadcasted_iota(jnp.int32, (c, pair_w), 0)
    lane_i = lax.broadcasted_iota(jnp.int32, (c, pair_w), 1)
    left = lane_i < RW_HEAD
    strict = (lane_i % RW_HEAD) < row_i
    incl = (lane_i % RW_HEAD) <= row_i
    eye_fam = ((lane_i % RW_HEAD) == row_i).astype(F32)
    pr = lax.broadcasted_iota(jnp.int32, (pair_w, pair_w), 0)
    pc = lax.broadcasted_iota(jnp.int32, (pair_w, pair_w), 1)
    same_head = (pr // RW_HEAD) == (pc // RW_HEAD)
    eye_pair = (pr == pc).astype(F32)

    def halves(x):
        z = jnp.zeros_like(x)
        return jnp.where(left, x, z), jnp.where(left, z, x)

    def blockdiag(x):
        return jnp.concatenate(halves(x), axis=0)

    probs = [(q, n) for q in range(n_pairs) for n in range(n_chunks)]
    n_probs = len(probs)

    def piece(x, q, n):
        return x[n * c:(n + 1) * c, q * pair_w:(q + 1) * pair_w]

    a_d = [piece(a_dec, q, n) for q, n in probs]
    r_d = [piece(r_dec, q, n) for q, n in probs]
    vv = [piece(v_b, q, n) for q, n in probs]
    vv_bd = [blockdiag(x) for x in vv]
    g = [_mm_nt(jnp.concatenate([a_d[i], r_d[i]], axis=0),
                jnp.concatenate(halves(piece(b_inv, q, n)) + halves(piece(k_inv, q, n)), axis=0))
         for i, (q, n) in enumerate(probs)]
    a_ab = [jnp.where(strict, x[0:c, 0:pair_w], 0.0) for x in g]
    a_ak = [jnp.where(strict, x[0:c, pair_w:2 * pair_w], 0.0).astype(BF16) for x in g]
    lhs_o = [jnp.concatenate([r_d[i], jnp.where(incl, x[c:2 * c, 0:pair_w], 0.0).astype(BF16),
                              jnp.where(incl, x[c:2 * c, pair_w:2 * pair_w], 0.0).astype(BF16)], axis=1)
             for i, x in enumerate(g)]
    tinv = [eye_fam + x for x in a_ab]
    pw = [x.astype(BF16) for x in a_ab]
    pw = [bdot(x, blockdiag(x)).astype(BF16) for x in pw]
    akv = [bdot(a_ak[i], vv_bd[i]).astype(BF16) for i in range(n_probs)]
    n_steps = int(math.log2(c)) - 1
    for s in range(n_steps):
        if s + 1 < n_steps:
            both = [bdot(jnp.concatenate([tinv[i].astype(BF16), pw[i]], axis=0), blockdiag(pw[i]))
                    for i in range(n_probs)]
            tinv = [tinv[i] + both[i][0:c] for i in range(n_probs)]
            pw = [x[c:2 * c].astype(BF16) for x in both]
        else:
            tinv = [tinv[i] + bdot(tinv[i].astype(BF16), blockdiag(pw[i])) for i in range(n_probs)]
    w1u0 = [bdot(tinv[i].astype(BF16), jnp.concatenate([blockdiag(a_d[i]), blockdiag(akv[i])], axis=1))
            for i in range(n_probs)]
    tn = [_mm_tn(piece(b_end, q, n), w1u0[i]) for i, (q, n) in enumerate(probs)]
    kv = [_mm_tn(piece(k_end, q, n), vv[i]) for i, (q, n) in enumerate(probs)]
    trans = [(jnp.where(same_head, tn[i][:, 0:pair_w], 0.0)
              + eye_pair * p_end[n * c:n * c + 1, q * pair_w:(q + 1) * pair_w]).astype(BF16)
             for i, (q, n) in enumerate(probs)]
    add = [jnp.where(same_head, tn[i][:, pair_w:2 * pair_w] + kv[i], 0.0) for i in range(n_probs)]
    w1 = [x[:, 0:pair_w].astype(BF16) for x in w1u0]
    u0 = [x[:, pair_w:2 * pair_w] for x in w1u0]

    st_at = {}
    chains = [(q, b) for q in range(n_pairs) for b in range(nb)]
    st = {qb: st_s[qb[0] * nb + qb[1]] for qb in chains}
    for step in range(chunks_per_row):
        for q, b in chains:
            i = q * n_chunks + b * chunks_per_row + step
            st_b = st[q, b].astype(BF16)
            st_at[i] = st_b
            st[q, b] = bdot(trans[i], st_b) + add[i]
    for q, b in chains:
        st_s[q * nb + b] = st[q, b]
    u = [(bdot(w1[i], st_at[i]) + u0[i]).astype(BF16) for i in range(n_probs)]
    o_p = [bdot(lhs_o[i], jnp.concatenate([st_at[i], blockdiag(u[i]), vv_bd[i]], axis=0))
           for i in range(n_probs)]
    o = jnp.concatenate(
        [jnp.concatenate(o_p[q * n_chunks:(q + 1) * n_chunks], axis=0) for q in range(n_pairs)], axis=1)

    inv_n = 1.0 / RW_HEAD
    mean = _mm(o, head_ones) * inv_n
    dlt = o - mean
    var = _mm(dlt * dlt, head_ones) * inv_n
    o_n = dlt * lax.rsqrt(var + RW_GN_EPS) * lnw_ref[...] + lnb_ref[...]
    bonus = _mm(r * kmod * rk_ref[...], head_ones) * v
    o_ref[...] = ((o_n + bonus) * lora[:, 2 * MIX_W:3 * MIX_W]).reshape(nb, tl, MIX_W)


def _rwkv(p, lora, mu_rkv, k_k, k_a, r_k, ln_w, ln_b, *, batch, seq, tl=256):
    tl = min(tl, seq)
    blocks_per_tile = tl // SUBLANES
    row = lambda v: v.reshape(1, MIX_W)
    p3 = p.reshape(batch, seq, p.shape[1])
    lora3 = lora.reshape(batch, seq, lora.shape[1])

    def cur(col):
        return pl.BlockSpec((batch, tl, MIX_W), lambda i: (0, i, col))

    def prev(col):
        return pl.BlockSpec((batch, SUBLANES, MIX_W),
                            lambda i: (0, jnp.maximum(i * blocks_per_tile - 1, 0), col))

    in_specs = [cur(2), cur(3), cur(4), prev(2), prev(3), prev(4),
                pl.BlockSpec((batch, tl, 3 * MIX_W), lambda i: (0, i, 0)),
                _const_spec((3, MIX_W))] + [_const_spec((1, MIX_W))] * 5
    pair_w = 2 * RW_HEAD
    out = pl.pallas_call(
        _rwkv_body,
        grid=(seq // tl,),
        in_specs=in_specs,
        out_specs=pl.BlockSpec((batch, tl, MIX_W), lambda i: (0, i, 0)),
        out_shape=jax.ShapeDtypeStruct((batch, seq, MIX_W), F32),
        scratch_shapes=[pltpu.VMEM((batch * MIX_W // pair_w, pair_w, pair_w), F32)],
        compiler_params=_params(0, 1),
        name="rwkv",
    )(p3, p3, p3, p3, p3, p3, lora3, mu_rkv, row(k_k), row(k_a), row(r_k), row(ln_w), row(ln_b))
    return out.reshape(batch * seq, MIX_W)


def _merge_body(x_ref, ub_ref, ubprev_ref, z_ref, zprev_ref, bg_ref, cg_ref, cgprev_ref, ya_ref,
                yc_ref, g_ref, wgate01_ref, wgate23_ref, wbr_ref, wout_ref, poolw_ref, pscale_ref,
                convw_ref, o_ref, *, seq, sub):
    i = pl.program_id(0)
    tm, d = x_ref.shape
    at_start = (i * tm) % seq == 0
    lane = lax.broadcasted_iota(jnp.int32, (sub, MIX_W), 1)
    group = lane // (MIX_W // len(POOL_WINDOWS))
    row = lax.broadcasted_iota(jnp.int32, (sub, MIX_W), 0)

    for r0 in range(0, tm, sub):
        rows = slice(r0, r0 + sub)
        x = x_ref[rows, :]
        h = _rmsnorm(x, g_ref[...]).astype(BF16)

        u = ub_ref[rows, :]
        if r0 == 0:
            u_halo = jnp.where(at_start, 0.0, ubprev_ref[...])
            z_halo = jnp.where(at_start, 0.0, cgprev_ref[...] * zprev_ref[...])
        else:
            u_halo = ub_ref[r0 - POOL_HALO:r0, :]
            z_halo = cg_ref[r0 - SUBLANES:r0, :] * z_ref[r0 - SUBLANES:r0, :]
        ext = jnp.concatenate([u_halo, u], axis=0)
        sums = {1: ext}
        span = 1
        while span < POOL_WINDOWS[-1]:
            sums[2 * span] = sums[span] + pltpu.roll(sums[span], span, 0)
            span *= 2
        win_sum = sums[POOL_WINDOWS[-1]][POOL_HALO:, :]
        win = jnp.full((sub, MIX_W), float(POOL_WINDOWS[-1]), F32)
        for gi in range(len(POOL_WINDOWS) - 2, -1, -1):
            win_sum = jnp.where(group == gi, sums[POOL_WINDOWS[gi]][POOL_HALO:, :], win_sum)
            win = jnp.where(group == gi, float(POOL_WINDOWS[gi]), win)
        pos = (i * tm + r0) % seq + row
        count = jnp.minimum((pos + 1).astype(F32), win)
        y_pool = _mm(win_sum / count - u, poolw_ref[...]) * pscale_ref[...]

        zext = jnp.concatenate([z_halo, cg_ref[rows, :] * z_ref[rows, :]], axis=0)
        conv = (convw_ref[2:3, :] * zext
                + convw_ref[1:2, :] * pltpu.roll(zext, 1, 0)
                + convw_ref[0:1, :] * pltpu.roll(zext, 2, 0))
        y_conv = bg_ref[rows, :] * conv[SUBLANES:, :]

        ys = [y.astype(BF16) for y in (ya_ref[rows, :], y_pool, yc_ref[rows, :], y_conv)]
        merged = []
        for j in range(d // MIX_W):
            cols = slice(j * MIX_W, (j + 1) * MIX_W)
            acc = None
            for gi in range(N_BRANCH):
                wgate_ref = (wgate01_ref, wgate23_ref)[gi // 2]
                gcols = slice((gi % 2) * d + j * MIX_W, (gi % 2) * d + (j + 1) * MIX_W)
                gate = jax.nn.sigmoid(jnp.dot(h, wgate_ref[:, gcols], preferred_element_type=F32))
                term = jnp.dot(ys[gi], wbr_ref[gi, :, cols], preferred_element_type=F32) * gate
                acc = term if acc is None else acc + term
            merged.append(acc.astype(BF16))
        o_ref[rows, :] = x + jnp.dot(jnp.concatenate(merged, axis=1), wout_ref[...],
                                     preferred_element_type=F32)


def _merge(x, p, y_s5, y_rwkv, norm_g, w_in, w_branch, w_out, layer, pool_w, pool_scale, conv_w, *,
           seq, tm=1024, sub=512):
    t, d = x.shape
    tm = min(tm, seq)
    gate_block = 2 * d
    first_gate_block = 8 * MIX_W // gate_block
    pool_groups = len(POOL_WINDOWS)
    eye = jnp.eye(pool_groups, dtype=F32)
    pool_bd = jnp.einsum("gcd,gk->gckd", pool_w, eye).reshape(MIX_W, MIX_W)

    def cur(col):
        return pl.BlockSpec((tm, MIX_W), lambda i: (i, col))

    def prev(col, rows):
        per_tile = tm // rows
        return pl.BlockSpec((rows, MIX_W), lambda i: (jnp.maximum(i * per_tile - 1, 0), col))

    in_specs = [
        pl.BlockSpec((tm, d), lambda i: (i, 0)),
        cur(1), prev(1, POOL_HALO), cur(5), prev(5, SUBLANES), cur(6), cur(7), prev(7, SUBLANES),
        pl.BlockSpec((tm, MIX_W), lambda i: (i, 0)),
        pl.BlockSpec((tm, MIX_W), lambda i: (i, 0)),
        _const_spec((1, d)),
        _layer_spec(w_in.shape, layer, col_block=(gate_block, first_gate_block)),
        _layer_spec(w_in.shape, layer, col_block=(gate_block, first_gate_block + 1)),
        _layer_spec(w_branch.shape, layer), _layer_spec(w_out.shape, layer),
        _const_spec((MIX_W, MIX_W)), _const_spec((1, MIX_W)), _const_spec(conv_w.shape),
    ]
    return pl.pallas_call(
        functools.partial(_merge_body, seq=seq, sub=min(sub, tm)),
        grid=(t // tm,),
        in_specs=in_specs,
        out_specs=pl.BlockSpec((tm, d), lambda i: (i, 0)),
        out_shape=jax.ShapeDtypeStruct((t, d), F32),
        compiler_params=_params(1),
        name="merge",
    )(x, p, p, p, p, p, p, p, y_s5, y_rwkv, norm_g.reshape(1, d), w_in, w_in, w_branch, w_out,
      pool_bd, pool_scale.reshape(1, MIX_W), conv_w)


def kernel(x, ffn1_norm, ffn1_w_gate, ffn1_w_up, ffn1_w_down, mix_norm, w_in, s5_lambda_re, s5_lambda_im, s5_log_dt, s5_b_re, s5_b_im, s5_c_re, s5_c_im, s5_d, s5_w_glu, pool_w, pool_scale, rwkv_mu_rkv, rwkv_mu_wag, rwkv_w0, rwkv_w1, rwkv_w2, rwkv_a0, rwkv_a1, rwkv_a2, rwkv_g1, rwkv_g2, rwkv_k_k, rwkv_k_a, rwkv_r_k, rwkv_ln_w, rwkv_ln_b, conv_w, w_branch, w_out, ffn2_norm, ffn2_w_gate, ffn2_w_up, ffn2_w_down, final_norm):
    batch, seq, d = x.shape
    depth = w_in.shape[0]
    bf = lambda w: w.astype(BF16)
    ffn1 = (ffn1_w_gate, ffn1_w_up, ffn1_w_down)
    ffn2 = (ffn2_w_gate, ffn2_w_up, ffn2_w_down)
    w_in_b, w_branch_b, w_out_b = bf(w_in), bf(w_branch), bf(w_out)
    xf = x.reshape(batch * seq, d)
    for l in range(depth):
        xf = _ffn(xf, ffn1_norm[l], *ffn1, l)
        p, lora = _proj(xf, mix_norm[l], w_in_b, l, rwkv_mu_wag[l], rwkv_w0[l],
                        bf(rwkv_w1[l]), rwkv_w2[l], rwkv_a0[l], bf(rwkv_a1[l]), rwkv_a2[l],
                        bf(rwkv_g1[l]), rwkv_g2[l], seq=seq)
        y_s5 = _s5(p, s5_lambda_re[l], s5_lambda_im[l], s5_log_dt[l], s5_b_re[l], s5_b_im[l],
                   s5_c_re[l], s5_c_im[l], s5_d[l], s5_w_glu[l], batch=batch, seq=seq)
        y_rwkv = _rwkv(p, lora, rwkv_mu_rkv[l], rwkv_k_k[l], rwkv_k_a[l], rwkv_r_k[l].reshape(-1),
                       rwkv_ln_w[l], rwkv_ln_b[l], batch=batch, seq=seq)
        xf = _merge(xf, p, y_s5, y_rwkv, mix_norm[l], w_in_b, w_branch_b, w_out_b, l,
                    pool_w[l], pool_scale[l], conv_w[l], seq=seq)
        xf = _ffn(xf, ffn2_norm[l], *ffn2, l, final_norm if l == depth - 1 else None)
    return xf.reshape(batch, seq, d)
```

```python
import functools
import math

import jax
import jax.numpy as jnp
from jax import lax
from jax.experimental import pallas as pl
from jax.experimental.pallas import tpu as pltpu

F32 = jnp.float32
BF16 = jnp.bfloat16

MIX_W = 256
N_BRANCH = 4
S5_GROUPS = 16
S5_STATE = 64
POOL_WINDOWS = (2, 4, 8, 16)
POOL_HALO = 16
RW_HEAD = 64
RW_CHUNK = 64
RW_GN_EPS = 64e-5
NORM_EPS = 1e-6
SUBLANES = 8
VMEM_LIMIT_BYTES = 56 * 1024 * 1024


def _params(n_parallel, n_arbitrary=0):
    return pltpu.CompilerParams(
        dimension_semantics=("parallel",) * n_parallel + ("arbitrary",) * n_arbitrary,
        vmem_limit_bytes=VMEM_LIMIT_BYTES)


def _const_spec(shape):
    zeros = (0,) * len(shape)
    return pl.BlockSpec(shape, lambda *_: zeros, pipeline_mode=pl.Buffered(1))


def _layer_spec(stacked_shape, layer, col_block=None):
    shape = tuple(stacked_shape[1:])
    tail = (0,) * (len(shape) - 1)
    if col_block is None:
        index = (layer,) + tail + (0,)
    else:
        width, col = col_block
        shape = shape[:-1] + (width,)
        index = (layer,) + tail + (col,)
    return pl.BlockSpec((None,) + shape, lambda *_: index, pipeline_mode=pl.Buffered(1))


def _rmsnorm(x, g):
    return x * lax.rsqrt(jnp.mean(x * x, axis=-1, keepdims=True) + NORM_EPS) * g


def _mm(a, b):
    return jnp.dot(a.astype(BF16), b.astype(BF16), preferred_element_type=F32)


def _mm_nt(a, b):
    return lax.dot_general(a.astype(BF16), b.astype(BF16), (((1,), (1,)), ((), ())),
                           preferred_element_type=F32)


def _mm_tn(a, b):
    return lax.dot_general(a.astype(BF16), b.astype(BF16), (((0,), (0,)), ((), ())),
                           preferred_element_type=F32)


def _shift_rows(cur, prev_row):
    rolled = pltpu.roll(cur, 1, 0)
    row = lax.broadcasted_iota(jnp.int32, cur.shape, 0)
    return jnp.where(row == 0, prev_row, rolled)


def _softplus(z):
    return jnp.maximum(z, 0.0) + jnp.log1p(jnp.exp(-jnp.abs(z)))


def _ffn_body(x_ref, g_ref, wg_ref, wu_ref, wd_ref, *rest, n_chunks, final):
    if final:
        fg_ref, o_ref, wg_s, wu_s, wd_s, h_s, acc_ref = rest
    else:
        o_ref, wg_s, wu_s, wd_s, h_s, acc_ref = rest
    s = pl.program_id(0)

    def chunk(h, j):
        gate = jnp.dot(h, wg_s[j], preferred_element_type=F32)
        up = jnp.dot(h, wu_s[j], preferred_element_type=F32)
        act = (gate * jax.nn.sigmoid(gate) * up).astype(BF16)
        return jnp.dot(act, wd_s[j], preferred_element_type=F32)

    def finish():
        y = x_ref[...] + 0.5 * acc_ref[...]
        if final:
            y = _rmsnorm(y, fg_ref[...])
        o_ref[...] = y

    @pl.when(s < n_chunks)
    def _():
        wg_s[s] = wg_ref[...].astype(BF16)
        wu_s[s] = wu_ref[...].astype(BF16)
        wd_s[s] = wd_ref[...].astype(BF16)

        @pl.when(s == 0)
        def _():
            h_s[...] = _rmsnorm(x_ref[...], g_ref[...]).astype(BF16)
            acc_ref[...] = jnp.zeros_like(acc_ref)

        acc_ref[...] += chunk(h_s[...], s)

        @pl.when(s == n_chunks - 1)
        def _():
            finish()

    @pl.when(s >= n_chunks)
    def _():
        h = _rmsnorm(x_ref[...], g_ref[...]).astype(BF16)
        for j in range(n_chunks):
            contrib = chunk(h, j)
            if j == 0:
                acc_ref[...] = contrib
            else:
                acc_ref[...] += contrib
        finish()


def _ffn(x, norm_g, w_gate, w_up, w_down, layer, final_g=None, *, tm=1024, f_chunk=256):
    t, d = x.shape
    d_ff = w_gate.shape[2]
    tm = min(tm, t)
    n_chunks = d_ff // f_chunk
    final = final_g is not None
    tile = lambda s: (jnp.maximum(s - (n_chunks - 1), 0), 0)
    in_specs = [
        pl.BlockSpec((tm, d), tile),
        _layer_spec(norm_g.shape, layer),
        pl.BlockSpec((None, d, f_chunk), lambda s: (layer, 0, jnp.minimum(s, n_chunks - 1))),
        pl.BlockSpec((None, d, f_chunk), lambda s: (layer, 0, jnp.minimum(s, n_chunks - 1))),
        pl.BlockSpec((None, f_chunk, d), lambda s: (layer, jnp.minimum(s, n_chunks - 1), 0)),
    ]
    args = [x, norm_g, w_gate, w_up, w_down]
    if final:
        in_specs.append(_const_spec((1, d)))
        args.append(final_g.reshape(1, d))
    return pl.pallas_call(
        functools.partial(_ffn_body, n_chunks=n_chunks, final=final),
        grid=(n_chunks - 1 + t // tm,),
        in_specs=in_specs,
        out_specs=pl.BlockSpec((tm, d), tile),
        out_shape=jax.ShapeDtypeStruct((t, d), F32),
        scratch_shapes=[pltpu.VMEM((n_chunks, d, f_chunk), BF16), pltpu.VMEM((n_chunks, d, f_chunk), BF16),
                        pltpu.VMEM((n_chunks, f_chunk, d), BF16), pltpu.VMEM((tm, d), BF16),
                        pltpu.VMEM((tm, d), F32)],
        compiler_params=_params(0, 1),
        name="ffn_final" if final else "ffn",
    )(*args)


def _proj_body(x_ref, xprev_ref, g_ref, win_ref, mu_ref, w0_ref, w1_ref, w2_ref, a0_ref, a1_ref,
               a2_ref, g1_ref, g2_ref, p_ref, lora_ref, *, seq):
    i = pl.program_id(0)
    tm = x_ref.shape[0]
    g = g_ref[...]
    h = _rmsnorm(x_ref[...], g)
    at_start = (i * tm) % seq == 0
    h_prev = _rmsnorm(xprev_ref[SUBLANES - 1:SUBLANES, :], g)
    h_prev = jnp.where(at_start, 0.0, h_prev)
    hx = _shift_rows(h, h_prev) - h
    hb = h.astype(BF16)

    def main_cols(lo, hi):
        for j in range(lo, hi):
            cols = slice(j * MIX_W, (j + 1) * MIX_W)
            p_ref[:, cols] = jnp.dot(hb, win_ref[:, cols], preferred_element_type=F32)

    main_cols(0, 3)
    xw = (h + hx * mu_ref[0:1, :]).astype(BF16)
    t_w = jnp.tanh(jnp.dot(xw, w1_ref[...], preferred_element_type=F32))
    main_cols(3, 5)
    xa = (h + hx * mu_ref[1:2, :]).astype(BF16)
    t_a = jnp.dot(xa, a1_ref[...], preferred_element_type=F32)
    main_cols(5, 7)
    xg = (h + hx * mu_ref[2:3, :]).astype(BF16)
    t_g = jax.nn.sigmoid(jnp.dot(xg, g1_ref[...], preferred_element_type=F32))
    main_cols(7, 8)
    w_log = -_softplus(-(w0_ref[...] + _mm(t_w, w2_ref[...]))) - 0.5
    lora_ref[:, 0:MIX_W] = -jnp.exp(w_log)
    lora_ref[:, MIX_W:2 * MIX_W] = jax.nn.sigmoid(a0_ref[...] + _mm(t_a, a2_ref[...]))
    lora_ref[:, 2 * MIX_W:3 * MIX_W] = _mm(t_g, g2_ref[...])


def _proj(x, norm_g, w_in, layer, mu_wag, w0, w1, w2, a0, a1, a2, g1, g2, *, seq, tm=1024):
    t, d = x.shape
    tm = min(tm, seq)
    n_small = 8 * MIX_W
    blocks_per_tile = tm // SUBLANES
    in_specs = [
        pl.BlockSpec((tm, d), lambda i: (i, 0)),
        pl.BlockSpec((SUBLANES, d), lambda i: (jnp.maximum(i * blocks_per_tile - 1, 0), 0)),
        _layer_spec(norm_g.shape, layer),
        _layer_spec(w_in.shape, layer, col_block=(n_small, 0)),
    ] + [_layer_spec(a.shape, layer) for a in (mu_wag, w0, w1, w2, a0, a1, a2, g1, g2)]
    return pl.pallas_call(
        functools.partial(_proj_body, seq=seq),
        grid=(t // tm,),
        in_specs=in_specs,
        out_specs=[pl.BlockSpec((tm, n_small), lambda i: (i, 0)),
                   pl.BlockSpec((tm, 3 * MIX_W), lambda i: (i, 0))],
        out_shape=[jax.ShapeDtypeStruct((t, n_small), F32),
                   jax.ShapeDtypeStruct((t, 3 * MIX_W), F32)],
        compiler_params=_params(1),
        name="proj",
    )(x, x, norm_g, w_in, mu_wag, w0, w1, w2, a0, a1, a2, g1, g2)


def _s5_body(u_ref, lre_ref, lim_ref, ldt_ref, bre_ref, bim_ref, cre_ref, cim_ref, d_ref, wglu_ref,
             o_ref, wre_s, wim_s, abar_s, xre_s, xim_s, hre_s, him_s, st_s, *, unroll):
    @pl.when(pl.program_id(0) == 0)
    def _():
        lr = lre_ref[...]
        li = lim_ref[...]
        dt = jnp.exp(ldt_ref[...])
        mag = jnp.exp(lr * dt)
        ar = mag * jnp.cos(li * dt)
        ai = mag * jnp.sin(li * dt)
        inv = 1.0 / (lr * lr + li * li)
        qr, qi = lr * inv, -li * inv
        coef_re = (ar - 1.0) * qr - ai * qi
        coef_im = (ar - 1.0) * qi + ai * qr
        wre_s[...] = (coef_re * bre_ref[...] - coef_im * bim_ref[...]).astype(BF16)
        wim_s[...] = (coef_re * bim_ref[...] + coef_im * bre_ref[...]).astype(BF16)
        abar_s[0:1, :] = ar
        abar_s[1:2, :] = ai
        st_s[...] = jnp.zeros_like(st_s)

    nb, tl = u_ref.shape[0], u_ref.shape[1]
    rows, n_state = nb * tl, wre_s.shape[1]
    u = u_ref[...].reshape(rows, MIX_W)
    ub = u.astype(BF16)
    blocks_per_row = tl // SUBLANES
    blocks = (nb * blocks_per_row, SUBLANES, n_state)
    xre_s[...] = jnp.dot(ub, wre_s[...], preferred_element_type=F32).reshape(blocks)
    xim_s[...] = jnp.dot(ub, wim_s[...], preferred_element_type=F32).reshape(blocks)
    ar = abar_s[0:1, :]
    ai = abar_s[1:2, :]

    def step(j, carry):
        carry = list(carry)
        for q in range(SUBLANES):
            for b in range(nb):
                blk = b * blocks_per_row + j
                hr, hi = carry[b]
                nr = ar * hr - ai * hi + xre_s[blk, q:q + 1, :]
                ni = ar * hi + ai * hr + xim_s[blk, q:q + 1, :]
                hre_s[blk, q:q + 1, :] = nr
                him_s[blk, q:q + 1, :] = ni
                carry[b] = (nr, ni)
        return tuple(carry)

    init = tuple((st_s[2 * b:2 * b + 1, :], st_s[2 * b + 1:2 * b + 2, :]) for b in range(nb))
    final = lax.fori_loop(0, blocks_per_row, step, init, unroll=unroll)
    for b in range(nb):
        st_s[2 * b:2 * b + 1, :] = final[b][0]
        st_s[2 * b + 1:2 * b + 2, :] = final[b][1]

    y = (_mm(hre_s[...].reshape(rows, n_state), cre_ref[...])
         - _mm(him_s[...].reshape(rows, n_state), cim_ref[...]) + d_ref[...] * u)
    gl = jax.nn.gelu(y, approximate=True)
    o_ref[...] = (gl * jax.nn.sigmoid(_mm(gl, wglu_ref[...]))).reshape(nb, tl, MIX_W)


def _s5_params(lam_re, lam_im, log_dt, b_re, b_im, c_re, c_im, d_skip, w_glu):
    depth = lam_re.shape[0]
    n_state = S5_GROUPS * S5_STATE
    eye = jnp.eye(S5_GROUPS, dtype=F32)
    row = lambda v: v.reshape(depth, 1, -1)
    b_bd = lambda b: jnp.einsum("lgph,gk->lghkp", b, eye).reshape(depth, MIX_W, n_state)
    c_bd = lambda c: jnp.einsum("lghp,gk->lgpkh", c, eye).reshape(depth, n_state, MIX_W)
    dt_rows = jnp.broadcast_to(log_dt[:, :, None], (depth, S5_GROUPS, S5_STATE))
    return (row(lam_re), row(lam_im), row(dt_rows), b_bd(b_re), b_bd(b_im),
            c_bd(c_re).astype(BF16), c_bd(c_im).astype(BF16), row(d_skip), w_glu.astype(BF16))


def _s5(p, params, layer, *, batch, seq, tl=256):
    n_state = S5_GROUPS * S5_STATE
    tl = min(tl, seq)
    in_specs = ([pl.BlockSpec((batch, tl, MIX_W), lambda i: (0, i, 0))]
                + [_layer_spec(a.shape, layer) for a in params])
    scan_buf = pltpu.VMEM((batch * tl // SUBLANES, SUBLANES, n_state), F32)
    out = pl.pallas_call(
        functools.partial(_s5_body, unroll=2),
        grid=(seq // tl,),
        in_specs=in_specs,
        out_specs=pl.BlockSpec((batch, tl, MIX_W), lambda i: (0, i, 0)),
        out_shape=jax.ShapeDtypeStruct((batch, seq, MIX_W), F32),
        scratch_shapes=[pltpu.VMEM((MIX_W, n_state), BF16), pltpu.VMEM((MIX_W, n_state), BF16),
                        pltpu.VMEM((2, n_state), F32)]
                       + [scan_buf] * 4 + [pltpu.VMEM((2 * batch, n_state), F32)],
        compiler_params=_params(0, 1),
        name="s5",
    )(p.reshape(batch, seq, p.shape[1]), *params)
    return out.reshape(batch * seq, MIX_W)


def _segment_ones(n, seg):
    r = lax.broadcasted_iota(jnp.int32, (n, n), 0) // seg
    c = lax.broadcasted_iota(jnp.int32, (n, n), 1) // seg
    return r == c


def _rwkv_body(rp_ref, kp_ref, vp_ref, rprev_ref, kprev_ref, vprev_ref, lora_ref, mu_ref, kk_ref,
               ka_ref, rk_ref, lnw_ref, lnb_ref, o_ref, st_s):
    nb, tl = rp_ref.shape[0], rp_ref.shape[1]
    rows = nb * tl
    c = RW_CHUNK
    first = pl.program_id(0) == 0

    @pl.when(first)
    def _():
        st_s[...] = jnp.zeros_like(st_s)

    row_id = lax.broadcasted_iota(jnp.int32, (rows, MIX_W), 0)

    def mixed(cur_ref, prev_ref, mu):
        cur = cur_ref[...].reshape(rows, MIX_W)
        shifted = pltpu.roll(cur, 1, 0)
        for b in range(nb):
            prev_row = jnp.where(first, 0.0, prev_ref[b, SUBLANES - 1:SUBLANES, :])
            shifted = jnp.where(row_id == b * tl, prev_row, shifted)
        return cur + (shifted - cur) * mu

    r = mixed(rp_ref, rprev_ref, mu_ref[0:1, :])
    k = mixed(kp_ref, kprev_ref, mu_ref[1:2, :])
    v = mixed(vp_ref, vprev_ref, mu_ref[2:3, :])
    lora = lora_ref[...].reshape(rows, 3 * MIX_W)
    logw = lora[:, 0:MIX_W]
    a = lora[:, MIX_W:2 * MIX_W]
    head_ones = _segment_ones(MIX_W, RW_HEAD).astype(BF16)
    kraw = k * kk_ref[...]
    kk = kraw / jnp.maximum(jnp.sqrt(_mm(kraw * kraw, head_ones)), 1e-12)
    kmod = k * (1.0 + (a - 1.0) * ka_ref[...])

    bdot = functools.partial(jnp.dot, preferred_element_type=F32)
    n_chunks = rows // c
    chunks_per_row = tl // c

    ti = lax.broadcasted_iota(jnp.int32, (rows, rows), 0)
    tj = lax.broadcasted_iota(jnp.int32, (rows, rows), 1)
    tri = (((ti // c) == (tj // c)) & (tj <= ti)).astype(BF16)
    w_hi = logw.astype(BF16)
    rem = logw - w_hi.astype(F32)
    w_mid = rem.astype(BF16)
    w_lo = (rem - w_mid.astype(F32)).astype(BF16)
    cum = bdot(tri, w_hi) + bdot(tri, w_mid) + bdot(tri, w_lo)
    tot = jnp.concatenate(
        [jnp.broadcast_to(cum[(n + 1) * c - 1:(n + 1) * c, :], (c, MIX_W)) for n in range(n_chunks)],
        axis=0)
    inv = jnp.exp(-cum)
    to_end = jnp.exp(tot - cum)
    r_dec = (r * jnp.exp(cum)).astype(BF16)
    a_dec = (-kk * jnp.exp(cum - logw)).astype(BF16)
    b_inv = (kk * a * inv).astype(BF16)
    k_inv = (kmod * inv).astype(BF16)
    b_end = (kk * a * to_end).astype(BF16)
    k_end = (kmod * to_end).astype(BF16)
    v_b = v.astype(BF16)
    p_end = jnp.exp(tot)

    pair_w = 2 * RW_HEAD
    n_pairs = MIX_W // pair_w
    row_i = lax.broadcasted_iota(jnp.int32, (c, pair_w), 0)
    lane_i = lax.broadcasted_iota(jnp.int32, (c, pair_w), 1)
    left = lane_i < RW_HEAD
    strict = (lane_i % RW_HEAD) < row_i
    incl = (lane_i % RW_HEAD) <= row_i
    eye_fam = ((lane_i % RW_HEAD) == row_i).astype(F32)
    pr = lax.broadcasted_iota(jnp.int32, (pair_w, pair_w), 0)
    pc = lax.broadcasted_iota(jnp.int32, (pair_w, pair_w), 1)
    same_head = (pr // RW_HEAD) == (pc // RW_HEAD)
    eye_pair = (pr == pc).astype(F32)

    def halves(x):
        z = jnp.zeros_like(x)
        return jnp.where(left, x, z), jnp.where(left, z, x)

    def blockdiag(x):
        return jnp.concatenate(halves(x), axis=0)

    probs = [(q, n) for q in range(n_pairs) for n in range(n_chunks)]
    n_probs = len(probs)

    def piece(x, q, n):
        return x[n * c:(n + 1) * c, q * pair_w:(q + 1) * pair_w]

    a_d = [piece(a_dec, q, n) for q, n in probs]
    r_d = [piece(r_dec, q, n) for q, n in probs]
    vv = [piece(v_b, q, n) for q, n in probs]
    vv_bd = [blockdiag(x) for x in vv]
    g = [_mm_nt(jnp.concatenate([a_d[i], r_d[i]], axis=0),
                jnp.concatenate(halves(piece(b_inv, q, n)) + halves(piece(k_inv, q, n)), axis=0))
         for i, (q, n) in enumerate(probs)]
    a_ab = [jnp.where(strict, x[0:c, 0:pair_w], 0.0) for x in g]
    a_ak = [jnp.where(strict, x[0:c, pair_w:2 * pair_w], 0.0).astype(BF16) for x in g]
    lhs_o = [jnp.concatenate([r_d[i], jnp.where(incl, x[c:2 * c, 0:pair_w], 0.0).astype(BF16),
                              jnp.where(incl, x[c:2 * c, pair_w:2 * pair_w], 0.0).astype(BF16)], axis=1)
             for i, x in enumerate(g)]
    tinv = [eye_fam + x for x in a_ab]
    pw = [x.astype(BF16) for x in a_ab]
    pw = [bdot(x, blockdiag(x)).astype(BF16) for x in pw]
    akv = [bdot(a_ak[i], vv_bd[i]).astype(BF16) for i in range(n_probs)]
    n_steps = int(math.log2(c)) - 1
    for s in range(n_steps):
        if s + 1 < n_steps:
            both = [bdot(jnp.concatenate([tinv[i].astype(BF16), pw[i]], axis=0), blockdiag(pw[i]))
                    for i in range(n_probs)]
            tinv = [tinv[i] + both[i][0:c] for i in range(n_probs)]
            pw = [x[c:2 * c].astype(BF16) for x in both]
        else:
            tinv = [tinv[i] + bdot(tinv[i].astype(BF16), blockdiag(pw[i])) for i in range(n_probs)]
    w1u0 = [bdot(tinv[i].astype(BF16), jnp.concatenate([blockdiag(a_d[i]), blockdiag(akv[i])], axis=1))
            for i in range(n_probs)]
    tn = [_mm_tn(piece(b_end, q, n), w1u0[i]) for i, (q, n) in enumerate(probs)]
    kv = [_mm_tn(piece(k_end, q, n), vv[i]) for i, (q, n) in enumerate(probs)]
    trans = [(jnp.where(same_head, tn[i][:, 0:pair_w], 0.0)
              + eye_pair * p_end[n * c:n * c + 1, q * pair_w:(q + 1) * pair_w]).astype(BF16)
             for i, (q, n) in enumerate(probs)]
    add = [jnp.where(same_head, tn[i][:, pair_w:2 * pair_w] + kv[i], 0.0) for i in range(n_probs)]
    w1 = [x[:, 0:pair_w].astype(BF16) for x in w1u0]
    u0 = [x[:, pair_w:2 * pair_w] for x in w1u0]

    st_at = {}
    chains = [(q, b) for q in range(n_pairs) for b in range(nb)]
    st = {qb: st_s[qb[0] * nb + qb[1]] for qb in chains}
    for step in range(chunks_per_row):
        for q, b in chains:
            i = q * n_chunks + b * chunks_per_row + step
            st_b = st[q, b].astype(BF16)
            st_at[i] = st_b
            st[q, b] = bdot(trans[i], st_b) + add[i]
    for q, b in chains:
        st_s[q * nb + b] = st[q, b]
    u = [(bdot(w1[i], st_at[i]) + u0[i]).astype(BF16) for i in range(n_probs)]
    o_p = [bdot(lhs_o[i], jnp.concatenate([st_at[i], blockdiag(u[i]), vv_bd[i]], axis=0))
           for i in range(n_probs)]
    o = jnp.concatenate(
        [jnp.concatenate(o_p[q * n_chunks:(q + 1) * n_chunks], axis=0) for q in range(n_pairs)], axis=1)

    inv_n = 1.0 / RW_HEAD
    mean = _mm(o, head_ones) * inv_n
    dlt = o - mean
    var = _mm(dlt * dlt, head_ones) * inv_n
    o_n = dlt * lax.rsqrt(var + RW_GN_EPS) * lnw_ref[...] + lnb_ref[...]
    bonus = _mm(r * kmod * rk_ref[...], head_ones) * v
    o_ref[...] = ((o_n + bonus) * lora[:, 2 * MIX_W:3 * MIX_W]).reshape(nb, tl, MIX_W)


def _rwkv(p, lora, params, layer, *, batch, seq, tl=256):
    tl = min(tl, seq)
    blocks_per_tile = tl // SUBLANES
    p3 = p.reshape(batch, seq, p.shape[1])
    lora3 = lora.reshape(batch, seq, lora.shape[1])

    def cur(col):
        return pl.BlockSpec((batch, tl, MIX_W), lambda i: (0, i, col))

    def prev(col):
        return pl.BlockSpec((batch, SUBLANES, MIX_W),
                            lambda i: (0, jnp.maximum(i * blocks_per_tile - 1, 0), col))

    in_specs = [cur(2), cur(3), cur(4), prev(2), prev(3), prev(4),
                pl.BlockSpec((batch, tl, 3 * MIX_W), lambda i: (0, i, 0))]
    in_specs += [_layer_spec(a.shape, layer) for a in params]
    pair_w = 2 * RW_HEAD
    out = pl.pallas_call(
        _rwkv_body,
        grid=(seq // tl,),
        in_specs=in_specs,
        out_specs=pl.BlockSpec((batch, tl, MIX_W), lambda i: (0, i, 0)),
        out_shape=jax.ShapeDtypeStruct((batch, seq, MIX_W), F32),
        scratch_shapes=[pltpu.VMEM((batch * MIX_W // pair_w, pair_w, pair_w), F32)],
        compiler_params=_params(0, 1),
        name="rwkv",
    )(p3, p3, p3, p3, p3, p3, lora3, *params)
    return out.reshape(batch * seq, MIX_W)


def _merge_body(x_ref, ub_ref, ubprev_ref, z_ref, zprev_ref, bg_ref, cg_ref, cgprev_ref, ya_ref,
                yc_ref, g_ref, wgate01_ref, wgate23_ref, wbr_ref, wout_ref, poolw_ref, pscale_ref,
                convw_ref, o_ref, *, seq, sub):
    i = pl.program_id(0)
    tm, d = x_ref.shape
    at_start = (i * tm) % seq == 0
    lane = lax.broadcasted_iota(jnp.int32, (sub, MIX_W), 1)
    group = lane // (MIX_W // len(POOL_WINDOWS))
    row = lax.broadcasted_iota(jnp.int32, (sub, MIX_W), 0)

    for r0 in range(0, tm, sub):
        rows = slice(r0, r0 + sub)
        x = x_ref[rows, :]
        h = _rmsnorm(x, g_ref[...]).astype(BF16)

        u = ub_ref[rows, :]
        if r0 == 0:
            u_halo = jnp.where(at_start, 0.0, ubprev_ref[...])
            z_halo = jnp.where(at_start, 0.0, cgprev_ref[...] * zprev_ref[...])
        else:
            u_halo = ub_ref[r0 - POOL_HALO:r0, :]
            z_halo = cg_ref[r0 - SUBLANES:r0, :] * z_ref[r0 - SUBLANES:r0, :]
        ext = jnp.concatenate([u_halo, u], axis=0)
        sums = {1: ext}
        span = 1
        while span < POOL_WINDOWS[-1]:
            sums[2 * span] = sums[span] + pltpu.roll(sums[span], span, 0)
            span *= 2
        win_sum = sums[POOL_WINDOWS[-1]][POOL_HALO:, :]
        win = jnp.full((sub, MIX_W), float(POOL_WINDOWS[-1]), F32)
        for gi in range(len(POOL_WINDOWS) - 2, -1, -1):
            win_sum = jnp.where(group == gi, sums[POOL_WINDOWS[gi]][POOL_HALO:, :], win_sum)
            win = jnp.where(group == gi, float(POOL_WINDOWS[gi]), win)
        pos = (i * tm + r0) % seq + row
        count = jnp.minimum((pos + 1).astype(F32), win)
        y_pool = _mm(win_sum / count - u, poolw_ref[...]) * pscale_ref[...]

        zext = jnp.concatenate([z_halo, cg_ref[rows, :] * z_ref[rows, :]], axis=0)
        conv = (convw_ref[2:3, :] * zext
                + convw_ref[1:2, :] * pltpu.roll(zext, 1, 0)
                + convw_ref[0:1, :] * pltpu.roll(zext, 2, 0))
        y_conv = bg_ref[rows, :] * conv[SUBLANES:, :]

        ys = [y.astype(BF16) for y in (ya_ref[rows, :], y_pool, yc_ref[rows, :], y_conv)]
        merged = []
        for j in range(d // MIX_W):
            cols = slice(j * MIX_W, (j + 1) * MIX_W)
            acc = None
            for gi in range(N_BRANCH):
                wgate_ref = (wgate01_ref, wgate23_ref)[gi // 2]
                gcols = slice((gi % 2) * d + j * MIX_W, (gi % 2) * d + (j + 1) * MIX_W)
                gate = jax.nn.sigmoid(jnp.dot(h, wgate_ref[:, gcols], preferred_element_type=F32))
                term = jnp.dot(ys[gi], wbr_ref[gi, :, cols], preferred_element_type=F32) * gate
                acc = term if acc is None else acc + term
            merged.append(acc.astype(BF16))
        o_ref[rows, :] = x + jnp.dot(jnp.concatenate(merged, axis=1), wout_ref[...],
                                     preferred_element_type=F32)


def _merge(x, p, y_s5, y_rwkv, norm_g, w_in, w_branch, w_out, layer, pool_bd, pool_scale, conv_w, *,
           seq, tm=1024, sub=512):
    t, d = x.shape
    tm = min(tm, seq)
    gate_block = 2 * d
    first_gate_block = 8 * MIX_W // gate_block

    def cur(col):
        return pl.BlockSpec((tm, MIX_W), lambda i: (i, col))

    def prev(col, rows):
        per_tile = tm // rows
        return pl.BlockSpec((rows, MIX_W), lambda i: (jnp.maximum(i * per_tile - 1, 0), col))

    in_specs = [
        pl.BlockSpec((tm, d), lambda i: (i, 0)),
        cur(1), prev(1, POOL_HALO), cur(5), prev(5, SUBLANES), cur(6), cur(7), prev(7, SUBLANES),
        pl.BlockSpec((tm, MIX_W), lambda i: (i, 0)),
        pl.BlockSpec((tm, MIX_W), lambda i: (i, 0)),
        _layer_spec(norm_g.shape, layer),
        _layer_spec(w_in.shape, layer, col_block=(gate_block, first_gate_block)),
        _layer_spec(w_in.shape, layer, col_block=(gate_block, first_gate_block + 1)),
        _layer_spec(w_branch.shape, layer), _layer_spec(w_out.shape, layer),
        _layer_spec(pool_bd.shape, layer), _layer_spec(pool_scale.shape, layer),
        _layer_spec(conv_w.shape, layer),
    ]
    return pl.pallas_call(
        functools.partial(_merge_body, seq=seq, sub=min(sub, tm)),
        grid=(t // tm,),
        in_specs=in_specs,
        out_specs=pl.BlockSpec((tm, d), lambda i: (i, 0)),
        out_shape=jax.ShapeDtypeStruct((t, d), F32),
        compiler_params=_params(1),
        name="merge",
    )(x, p, p, p, p, p, p, p, y_s5, y_rwkv, norm_g, w_in, w_in, w_branch, w_out,
      pool_bd, pool_scale, conv_w)


def kernel(x, ffn1_norm, ffn1_w_gate, ffn1_w_up, ffn1_w_down, mix_norm, w_in, s5_lambda_re, s5_lambda_im, s5_log_dt, s5_b_re, s5_b_im, s5_c_re, s5_c_im, s5_d, s5_w_glu, pool_w, pool_scale, rwkv_mu_rkv, rwkv_mu_wag, rwkv_w0, rwkv_w1, rwkv_w2, rwkv_a0, rwkv_a1, rwkv_a2, rwkv_g1, rwkv_g2, rwkv_k_k, rwkv_k_a, rwkv_r_k, rwkv_ln_w, rwkv_ln_b, conv_w, w_branch, w_out, ffn2_norm, ffn2_w_gate, ffn2_w_up, ffn2_w_down, final_norm):
    batch, seq, d = x.shape
    depth = w_in.shape[0]
    bf = lambda w: w.astype(BF16)
    ffn1 = (ffn1_w_gate, ffn1_w_up, ffn1_w_down)
    ffn2 = (ffn2_w_gate, ffn2_w_up, ffn2_w_down)
    w_in_b, w_branch_b, w_out_b = bf(w_in), bf(w_branch), bf(w_out)
    vec = lambda a: a.reshape(depth, 1, -1)
    ffn1_g, ffn2_g, mix_g = vec(ffn1_norm), vec(ffn2_norm), vec(mix_norm)
    proj_params = (rwkv_mu_wag, vec(rwkv_w0), bf(rwkv_w1), rwkv_w2, vec(rwkv_a0), bf(rwkv_a1),
                   rwkv_a2, bf(rwkv_g1), rwkv_g2)
    s5_params = _s5_params(s5_lambda_re, s5_lambda_im, s5_log_dt, s5_b_re, s5_b_im, s5_c_re, s5_c_im,
                           s5_d, s5_w_glu)
    rwkv_params = (rwkv_mu_rkv, vec(rwkv_k_k), vec(rwkv_k_a), vec(rwkv_r_k), vec(rwkv_ln_w),
                   vec(rwkv_ln_b))
    pool_groups = len(POOL_WINDOWS)
    pool_bd = jnp.einsum("lgcd,gk->lgckd", pool_w, jnp.eye(pool_groups, dtype=F32)).reshape(
        depth, MIX_W, MIX_W)
    pool_s = vec(pool_scale)
    xf = x.reshape(batch * seq, d)
    for l in range(depth):
        xf = _ffn(xf, ffn1_g, *ffn1, l)
        p, lora = _proj(xf, mix_g, w_in_b, l, *proj_params, seq=seq)
        y_s5 = _s5(p, s5_params, l, batch=batch, seq=seq)
        y_rwkv = _rwkv(p, lora, rwkv_params, l, batch=batch, seq=seq)
        xf = _merge(xf, p, y_s5, y_rwkv, mix_g, w_in_b, w_branch_b, w_out_b, l,
                    pool_bd, pool_s, conv_w, seq=seq)
        xf = _ffn(xf, ffn2_g, *ffn2, l, final_norm if l == depth - 1 else None)
    return xf.reshape(batch, seq, d)
```

```python
import functools
import math

import jax
import jax.numpy as jnp
from jax import lax
from jax.experimental import pallas as pl
from jax.experimental.pallas import tpu as pltpu

F32 = jnp.float32
BF16 = jnp.bfloat16

MIX_W = 256
N_BRANCH = 4
S5_GROUPS = 16
S5_STATE = 64
POOL_WINDOWS = (2, 4, 8, 16)
POOL_HALO = 16
RW_HEAD = 64
RW_CHUNK = 64
RW_GN_EPS = 64e-5
NORM_EPS = 1e-6
SUBLANES = 8
VMEM_LIMIT_BYTES = 56 * 1024 * 1024


def _params(n_parallel, n_arbitrary=0):
    return pltpu.CompilerParams(
        dimension_semantics=("parallel",) * n_parallel + ("arbitrary",) * n_arbitrary,
        vmem_limit_bytes=VMEM_LIMIT_BYTES)


def _const_spec(shape):
    zeros = (0,) * len(shape)
    return pl.BlockSpec(shape, lambda *_: zeros, pipeline_mode=pl.Buffered(1))


def _layer_spec(stacked_shape, layer, col_block=None):
    shape = tuple(stacked_shape[1:])
    tail = (0,) * (len(shape) - 1)
    if col_block is None:
        index = (layer,) + tail + (0,)
    else:
        width, col = col_block
        shape = shape[:-1] + (width,)
        index = (layer,) + tail + (col,)
    return pl.BlockSpec((None,) + shape, lambda *_: index, pipeline_mode=pl.Buffered(1))


def _rmsnorm(x, g):
    return x * lax.rsqrt(jnp.mean(x * x, axis=-1, keepdims=True) + NORM_EPS) * g


def _mm(a, b):
    return jnp.dot(a.astype(BF16), b.astype(BF16), preferred_element_type=F32)


def _mm_nt(a, b):
    return lax.dot_general(a.astype(BF16), b.astype(BF16), (((1,), (1,)), ((), ())),
                           preferred_element_type=F32)


def _mm_tn(a, b):
    return lax.dot_general(a.astype(BF16), b.astype(BF16), (((0,), (0,)), ((), ())),
                           preferred_element_type=F32)


def _shift_rows(cur, prev_row):
    rolled = pltpu.roll(cur, 1, 0)
    row = lax.broadcasted_iota(jnp.int32, cur.shape, 0)
    return jnp.where(row == 0, prev_row, rolled)


def _softplus(z):
    return jnp.maximum(z, 0.0) + jnp.log1p(jnp.exp(-jnp.abs(z)))


def _ffn_body(x_ref, g_ref, wg_ref, wu_ref, wd_ref, *rest, n_chunks, final):
    if final:
        fg_ref, o_ref, wg_s, wu_s, wd_s, h_s, acc_ref = rest
    else:
        o_ref, wg_s, wu_s, wd_s, h_s, acc_ref = rest
    s = pl.program_id(0)

    def chunk(h, j):
        gate = jnp.dot(h, wg_s[j], preferred_element_type=F32)
        up = jnp.dot(h, wu_s[j], preferred_element_type=F32)
        act = (gate * jax.nn.sigmoid(gate) * up).astype(BF16)
        return jnp.dot(act, wd_s[j], preferred_element_type=F32)

    def finish():
        y = x_ref[...] + 0.5 * acc_ref[...]
        if final:
            y = _rmsnorm(y, fg_ref[...])
        o_ref[...] = y

    @pl.when(s < n_chunks)
    def _():
        wg_s[s] = wg_ref[...].astype(BF16)
        wu_s[s] = wu_ref[...].astype(BF16)
        wd_s[s] = wd_ref[...].astype(BF16)

        @pl.when(s == 0)
        def _():
            h_s[...] = _rmsnorm(x_ref[...], g_ref[...]).astype(BF16)
            acc_ref[...] = jnp.zeros_like(acc_ref)

        acc_ref[...] += chunk(h_s[...], s)

        @pl.when(s == n_chunks - 1)
        def _():
            finish()

    @pl.when(s >= n_chunks)
    def _():
        h = _rmsnorm(x_ref[...], g_ref[...]).astype(BF16)
        for j in range(n_chunks):
            contrib = chunk(h, j)
            if j == 0:
                acc_ref[...] = contrib
            else:
                acc_ref[...] += contrib
        finish()


def _ffn(x, norm_g, w_gate, w_up, w_down, layer, final_g=None, *, tm=1024, f_chunk=256):
    t, d = x.shape
    d_ff = w_gate.shape[2]
    tm = min(tm, t)
    n_chunks = d_ff // f_chunk
    final = final_g is not None
    tile = lambda s: (jnp.maximum(s - (n_chunks - 1), 0), 0)
    in_specs = [
        pl.BlockSpec((tm, d), tile),
        _layer_spec(norm_g.shape, layer),
        pl.BlockSpec((None, d, f_chunk), lambda s: (layer, 0, jnp.minimum(s, n_chunks - 1))),
        pl.BlockSpec((None, d, f_chunk), lambda s: (layer, 0, jnp.minimum(s, n_chunks - 1))),
        pl.BlockSpec((None, f_chunk, d), lambda s: (layer, jnp.minimum(s, n_chunks - 1), 0)),
    ]
    args = [x, norm_g, w_gate, w_up, w_down]
    if final:
        in_specs.append(_const_spec((1, d)))
        args.append(final_g.reshape(1, d))
    return pl.pallas_call(
        functools.partial(_ffn_body, n_chunks=n_chunks, final=final),
        grid=(n_chunks - 1 + t // tm,),
        in_specs=in_specs,
        out_specs=pl.BlockSpec((tm, d), tile),
        out_shape=jax.ShapeDtypeStruct((t, d), F32),
        scratch_shapes=[pltpu.VMEM((n_chunks, d, f_chunk), BF16), pltpu.VMEM((n_chunks, d, f_chunk), BF16),
                        pltpu.VMEM((n_chunks, f_chunk, d), BF16), pltpu.VMEM((tm, d), BF16),
                        pltpu.VMEM((tm, d), F32)],
        compiler_params=_params(0, 1),
        name="ffn_final" if final else "ffn",
    )(*args)


def _proj_body(x_ref, xprev_ref, g_ref, win_ref, mu_ref, w0_ref, w1_ref, w2_ref, a0_ref, a1_ref,
               a2_ref, g1_ref, g2_ref, p_ref, lora_ref, *, seq):
    i = pl.program_id(0)
    tm = x_ref.shape[0]
    g = g_ref[...]
    h = _rmsnorm(x_ref[...], g)
    at_start = (i * tm) % seq == 0
    h_prev = _rmsnorm(xprev_ref[SUBLANES - 1:SUBLANES, :], g)
    h_prev = jnp.where(at_start, 0.0, h_prev)
    hx = _shift_rows(h, h_prev) - h
    hb = h.astype(BF16)

    def main_cols(lo, hi):
        for j in range(lo, hi):
            cols = slice(j * MIX_W, (j + 1) * MIX_W)
            p_ref[:, cols] = jnp.dot(hb, win_ref[:, cols], preferred_element_type=F32)

    main_cols(0, 3)
    xw = (h + hx * mu_ref[0:1, :]).astype(BF16)
    t_w = jnp.tanh(jnp.dot(xw, w1_ref[...], preferred_element_type=F32))
    main_cols(3, 5)
    xa = (h + hx * mu_ref[1:2, :]).astype(BF16)
    t_a = jnp.dot(xa, a1_ref[...], preferred_element_type=F32)
    main_cols(5, 7)
    xg = (h + hx * mu_ref[2:3, :]).astype(BF16)
    t_g = jax.nn.sigmoid(jnp.dot(xg, g1_ref[...], preferred_element_type=F32))
    main_cols(7, 8)
    w_log = -_softplus(-(w0_ref[...] + _mm(t_w, w2_ref[...]))) - 0.5
    lora_ref[:, 0:MIX_W] = -jnp.exp(w_log)
    lora_ref[:, MIX_W:2 * MIX_W] = jax.nn.sigmoid(a0_ref[...] + _mm(t_a, a2_ref[...]))
    lora_ref[:, 2 * MIX_W:3 * MIX_W] = _mm(t_g, g2_ref[...])


def _proj(x, norm_g, w_in, layer, mu_wag, w0, w1, w2, a0, a1, a2, g1, g2, *, seq, tm=1024):
    t, d = x.shape
    tm = min(tm, seq)
    n_small = 8 * MIX_W
    blocks_per_tile = tm // SUBLANES
    in_specs = [
        pl.BlockSpec((tm, d), lambda i: (i, 0)),
        pl.BlockSpec((SUBLANES, d), lambda i: (jnp.maximum(i * blocks_per_tile - 1, 0), 0)),
        _layer_spec(norm_g.shape, layer),
        _layer_spec(w_in.shape, layer, col_block=(n_small, 0)),
    ] + [_layer_spec(a.shape, layer) for a in (mu_wag, w0, w1, w2, a0, a1, a2, g1, g2)]
    return pl.pallas_call(
        functools.partial(_proj_body, seq=seq),
        grid=(t // tm,),
        in_specs=in_specs,
        out_specs=[pl.BlockSpec((tm, n_small), lambda i: (i, 0)),
                   pl.BlockSpec((tm, 3 * MIX_W), lambda i: (i, 0))],
        out_shape=[jax.ShapeDtypeStruct((t, n_small), F32),
                   jax.ShapeDtypeStruct((t, 3 * MIX_W), F32)],
        compiler_params=_params(1),
        name="proj",
    )(x, x, norm_g, w_in, mu_wag, w0, w1, w2, a0, a1, a2, g1, g2)


def _s5_body(u_ref, lre_ref, lim_ref, ldt_ref, bre_ref, bim_ref, cre_ref, cim_ref, d_ref, wglu_ref,
             o_ref, wre_s, wim_s, abar_s, xre_s, xim_s, hre_s, him_s, st_s, *, unroll):
    @pl.when(pl.program_id(0) == 0)
    def _():
        lr = lre_ref[...]
        li = lim_ref[...]
        dt = jnp.exp(ldt_ref[...])
        mag = jnp.exp(lr * dt)
        ar = mag * jnp.cos(li * dt)
        ai = mag * jnp.sin(li * dt)
        inv = 1.0 / (lr * lr + li * li)
        qr, qi = lr * inv, -li * inv
        coef_re = (ar - 1.0) * qr - ai * qi
        coef_im = (ar - 1.0) * qi + ai * qr
        wre_s[...] = (coef_re * bre_ref[...] - coef_im * bim_ref[...]).astype(BF16)
        wim_s[...] = (coef_re * bim_ref[...] + coef_im * bre_ref[...]).astype(BF16)
        abar_s[0:1, :] = ar
        abar_s[1:2, :] = ai
        st_s[...] = jnp.zeros_like(st_s)

    nb, tl = u_ref.shape[0], u_ref.shape[1]
    rows, n_state = nb * tl, wre_s.shape[1]
    u = u_ref[...].reshape(rows, MIX_W)
    ub = u.astype(BF16)
    blocks_per_row = tl // SUBLANES
    blocks = (nb * blocks_per_row, SUBLANES, n_state)
    xre_s[...] = jnp.dot(ub, wre_s[...], preferred_element_type=F32).reshape(blocks)
    xim_s[...] = jnp.dot(ub, wim_s[...], preferred_element_type=F32).reshape(blocks)
    ar = abar_s[0:1, :]
    ai = abar_s[1:2, :]

    def step(j, carry):
        carry = list(carry)
        for q in range(SUBLANES):
            for b in range(nb):
                blk = b * blocks_per_row + j
                hr, hi = carry[b]
                nr = ar * hr - ai * hi + xre_s[blk, q:q + 1, :]
                ni = ar * hi + ai * hr + xim_s[blk, q:q + 1, :]
                hre_s[blk, q:q + 1, :] = nr
                him_s[blk, q:q + 1, :] = ni
                carry[b] = (nr, ni)
        return tuple(carry)

    init = tuple((st_s[2 * b:2 * b + 1, :], st_s[2 * b + 1:2 * b + 2, :]) for b in range(nb))
    final = lax.fori_loop(0, blocks_per_row, step, init, unroll=unroll)
    for b in range(nb):
        st_s[2 * b:2 * b + 1, :] = final[b][0]
        st_s[2 * b + 1:2 * b + 2, :] = final[b][1]

    y = (_mm(hre_s[...].reshape(rows, n_state), cre_ref[...])
         - _mm(him_s[...].reshape(rows, n_state), cim_ref[...]) + d_ref[...] * u)
    gl = jax.nn.gelu(y, approximate=True)
    o_ref[...] = (gl * jax.nn.sigmoid(_mm(gl, wglu_ref[...]))).reshape(nb, tl, MIX_W)


def _s5_params(lam_re, lam_im, log_dt, b_re, b_im, c_re, c_im, d_skip, w_glu):
    depth = lam_re.shape[0]
    n_state = S5_GROUPS * S5_STATE
    eye = jnp.eye(S5_GROUPS, dtype=F32)
    row = lambda v: v.reshape(depth, 1, -1)
    b_bd = lambda b: jnp.einsum("lgph,gk->lghkp", b, eye).reshape(depth, MIX_W, n_state)
    c_bd = lambda c: jnp.einsum("lghp,gk->lgpkh", c, eye).reshape(depth, n_state, MIX_W)
    dt_rows = jnp.broadcast_to(log_dt[:, :, None], (depth, S5_GROUPS, S5_STATE))
    return (row(lam_re), row(lam_im), row(dt_rows), b_bd(b_re), b_bd(b_im),
            c_bd(c_re).astype(BF16), c_bd(c_im).astype(BF16), row(d_skip), w_glu.astype(BF16))


def _s5(p, params, layer, *, batch, seq, tl=512):
    n_state = S5_GROUPS * S5_STATE
    tl = min(tl, seq)
    in_specs = ([pl.BlockSpec((batch, tl, MIX_W), lambda i: (0, i, 0))]
                + [_layer_spec(a.shape, layer) for a in params])
    scan_buf = pltpu.VMEM((batch * tl // SUBLANES, SUBLANES, n_state), F32)
    out = pl.pallas_call(
        functools.partial(_s5_body, unroll=2),
        grid=(seq // tl,),
        in_specs=in_specs,
        out_specs=pl.BlockSpec((batch, tl, MIX_W), lambda i: (0, i, 0)),
        out_shape=jax.ShapeDtypeStruct((batch, seq, MIX_W), F32),
        scratch_shapes=[pltpu.VMEM((MIX_W, n_state), BF16), pltpu.VMEM((MIX_W, n_state), BF16),
                        pltpu.VMEM((2, n_state), F32)]
                       + [scan_buf] * 4 + [pltpu.VMEM((2 * batch, n_state), F32)],
        compiler_params=_params(0, 1),
        name="s5",
    )(p.reshape(batch, seq, p.shape[1]), *params)
    return out.reshape(batch * seq, MIX_W)


def _segment_ones(n, seg):
    r = lax.broadcasted_iota(jnp.int32, (n, n), 0) // seg
    c = lax.broadcasted_iota(jnp.int32, (n, n), 1) // seg
    return r == c


def _rwkv_body(rp_ref, kp_ref, vp_ref, rprev_ref, kprev_ref, vprev_ref, lora_ref, mu_ref, kk_ref,
               ka_ref, rk_ref, lnw_ref, lnb_ref, o_ref, st_s):
    nb, tl = rp_ref.shape[0], rp_ref.shape[1]
    rows = nb * tl
    c = RW_CHUNK
    first = pl.program_id(0) == 0

    @pl.when(first)
    def _():
        st_s[...] = jnp.zeros_like(st_s)

    row_id = lax.broadcasted_iota(jnp.int32, (rows, MIX_W), 0)

    def mixed(cur_ref, prev_ref, mu):
        cur = cur_ref[...].reshape(rows, MIX_W)
        shifted = pltpu.roll(cur, 1, 0)
        for b in range(nb):
            prev_row = jnp.where(first, 0.0, prev_ref[b, SUBLANES - 1:SUBLANES, :])
            shifted = jnp.where(row_id == b * tl, prev_row, shifted)
        return cur + (shifted - cur) * mu

    r = mixed(rp_ref, rprev_ref, mu_ref[0:1, :])
    k = mixed(kp_ref, kprev_ref, mu_ref[1:2, :])
    v = mixed(vp_ref, vprev_ref, mu_ref[2:3, :])
    lora = lora_ref[...].reshape(rows, 3 * MIX_W)
    logw = lora[:, 0:MIX_W]
    a = lora[:, MIX_W:2 * MIX_W]
    head_ones = _segment_ones(MIX_W, RW_HEAD).astype(BF16)
    kraw = k * kk_ref[...]
    kk = kraw / jnp.maximum(jnp.sqrt(_mm(kraw * kraw, head_ones)), 1e-12)
    kmod = k * (1.0 + (a - 1.0) * ka_ref[...])

    bdot = functools.partial(jnp.dot, preferred_element_type=F32)
    n_chunks = rows // c
    chunks_per_row = tl // c

    ti = lax.broadcasted_iota(jnp.int32, (rows, rows), 0)
    tj = lax.broadcasted_iota(jnp.int32, (rows, rows), 1)
    tri = (((ti // c) == (tj // c)) & (tj <= ti)).astype(BF16)
    w_hi = logw.astype(BF16)
    rem = logw - w_hi.astype(F32)
    w_mid = rem.astype(BF16)
    w_lo = (rem - w_mid.astype(F32)).astype(BF16)
    cum = bdot(tri, w_hi) + bdot(tri, w_mid) + bdot(tri, w_lo)
    tot = jnp.concatenate(
        [jnp.broadcast_to(cum[(n + 1) * c - 1:(n + 1) * c, :], (c, MIX_W)) for n in range(n_chunks)],
        axis=0)
    inv = jnp.exp(-cum)
    to_end = jnp.exp(tot - cum)
    r_dec = (r * jnp.exp(cum)).astype(BF16)
    a_dec = (-kk * jnp.exp(cum - logw)).astype(BF16)
    b_inv = (kk * a * inv).astype(BF16)
    k_inv = (kmod * inv).astype(BF16)
    b_end = (kk * a * to_end).astype(BF16)
    k_end = (kmod * to_end).astype(BF16)
    v_b = v.astype(BF16)
    p_end = [jnp.exp(cum[(n + 1) * c - 1:(n + 1) * c, :]) for n in range(n_chunks)]

    pair_w = 2 * RW_HEAD
    n_pairs = MIX_W // pair_w
    row_i = lax.broadcasted_iota(jnp.int32, (c, pair_w), 0)
    lane_i = lax.broadcasted_iota(jnp.int32, (c, pair_w), 1)
    left = lane_i < RW_HEAD
    strict = (lane_i % RW_HEAD) < row_i
    incl = (lane_i % RW_HEAD) <= row_i
    eye_fam = ((lane_i % RW_HEAD) == row_i).astype(F32)
    pr = lax.broadcasted_iota(jnp.int32, (pair_w, pair_w), 0)
    pc = lax.broadcasted_iota(jnp.int32, (pair_w, pair_w), 1)
    same_head = (pr // RW_HEAD) == (pc // RW_HEAD)
    eye_pair = (pr == pc).astype(F32)

    def halves(x):
        z = jnp.zeros_like(x)
        return jnp.where(left, x, z), jnp.where(left, z, x)

    def blockdiag(x):
        return jnp.concatenate(halves(x), axis=0)

    probs = [(q, n) for q in range(n_pairs) for n in range(n_chunks)]
    n_probs = len(probs)

    def piece(x, q, n):
        return x[n * c:(n + 1) * c, q * pair_w:(q + 1) * pair_w]

    a_d = [piece(a_dec, q, n) for q, n in probs]
    r_d = [piece(r_dec, q, n) for q, n in probs]
    vv = [piece(v_b, q, n) for q, n in probs]
    vv_bd = [blockdiag(x) for x in vv]
    g = [_mm_nt(jnp.concatenate([a_d[i], r_d[i]], axis=0),
                jnp.concatenate(halves(piece(b_inv, q, n)) + halves(piece(k_inv, q, n)), axis=0))
         for i, (q, n) in enumerate(probs)]
    a_ab = [jnp.where(strict, x[0:c, 0:pair_w], 0.0) for x in g]
    a_ak = [jnp.where(strict, x[0:c, pair_w:2 * pair_w], 0.0).astype(BF16) for x in g]
    lhs_o = [jnp.concatenate([r_d[i], jnp.where(incl, x[c:2 * c, 0:pair_w], 0.0).astype(BF16),
                              jnp.where(incl, x[c:2 * c, pair_w:2 * pair_w], 0.0).astype(BF16)], axis=1)
             for i, x in enumerate(g)]
    tinv = [eye_fam + x for x in a_ab]
    pw = [x.astype(BF16) for x in a_ab]
    pw = [bdot(x, blockdiag(x)).astype(BF16) for x in pw]
    akv = [bdot(a_ak[i], vv_bd[i]).astype(BF16) for i in range(n_probs)]
    n_steps = int(math.log2(c)) - 1
    for s in range(n_steps):
        if s + 1 < n_steps:
            both = [bdot(jnp.concatenate([tinv[i].astype(BF16), pw[i]], axis=0), blockdiag(pw[i]))
                    for i in range(n_probs)]
            tinv = [tinv[i] + both[i][0:c] for i in range(n_probs)]
            pw = [x[c:2 * c].astype(BF16) for x in both]
        else:
            tinv = [tinv[i] + bdot(tinv[i].astype(BF16), blockdiag(pw[i])) for i in range(n_probs)]
    w1u0 = [bdot(tinv[i].astype(BF16), jnp.concatenate([blockdiag(a_d[i]), blockdiag(akv[i])], axis=1))
            for i in range(n_probs)]
    tn = [_mm_tn(piece(b_end, q, n), w1u0[i]) for i, (q, n) in enumerate(probs)]
    kv = [_mm_tn(piece(k_end, q, n), vv[i]) for i, (q, n) in enumerate(probs)]
    trans = [(jnp.where(same_head, tn[i][:, 0:pair_w], 0.0)
              + eye_pair * p_end[n][:, q * pair_w:(q + 1) * pair_w]).astype(BF16)
             for i, (q, n) in enumerate(probs)]
    add = [jnp.where(same_head, tn[i][:, pair_w:2 * pair_w] + kv[i], 0.0) for i in range(n_probs)]
    w1 = [x[:, 0:pair_w].astype(BF16) for x in w1u0]
    u0 = [x[:, pair_w:2 * pair_w] for x in w1u0]

    st_at = {}
    chains = [(q, b) for q in range(n_pairs) for b in range(nb)]
    st = {qb: st_s[qb[0] * nb + qb[1]] for qb in chains}
    for step in range(chunks_per_row):
        for q, b in chains:
            i = q * n_chunks + b * chunks_per_row + step
            st_b = st[q, b].astype(BF16)
            st_at[i] = st_b
            st[q, b] = bdot(trans[i], st_b) + add[i]
    for q, b in chains:
        st_s[q * nb + b] = st[q, b]
    u = [(bdot(w1[i], st_at[i]) + u0[i]).astype(BF16) for i in range(n_probs)]
    o_p = [bdot(lhs_o[i], jnp.concatenate([st_at[i], blockdiag(u[i]), vv_bd[i]], axis=0))
           for i in range(n_probs)]
    o = jnp.concatenate(
        [jnp.concatenate(o_p[q * n_chunks:(q + 1) * n_chunks], axis=0) for q in range(n_pairs)], axis=1)

    inv_n = 1.0 / RW_HEAD
    mean = _mm(o, head_ones) * inv_n
    dlt = o - mean
    var = _mm(dlt * dlt, head_ones) * inv_n
    o_n = dlt * lax.rsqrt(var + RW_GN_EPS) * lnw_ref[...] + lnb_ref[...]
    bonus = _mm(r * kmod * rk_ref[...], head_ones) * v
    o_ref[...] = ((o_n + bonus) * lora[:, 2 * MIX_W:3 * MIX_W]).reshape(nb, tl, MIX_W)


def _rwkv(p, lora, params, layer, *, batch, seq, tl=256):
    tl = min(tl, seq)
    blocks_per_tile = tl // SUBLANES
    p3 = p.reshape(batch, seq, p.shape[1])
    lora3 = lora.reshape(batch, seq, lora.shape[1])

    def cur(col):
        return pl.BlockSpec((batch, tl, MIX_W), lambda i: (0, i, col))

    def prev(col):
        return pl.BlockSpec((batch, SUBLANES, MIX_W),
                            lambda i: (0, jnp.maximum(i * blocks_per_tile - 1, 0), col))

    in_specs = [cur(2), cur(3), cur(4), prev(2), prev(3), prev(4),
                pl.BlockSpec((batch, tl, 3 * MIX_W), lambda i: (0, i, 0))]
    in_specs += [_layer_spec(a.shape, layer) for a in params]
    pair_w = 2 * RW_HEAD
    out = pl.pallas_call(
        _rwkv_body,
        grid=(seq // tl,),
        in_specs=in_specs,
        out_specs=pl.BlockSpec((batch, tl, MIX_W), lambda i: (0, i, 0)),
        out_shape=jax.ShapeDtypeStruct((batch, seq, MIX_W), F32),
        scratch_shapes=[pltpu.VMEM((batch * MIX_W // pair_w, pair_w, pair_w), F32)],
        compiler_params=_params(0, 1),
        name="rwkv",
    )(p3, p3, p3, p3, p3, p3, lora3, *params)
    return out.reshape(batch * seq, MIX_W)


def _merge_body(x_ref, ub_ref, ubprev_ref, z_ref, zprev_ref, bg_ref, cg_ref, cgprev_ref, ya_ref,
                yc_ref, g_ref, wgate01_ref, wgate23_ref, wbr_ref, wout_ref, poolw_ref, pscale_ref,
                convw_ref, o_ref, *, seq, sub):
    i = pl.program_id(0)
    tm, d = x_ref.shape
    at_start = (i * tm) % seq == 0
    lane = lax.broadcasted_iota(jnp.int32, (sub, MIX_W), 1)
    group = lane // (MIX_W // len(POOL_WINDOWS))
    row = lax.broadcasted_iota(jnp.int32, (sub, MIX_W), 0)

    for r0 in range(0, tm, sub):
        rows = slice(r0, r0 + sub)
        x = x_ref[rows, :]
        h = _rmsnorm(x, g_ref[...]).astype(BF16)

        u = ub_ref[rows, :]
        if r0 == 0:
            u_halo = jnp.where(at_start, 0.0, ubprev_ref[...])
            z_halo = jnp.where(at_start, 0.0, cgprev_ref[...] * zprev_ref[...])
        else:
            u_halo = ub_ref[r0 - POOL_HALO:r0, :]
            z_halo = cg_ref[r0 - SUBLANES:r0, :] * z_ref[r0 - SUBLANES:r0, :]
        ext = jnp.concatenate([u_halo, u], axis=0)
        sums = {1: ext}
        span = 1
        while span < POOL_WINDOWS[-1]:
            sums[2 * span] = sums[span] + pltpu.roll(sums[span], span, 0)
            span *= 2
        win_sum = sums[POOL_WINDOWS[-1]][POOL_HALO:, :]
        win = jnp.full((sub, MIX_W), float(POOL_WINDOWS[-1]), F32)
        for gi in range(len(POOL_WINDOWS) - 2, -1, -1):
            win_sum = jnp.where(group == gi, sums[POOL_WINDOWS[gi]][POOL_HALO:, :], win_sum)
            win = jnp.where(group == gi, float(POOL_WINDOWS[gi]), win)
        pos = (i * tm + r0) % seq + row
        count = jnp.minimum((pos + 1).astype(F32), win)
        y_pool = _mm(win_sum / count - u, poolw_ref[...]) * pscale_ref[...]

        zext = jnp.concatenate([z_halo, cg_ref[rows, :] * z_ref[rows, :]], axis=0)
        conv = (convw_ref[2:3, :] * zext
                + convw_ref[1:2, :] * pltpu.roll(zext, 1, 0)
                + convw_ref[0:1, :] * pltpu.roll(zext, 2, 0))
        y_conv = bg_ref[rows, :] * conv[SUBLANES:, :]

        ys = [y.astype(BF16) for y in (ya_ref[rows, :], y_pool, yc_ref[rows, :], y_conv)]
        merged = []
        for j in range(d // MIX_W):
            cols = slice(j * MIX_W, (j + 1) * MIX_W)
            acc = None
            for gi in range(N_BRANCH):
                wgate_ref = (wgate01_ref, wgate23_ref)[gi // 2]
                gcols = slice((gi % 2) * d + j * MIX_W, (gi % 2) * d + (j + 1) * MIX_W)
                gate = jax.nn.sigmoid(jnp.dot(h, wgate_ref[:, gcols], preferred_element_type=F32))
                term = jnp.dot(ys[gi], wbr_ref[gi, :, cols], preferred_element_type=F32) * gate
                acc = term if acc is None else acc + term
            merged.append(acc.astype(BF16))
        o_ref[rows, :] = x + jnp.dot(jnp.concatenate(merged, axis=1), wout_ref[...],
                                     preferred_element_type=F32)


def _merge(x, p, y_s5, y_rwkv, norm_g, w_in, w_branch, w_out, layer, pool_bd, pool_scale, conv_w, *,
           seq, tm=1024, sub=512):
    t, d = x.shape
    tm = min(tm, seq)
    gate_block = 2 * d
    first_gate_block = 8 * MIX_W // gate_block

    def cur(col):
        return pl.BlockSpec((tm, MIX_W), lambda i: (i, col))

    def prev(col, rows):
        per_tile = tm // rows
        return pl.BlockSpec((rows, MIX_W), lambda i: (jnp.maximum(i * per_tile - 1, 0), col))

    in_specs = [
        pl.BlockSpec((tm, d), lambda i: (i, 0)),
        cur(1), prev(1, POOL_HALO), cur(5), prev(5, SUBLANES), cur(6), cur(7), prev(7, SUBLANES),
        pl.BlockSpec((tm, MIX_W), lambda i: (i, 0)),
        pl.BlockSpec((tm, MIX_W), lambda i: (i, 0)),
        _layer_spec(norm_g.shape, layer),
        _layer_spec(w_in.shape, layer, col_block=(gate_block, first_gate_block)),
        _layer_spec(w_in.shape, layer, col_block=(gate_block, first_gate_block + 1)),
        _layer_spec(w_branch.shape, layer), _layer_spec(w_out.shape, layer),
        _layer_spec(pool_bd.shape, layer), _layer_spec(pool_scale.shape, layer),
        _layer_spec(conv_w.shape, layer),
    ]
    return pl.pallas_call(
        functools.partial(_merge_body, seq=seq, sub=min(sub, tm)),
        grid=(t // tm,),
        in_specs=in_specs,
        out_specs=pl.BlockSpec((tm, d), lambda i: (i, 0)),
        out_shape=jax.ShapeDtypeStruct((t, d), F32),
        compiler_params=_params(1),
        name="merge",
    )(x, p, p, p, p, p, p, p, y_s5, y_rwkv, norm_g, w_in, w_in, w_branch, w_out,
      pool_bd, pool_scale, conv_w)


def kernel(x, ffn1_norm, ffn1_w_gate, ffn1_w_up, ffn1_w_down, mix_norm, w_in, s5_lambda_re, s5_lambda_im, s5_log_dt, s5_b_re, s5_b_im, s5_c_re, s5_c_im, s5_d, s5_w_glu, pool_w, pool_scale, rwkv_mu_rkv, rwkv_mu_wag, rwkv_w0, rwkv_w1, rwkv_w2, rwkv_a0, rwkv_a1, rwkv_a2, rwkv_g1, rwkv_g2, rwkv_k_k, rwkv_k_a, rwkv_r_k, rwkv_ln_w, rwkv_ln_b, conv_w, w_branch, w_out, ffn2_norm, ffn2_w_gate, ffn2_w_up, ffn2_w_down, final_norm):
    batch, seq, d = x.shape
    depth = w_in.shape[0]
    bf = lambda w: w.astype(BF16)
    ffn1 = (ffn1_w_gate, ffn1_w_up, ffn1_w_down)
    ffn2 = (ffn2_w_gate, ffn2_w_up, ffn2_w_down)
    w_in_b, w_branch_b, w_out_b = bf(w_in), bf(w_branch), bf(w_out)
    vec = lambda a: a.reshape(depth, 1, -1)
    ffn1_g, ffn2_g, mix_g = vec(ffn1_norm), vec(ffn2_norm), vec(mix_norm)
    proj_params = (rwkv_mu_wag, vec(rwkv_w0), bf(rwkv_w1), rwkv_w2, vec(rwkv_a0), bf(rwkv_a1),
                   rwkv_a2, bf(rwkv_g1), rwkv_g2)
    s5_params = _s5_params(s5_lambda_re, s5_lambda_im, s5_log_dt, s5_b_re, s5_b_im, s5_c_re, s5_c_im,
                           s5_d, s5_w_glu)
    rwkv_params = (rwkv_mu_rkv, vec(rwkv_k_k), vec(rwkv_k_a), vec(rwkv_r_k), vec(rwkv_ln_w),
                   vec(rwkv_ln_b))
    pool_groups = len(POOL_WINDOWS)
    pool_bd = jnp.einsum("lgcd,gk->lgckd", pool_w, jnp.eye(pool_groups, dtype=F32)).reshape(
        depth, MIX_W, MIX_W)
    pool_s = vec(pool_scale)
    xf = x.reshape(batch * seq, d)
    for l in range(depth):
        xf = _ffn(xf, ffn1_g, *ffn1, l)
        p, lora = _proj(xf, mix_g, w_in_b, l, *proj_params, seq=seq)
        y_s5 = _s5(p, s5_params, l, batch=batch, seq=seq)
        y_rwkv = _rwkv(p, lora, rwkv_params, l, batch=batch, seq=seq)
        xf = _merge(xf, p, y_s5, y_rwkv, mix_g, w_in_b, w_branch_b, w_out_b, l,
                    pool_bd, pool_s, conv_w, seq=seq)
        xf = _ffn(xf, ffn2_g, *ffn2, l, final_norm if l == depth - 1 else None)
    return xf.reshape(batch, seq, d)
```

```python
import functools
import math

import jax
import jax.numpy as jnp
from jax import lax
from jax.experimental import pallas as pl
from jax.experimental.pallas import tpu as pltpu

F32 = jnp.float32
BF16 = jnp.bfloat16

MIX_W = 256
N_BRANCH = 4
S5_GROUPS = 16
S5_STATE = 64
POOL_WINDOWS = (2, 4, 8, 16)
POOL_HALO = 16
RW_HEAD = 64
RW_CHUNK = 64
RW_GN_EPS = 64e-5
NORM_EPS = 1e-6
SUBLANES = 8
VMEM_LIMIT_BYTES = 56 * 1024 * 1024


def _params(n_parallel, n_arbitrary=0):
    return pltpu.CompilerParams(
        dimension_semantics=("parallel",) * n_parallel + ("arbitrary",) * n_arbitrary,
        vmem_limit_bytes=VMEM_LIMIT_BYTES)


def _const_spec(shape):
    zeros = (0,) * len(shape)
    return pl.BlockSpec(shape, lambda *_: zeros, pipeline_mode=pl.Buffered(1))


def _layer_spec(stacked_shape, layer, col_block=None):
    shape = tuple(stacked_shape[1:])
    tail = (0,) * (len(shape) - 1)
    if col_block is None:
        index = (layer,) + tail + (0,)
    else:
        width, col = col_block
        shape = shape[:-1] + (width,)
        index = (layer,) + tail + (col,)
    return pl.BlockSpec((None,) + shape, lambda *_: index, pipeline_mode=pl.Buffered(1))


def _rmsnorm(x, g):
    return x * lax.rsqrt(jnp.mean(x * x, axis=-1, keepdims=True) + NORM_EPS) * g


def _mm(a, b):
    return jnp.dot(a.astype(BF16), b.astype(BF16), preferred_element_type=F32)


def _mm_nt(a, b):
    return lax.dot_general(a.astype(BF16), b.astype(BF16), (((1,), (1,)), ((), ())),
                           preferred_element_type=F32)


def _mm_tn(a, b):
    return lax.dot_general(a.astype(BF16), b.astype(BF16), (((0,), (0,)), ((), ())),
                           preferred_element_type=F32)


def _shift_rows(cur, prev_row):
    rolled = pltpu.roll(cur, 1, 0)
    row = lax.broadcasted_iota(jnp.int32, cur.shape, 0)
    return jnp.where(row == 0, prev_row, rolled)


def _softplus(z):
    return jnp.maximum(z, 0.0) + jnp.log1p(jnp.exp(-jnp.abs(z)))


def _ffn_body(x_ref, g_ref, wg_ref, wu_ref, wd_ref, *rest, n_chunks, final):
    if final:
        fg_ref, o_ref, wg_s, wu_s, wd_s, h_s, acc_ref = rest
    else:
        o_ref, wg_s, wu_s, wd_s, h_s, acc_ref = rest
    s = pl.program_id(0)

    def chunk(h, j):
        gate = jnp.dot(h, wg_s[j], preferred_element_type=F32)
        up = jnp.dot(h, wu_s[j], preferred_element_type=F32)
        act = (gate * jax.nn.sigmoid(gate) * up).astype(BF16)
        return jnp.dot(act, wd_s[j], preferred_element_type=F32)

    def finish():
        y = x_ref[...] + 0.5 * acc_ref[...]
        if final:
            y = _rmsnorm(y, fg_ref[...])
        o_ref[...] = y

    @pl.when(s < n_chunks)
    def _():
        wg_s[s] = wg_ref[...].astype(BF16)
        wu_s[s] = wu_ref[...].astype(BF16)
        wd_s[s] = wd_ref[...].astype(BF16)

        @pl.when(s == 0)
        def _():
            h_s[...] = _rmsnorm(x_ref[...], g_ref[...]).astype(BF16)
            acc_ref[...] = jnp.zeros_like(acc_ref)

        acc_ref[...] += chunk(h_s[...], s)

        @pl.when(s == n_chunks - 1)
        def _():
            finish()

    @pl.when(s >= n_chunks)
    def _():
        h = _rmsnorm(x_ref[...], g_ref[...]).astype(BF16)
        for j in range(n_chunks):
            contrib = chunk(h, j)
            if j == 0:
                acc_ref[...] = contrib
            else:
                acc_ref[...] += contrib
        finish()


def _ffn(x, norm_g, w_gate, w_up, w_down, layer, final_g=None, *, tm=1024, f_chunk=256):
    t, d = x.shape
    d_ff = w_gate.shape[2]
    tm = min(tm, t)
    n_chunks = d_ff // f_chunk
    final = final_g is not None
    tile = lambda s: (jnp.maximum(s - (n_chunks - 1), 0), 0)
    in_specs = [
        pl.BlockSpec((tm, d), tile),
        _layer_spec(norm_g.shape, layer),
        pl.BlockSpec((None, d, f_chunk), lambda s: (layer, 0, jnp.minimum(s, n_chunks - 1))),
        pl.BlockSpec((None, d, f_chunk), lambda s: (layer, 0, jnp.minimum(s, n_chunks - 1))),
        pl.BlockSpec((None, f_chunk, d), lambda s: (layer, jnp.minimum(s, n_chunks - 1), 0)),
    ]
    args = [x, norm_g, w_gate, w_up, w_down]
    if final:
        in_specs.append(_const_spec((1, d)))
        args.append(final_g.reshape(1, d))
    return pl.pallas_call(
        functools.partial(_ffn_body, n_chunks=n_chunks, final=final),
        grid=(n_chunks - 1 + t // tm,),
        in_specs=in_specs,
        out_specs=pl.BlockSpec((tm, d), tile),
        out_shape=jax.ShapeDtypeStruct((t, d), F32),
        scratch_shapes=[pltpu.VMEM((n_chunks, d, f_chunk), BF16), pltpu.VMEM((n_chunks, d, f_chunk), BF16),
                        pltpu.VMEM((n_chunks, f_chunk, d), BF16), pltpu.VMEM((tm, d), BF16),
                        pltpu.VMEM((tm, d), F32)],
        compiler_params=_params(0, 1),
        name="ffn_final" if final else "ffn",
    )(*args)


def _proj_body(x_ref, xprev_ref, g_ref, win_ref, mu_ref, w0_ref, w1_ref, w2_ref, a0_ref, a1_ref,
               a2_ref, g1_ref, g2_ref, p_ref, lora_ref, *, seq):
    i = pl.program_id(0)
    tm = x_ref.shape[0]
    g = g_ref[...]
    h = _rmsnorm(x_ref[...], g)
    at_start = (i * tm) % seq == 0
    h_prev = _rmsnorm(xprev_ref[SUBLANES - 1:SUBLANES, :], g)
    h_prev = jnp.where(at_start, 0.0, h_prev)
    hx = _shift_rows(h, h_prev) - h
    hb = h.astype(BF16)

    def main_cols(lo, hi):
        for j in range(lo, hi):
            cols = slice(j * MIX_W, (j + 1) * MIX_W)
            p_ref[:, cols] = jnp.dot(hb, win_ref[:, cols], preferred_element_type=F32)

    main_cols(0, 3)
    xw = (h + hx * mu_ref[0:1, :]).astype(BF16)
    t_w = jnp.tanh(jnp.dot(xw, w1_ref[...], preferred_element_type=F32))
    main_cols(3, 5)
    xa = (h + hx * mu_ref[1:2, :]).astype(BF16)
    t_a = jnp.dot(xa, a1_ref[...], preferred_element_type=F32)
    main_cols(5, 7)
    xg = (h + hx * mu_ref[2:3, :]).astype(BF16)
    t_g = jax.nn.sigmoid(jnp.dot(xg, g1_ref[...], preferred_element_type=F32))
    main_cols(7, 8)
    w_log = -_softplus(-(w0_ref[...] + _mm(t_w, w2_ref[...]))) - 0.5
    lora_ref[:, 0:MIX_W] = -jnp.exp(w_log)
    lora_ref[:, MIX_W:2 * MIX_W] = jax.nn.sigmoid(a0_ref[...] + _mm(t_a, a2_ref[...]))
    lora_ref[:, 2 * MIX_W:3 * MIX_W] = _mm(t_g, g2_ref[...])


def _proj(x, norm_g, w_in, layer, mu_wag, w0, w1, w2, a0, a1, a2, g1, g2, *, seq, tm=1024):
    t, d = x.shape
    tm = min(tm, seq)
    n_small = 8 * MIX_W
    blocks_per_tile = tm // SUBLANES
    in_specs = [
        pl.BlockSpec((tm, d), lambda i: (i, 0)),
        pl.BlockSpec((SUBLANES, d), lambda i: (jnp.maximum(i * blocks_per_tile - 1, 0), 0)),
        _layer_spec(norm_g.shape, layer),
        _layer_spec(w_in.shape, layer, col_block=(n_small, 0)),
    ] + [_layer_spec(a.shape, layer) for a in (mu_wag, w0, w1, w2, a0, a1, a2, g1, g2)]
    return pl.pallas_call(
        functools.partial(_proj_body, seq=seq),
        grid=(t // tm,),
        in_specs=in_specs,
        out_specs=[pl.BlockSpec((tm, n_small), lambda i: (i, 0)),
                   pl.BlockSpec((tm, 3 * MIX_W), lambda i: (i, 0))],
        out_shape=[jax.ShapeDtypeStruct((t, n_small), F32),
                   jax.ShapeDtypeStruct((t, 3 * MIX_W), F32)],
        compiler_params=_params(1),
        name="proj",
    )(x, x, norm_g, w_in, mu_wag, w0, w1, w2, a0, a1, a2, g1, g2)


def _s5_body(u_ref, lre_ref, lim_ref, ldt_ref, bre_ref, bim_ref, cre_ref, cim_ref, d_ref, wglu_ref,
             o_ref, wre_s, wim_s, abar_s, xre_s, xim_s, hre_s, him_s, st_s, *, unroll):
    @pl.when(pl.program_id(0) == 0)
    def _():
        lr = lre_ref[...]
        li = lim_ref[...]
        dt = jnp.exp(ldt_ref[...])
        mag = jnp.exp(lr * dt)
        ar = mag * jnp.cos(li * dt)
        ai = mag * jnp.sin(li * dt)
        inv = 1.0 / (lr * lr + li * li)
        qr, qi = lr * inv, -li * inv
        coef_re = (ar - 1.0) * qr - ai * qi
        coef_im = (ar - 1.0) * qi + ai * qr
        wre_s[...] = (coef_re * bre_ref[...] - coef_im * bim_ref[...]).astype(BF16)
        wim_s[...] = (coef_re * bim_ref[...] + coef_im * bre_ref[...]).astype(BF16)
        abar_s[0:1, :] = ar
        abar_s[1:2, :] = ai
        st_s[...] = jnp.zeros_like(st_s)

    nb, tl = u_ref.shape[0], u_ref.shape[1]
    rows, n_state = nb * tl, wre_s.shape[1]
    u = u_ref[...].reshape(rows, MIX_W)
    ub = u.astype(BF16)
    blocks_per_row = tl // SUBLANES
    blocks = (nb * blocks_per_row, SUBLANES, n_state)
    xre_s[...] = jnp.dot(ub, wre_s[...], preferred_element_type=F32).reshape(blocks)
    xim_s[...] = jnp.dot(ub, wim_s[...], preferred_element_type=F32).reshape(blocks)
    ar = abar_s[0:1, :]
    ai = abar_s[1:2, :]

    def step(j, carry):
        carry = list(carry)
        for q in range(SUBLANES):
            for b in range(nb):
                blk = b * blocks_per_row + j
                hr, hi = carry[b]
                nr = ar * hr - ai * hi + xre_s[blk, q:q + 1, :]
                ni = ar * hi + ai * hr + xim_s[blk, q:q + 1, :]
                hre_s[blk, q:q + 1, :] = nr
                him_s[blk, q:q + 1, :] = ni
                carry[b] = (nr, ni)
        return tuple(carry)

    init = tuple((st_s[2 * b:2 * b + 1, :], st_s[2 * b + 1:2 * b + 2, :]) for b in range(nb))
    final = lax.fori_loop(0, blocks_per_row, step, init, unroll=unroll)
    for b in range(nb):
        st_s[2 * b:2 * b + 1, :] = final[b][0]
        st_s[2 * b + 1:2 * b + 2, :] = final[b][1]

    y = (_mm(hre_s[...].reshape(rows, n_state), cre_ref[...])
         - _mm(him_s[...].reshape(rows, n_state), cim_ref[...]) + d_ref[...] * u)
    gl = jax.nn.gelu(y, approximate=True)
    o_ref[...] = (gl * jax.nn.sigmoid(_mm(gl, wglu_ref[...]))).reshape(nb, tl, MIX_W)


def _s5_params(lam_re, lam_im, log_dt, b_re, b_im, c_re, c_im, d_skip, w_glu):
    depth = lam_re.shape[0]
    n_state = S5_GROUPS * S5_STATE
    eye = jnp.eye(S5_GROUPS, dtype=F32)
    row = lambda v: v.reshape(depth, 1, -1)
    b_bd = lambda b: jnp.einsum("lgph,gk->lghkp", b, eye).reshape(depth, MIX_W, n_state)
    c_bd = lambda c: jnp.einsum("lghp,gk->lgpkh", c, eye).reshape(depth, n_state, MIX_W)
    dt_rows = jnp.broadcast_to(log_dt[:, :, None], (depth, S5_GROUPS, S5_STATE))
    return (row(lam_re), row(lam_im), row(dt_rows), b_bd(b_re), b_bd(b_im),
            c_bd(c_re).astype(BF16), c_bd(c_im).astype(BF16), row(d_skip), w_glu.astype(BF16))


def _s5(p, params, layer, *, batch, seq, tl=512):
    n_state = S5_GROUPS * S5_STATE
    tl = min(tl, seq)
    in_specs = ([pl.BlockSpec((batch, tl, MIX_W), lambda i: (0, i, 0))]
                + [_layer_spec(a.shape, layer) for a in params])
    scan_buf = pltpu.VMEM((batch * tl // SUBLANES, SUBLANES, n_state), F32)
    out = pl.pallas_call(
        functools.partial(_s5_body, unroll=2),
        grid=(seq // tl,),
        in_specs=in_specs,
        out_specs=pl.BlockSpec((batch, tl, MIX_W), lambda i: (0, i, 0)),
        out_shape=jax.ShapeDtypeStruct((batch, seq, MIX_W), F32),
        scratch_shapes=[pltpu.VMEM((MIX_W, n_state), BF16), pltpu.VMEM((MIX_W, n_state), BF16),
                        pltpu.VMEM((2, n_state), F32)]
                       + [scan_buf] * 4 + [pltpu.VMEM((2 * batch, n_state), F32)],
        compiler_params=_params(0, 1),
        name="s5",
    )(p.reshape(batch, seq, p.shape[1]), *params)
    return out.reshape(batch * seq, MIX_W)


def _segment_ones(n, seg):
    r = lax.broadcasted_iota(jnp.int32, (n, n), 0) // seg
    c = lax.broadcasted_iota(jnp.int32, (n, n), 1) // seg
    return r == c


def _rwkv_body(rp_ref, kp_ref, vp_ref, rprev_ref, kprev_ref, vprev_ref, lora_ref, mu_ref, kk_ref,
               ka_ref, rk_ref, lnw_ref, lnb_ref, o_ref, st_s):
    nb, tl = rp_ref.shape[0], rp_ref.shape[1]
    rows = nb * tl
    c = RW_CHUNK
    first = pl.program_id(0) == 0

    @pl.when(first)
    def _():
        st_s[...] = jnp.zeros_like(st_s)

    row_id = lax.broadcasted_iota(jnp.int32, (rows, MIX_W), 0)

    def mixed(cur_ref, prev_ref, mu):
        cur = cur_ref[...].reshape(rows, MIX_W)
        shifted = pltpu.roll(cur, 1, 0)
        for b in range(nb):
            prev_row = jnp.where(first, 0.0, prev_ref[b, SUBLANES - 1:SUBLANES, :])
            shifted = jnp.where(row_id == b * tl, prev_row, shifted)
        return cur + (shifted - cur) * mu

    r = mixed(rp_ref, rprev_ref, mu_ref[0:1, :])
    k = mixed(kp_ref, kprev_ref, mu_ref[1:2, :])
    v = mixed(vp_ref, vprev_ref, mu_ref[2:3, :])
    lora = lora_ref[...].reshape(rows, 3 * MIX_W)
    logw = lora[:, 0:MIX_W]
    a = lora[:, MIX_W:2 * MIX_W]
    head_ones = _segment_ones(MIX_W, RW_HEAD).astype(BF16)
    kraw = k * kk_ref[...]
    kk = kraw / jnp.maximum(jnp.sqrt(_mm(kraw * kraw, head_ones)), 1e-12)
    kmod = k * (1.0 + (a - 1.0) * ka_ref[...])

    bdot = functools.partial(jnp.dot, preferred_element_type=F32)
    n_chunks = rows // c
    chunks_per_row = tl // c

    tri = (lax.broadcasted_iota(jnp.int32, (c, c), 1)
           <= lax.broadcasted_iota(jnp.int32, (c, c), 0)).astype(BF16)
    w_hi = logw.astype(BF16)
    rem = logw - w_hi.astype(F32)
    w_mid = rem.astype(BF16)
    w_lo = (rem - w_mid.astype(F32)).astype(BF16)
    cum = jnp.concatenate(
        [bdot(tri, w_hi[n * c:(n + 1) * c]) + bdot(tri, w_mid[n * c:(n + 1) * c])
         + bdot(tri, w_lo[n * c:(n + 1) * c]) for n in range(n_chunks)], axis=0)
    tot = jnp.concatenate(
        [jnp.broadcast_to(cum[(n + 1) * c - 1:(n + 1) * c, :], (c, MIX_W)) for n in range(n_chunks)],
        axis=0)
    inv = jnp.exp(-cum)
    to_end = jnp.exp(tot - cum)
    r_dec = (r * jnp.exp(cum)).astype(BF16)
    a_dec = (-kk * jnp.exp(cum - logw)).astype(BF16)
    b_inv = (kk * a * inv).astype(BF16)
    k_inv = (kmod * inv).astype(BF16)
    b_end = (kk * a * to_end).astype(BF16)
    k_end = (kmod * to_end).astype(BF16)
    v_b = v.astype(BF16)
    p_end = [jnp.exp(cum[(n + 1) * c - 1:(n + 1) * c, :]) for n in range(n_chunks)]

    pair_w = 2 * RW_HEAD
    n_pairs = MIX_W // pair_w
    row_i = lax.broadcasted_iota(jnp.int32, (c, pair_w), 0)
    lane_i = lax.broadcasted_iota(jnp.int32, (c, pair_w), 1)
    left = lane_i < RW_HEAD
    strict = (lane_i % RW_HEAD) < row_i
    incl = (lane_i % RW_HEAD) <= row_i
    eye_fam = ((lane_i % RW_HEAD) == row_i).astype(F32)
    pr = lax.broadcasted_iota(jnp.int32, (pair_w, pair_w), 0)
    pc = lax.broadcasted_iota(jnp.int32, (pair_w, pair_w), 1)
    same_head = (pr // RW_HEAD) == (pc // RW_HEAD)
    eye_pair = (pr == pc).astype(F32)

    def halves(x):
        z = jnp.zeros_like(x)
        return jnp.where(left, x, z), jnp.where(left, z, x)

    def blockdiag(x):
        return jnp.concatenate(halves(x), axis=0)

    probs = [(q, n) for q in range(n_pairs) for n in range(n_chunks)]
    n_probs = len(probs)

    def piece(x, q, n):
        return x[n * c:(n + 1) * c, q * pair_w:(q + 1) * pair_w]

    a_d = [piece(a_dec, q, n) for q, n in probs]
    r_d = [piece(r_dec, q, n) for q, n in probs]
    vv = [piece(v_b, q, n) for q, n in probs]
    vv_bd = [blockdiag(x) for x in vv]
    g = [_mm_nt(jnp.concatenate([a_d[i], r_d[i]], axis=0),
                jnp.concatenate(halves(piece(b_inv, q, n)) + halves(piece(k_inv, q, n)), axis=0))
         for i, (q, n) in enumerate(probs)]
    a_ab = [jnp.where(strict, x[0:c, 0:pair_w], 0.0) for x in g]
    a_ak = [jnp.where(strict, x[0:c, pair_w:2 * pair_w], 0.0).astype(BF16) for x in g]
    lhs_o = [jnp.concatenate([r_d[i], jnp.where(incl, x[c:2 * c, 0:pair_w], 0.0).astype(BF16),
                              jnp.where(incl, x[c:2 * c, pair_w:2 * pair_w], 0.0).astype(BF16)], axis=1)
             for i, x in enumerate(g)]
    tinv = [eye_fam + x for x in a_ab]
    pw = [x.astype(BF16) for x in a_ab]
    pw = [bdot(x, blockdiag(x)).astype(BF16) for x in pw]
    akv = [bdot(a_ak[i], vv_bd[i]).astype(BF16) for i in range(n_probs)]
    n_steps = int(math.log2(c)) - 1
    for s in range(n_steps):
        if s + 1 < n_steps:
            both = [bdot(jnp.concatenate([tinv[i].astype(BF16), pw[i]], axis=0), blockdiag(pw[i]))
                    for i in range(n_probs)]
            tinv = [tinv[i] + both[i][0:c] for i in range(n_probs)]
            pw = [x[c:2 * c].astype(BF16) for x in both]
        else:
            tinv = [tinv[i] + bdot(tinv[i].astype(BF16), blockdiag(pw[i])) for i in range(n_probs)]
    w1u0 = [bdot(tinv[i].astype(BF16), jnp.concatenate([blockdiag(a_d[i]), blockdiag(akv[i])], axis=1))
            for i in range(n_probs)]
    tn = [_mm_tn(piece(b_end, q, n), w1u0[i]) for i, (q, n) in enumerate(probs)]
    kv = [_mm_tn(piece(k_end, q, n), vv[i]) for i, (q, n) in enumerate(probs)]
    trans = [(jnp.where(same_head, tn[i][:, 0:pair_w], 0.0)
              + eye_pair * p_end[n][:, q * pair_w:(q + 1) * pair_w]).astype(BF16)
             for i, (q, n) in enumerate(probs)]
    add = [jnp.where(same_head, tn[i][:, pair_w:2 * pair_w] + kv[i], 0.0) for i in range(n_probs)]
    w1 = [x[:, 0:pair_w].astype(BF16) for x in w1u0]
    u0 = [x[:, pair_w:2 * pair_w] for x in w1u0]

    st_at = {}
    chains = [(q, b) for q in range(n_pairs) for b in range(nb)]
    st = {qb: st_s[qb[0] * nb + qb[1]] for qb in chains}
    for step in range(chunks_per_row):
        for q, b in chains:
            i = q * n_chunks + b * chunks_per_row + step
            st_b = st[q, b].astype(BF16)
            st_at[i] = st_b
            st[q, b] = bdot(trans[i], st_b) + add[i]
    for q, b in chains:
        st_s[q * nb + b] = st[q, b]
    u = [(bdot(w1[i], st_at[i]) + u0[i]).astype(BF16) for i in range(n_probs)]
    o_p = [bdot(lhs_o[i], jnp.concatenate([st_at[i], blockdiag(u[i]), vv_bd[i]], axis=0))
           for i in range(n_probs)]
    o = jnp.concatenate(
        [jnp.concatenate(o_p[q * n_chunks:(q + 1) * n_chunks], axis=0) for q in range(n_pairs)], axis=1)

    inv_n = 1.0 / RW_HEAD
    mean = _mm(o, head_ones) * inv_n
    dlt = o - mean
    var = _mm(dlt * dlt, head_ones) * inv_n
    o_n = dlt * lax.rsqrt(var + RW_GN_EPS) * lnw_ref[...] + lnb_ref[...]
    bonus = _mm(r * kmod * rk_ref[...], head_ones) * v
    o_ref[...] = ((o_n + bonus) * lora[:, 2 * MIX_W:3 * MIX_W]).reshape(nb, tl, MIX_W)


def _rwkv(p, lora, params, layer, *, batch, seq, tl=256):
    tl = min(tl, seq)
    blocks_per_tile = tl // SUBLANES
    p3 = p.reshape(batch, seq, p.shape[1])
    lora3 = lora.reshape(batch, seq, lora.shape[1])

    def cur(col):
        return pl.BlockSpec((batch, tl, MIX_W), lambda i: (0, i, col))

    def prev(col):
        return pl.BlockSpec((batch, SUBLANES, MIX_W),
                            lambda i: (0, jnp.maximum(i * blocks_per_tile - 1, 0), col))

    in_specs = [cur(2), cur(3), cur(4), prev(2), prev(3), prev(4),
                pl.BlockSpec((batch, tl, 3 * MIX_W), lambda i: (0, i, 0))]
    in_specs += [_layer_spec(a.shape, layer) for a in params]
    pair_w = 2 * RW_HEAD
    out = pl.pallas_call(
        _rwkv_body,
        grid=(seq // tl,),
        in_specs=in_specs,
        out_specs=pl.BlockSpec((batch, tl, MIX_W), lambda i: (0, i, 0)),
        out_shape=jax.ShapeDtypeStruct((batch, seq, MIX_W), F32),
        scratch_shapes=[pltpu.VMEM((batch * MIX_W // pair_w, pair_w, pair_w), F32)],
        compiler_params=_params(0, 1),
        name="rwkv",
    )(p3, p3, p3, p3, p3, p3, lora3, *params)
    return out.reshape(batch * seq, MIX_W)


def _merge_body(x_ref, ub_ref, ubprev_ref, z_ref, zprev_ref, bg_ref, cg_ref, cgprev_ref, ya_ref,
                yc_ref, g_ref, wgate01_ref, wgate23_ref, wbr_ref, wout_ref, poolw_ref, pscale_ref,
                convw_ref, o_ref, *, seq, sub):
    i = pl.program_id(0)
    tm, d = x_ref.shape
    at_start = (i * tm) % seq == 0
    lane = lax.broadcasted_iota(jnp.int32, (sub, MIX_W), 1)
    group = lane // (MIX_W // len(POOL_WINDOWS))
    row = lax.broadcasted_iota(jnp.int32, (sub, MIX_W), 0)

    for r0 in range(0, tm, sub):
        rows = slice(r0, r0 + sub)
        x = x_ref[rows, :]
        h = _rmsnorm(x, g_ref[...]).astype(BF16)

        u = ub_ref[rows, :]
        if r0 == 0:
            u_halo = jnp.where(at_start, 0.0, ubprev_ref[...])
            z_halo = jnp.where(at_start, 0.0, cgprev_ref[...] * zprev_ref[...])
        else:
            u_halo = ub_ref[r0 - POOL_HALO:r0, :]
            z_halo = cg_ref[r0 - SUBLANES:r0, :] * z_ref[r0 - SUBLANES:r0, :]
        ext = jnp.concatenate([u_halo, u], axis=0)
        sums = {1: ext}
        span = 1
        while span < POOL_WINDOWS[-1]:
            sums[2 * span] = sums[span] + pltpu.roll(sums[span], span, 0)
            span *= 2
        win_sum = sums[POOL_WINDOWS[-1]][POOL_HALO:, :]
        win = jnp.full((sub, MIX_W), float(POOL_WINDOWS[-1]), F32)
        for gi in range(len(POOL_WINDOWS) - 2, -1, -1):
            win_sum = jnp.where(group == gi, sums[POOL_WINDOWS[gi]][POOL_HALO:, :], win_sum)
            win = jnp.where(group == gi, float(POOL_WINDOWS[gi]), win)
        pos = (i * tm + r0) % seq + row
        count = jnp.minimum((pos + 1).astype(F32), win)
        y_pool = _mm(win_sum / count - u, poolw_ref[...]) * pscale_ref[...]

        zext = jnp.concatenate([z_halo, cg_ref[rows, :] * z_ref[rows, :]], axis=0)
        conv = (convw_ref[2:3, :] * zext
                + convw_ref[1:2, :] * pltpu.roll(zext, 1, 0)
                + convw_ref[0:1, :] * pltpu.roll(zext, 2, 0))
        y_conv = bg_ref[rows, :] * conv[SUBLANES:, :]

        ys = [y.astype(BF16) for y in (ya_ref[rows, :], y_pool, yc_ref[rows, :], y_conv)]
        merged = []
        for j in range(d // MIX_W):
            cols = slice(j * MIX_W, (j + 1) * MIX_W)
            acc = None
            for gi in range(N_BRANCH):
                wgate_ref = (wgate01_ref, wgate23_ref)[gi // 2]
                gcols = slice((gi % 2) * d + j * MIX_W, (gi % 2) * d + (j + 1) * MIX_W)
                gate = jax.nn.sigmoid(jnp.dot(h, wgate_ref[:, gcols], preferred_element_type=F32))
                term = jnp.dot(ys[gi], wbr_ref[gi, :, cols], preferred_element_type=F32) * gate
                acc = term if acc is None else acc + term
            merged.append(acc.astype(BF16))
        o_ref[rows, :] = x + jnp.dot(jnp.concatenate(merged, axis=1), wout_ref[...],
                                     preferred_element_type=F32)


def _merge(x, p, y_s5, y_rwkv, norm_g, w_in, w_branch, w_out, layer, pool_bd, pool_scale, conv_w, *,
           seq, tm=1024, sub=512):
    t, d = x.shape
    tm = min(tm, seq)
    gate_block = 2 * d
    first_gate_block = 8 * MIX_W // gate_block

    def cur(col):
        return pl.BlockSpec((tm, MIX_W), lambda i: (i, col))

    def prev(col, rows):
        per_tile = tm // rows
        return pl.BlockSpec((rows, MIX_W), lambda i: (jnp.maximum(i * per_tile - 1, 0), col))

    in_specs = [
        pl.BlockSpec((tm, d), lambda i: (i, 0)),
        cur(1), prev(1, POOL_HALO), cur(5), prev(5, SUBLANES), cur(6), cur(7), prev(7, SUBLANES),
        pl.BlockSpec((tm, MIX_W), lambda i: (i, 0)),
        pl.BlockSpec((tm, MIX_W), lambda i: (i, 0)),
        _layer_spec(norm_g.shape, layer),
        _layer_spec(w_in.shape, layer, col_block=(gate_block, first_gate_block)),
        _layer_spec(w_in.shape, layer, col_block=(gate_block, first_gate_block + 1)),
        _layer_spec(w_branch.shape, layer), _layer_spec(w_out.shape, layer),
        _layer_spec(pool_bd.shape, layer), _layer_spec(pool_scale.shape, layer),
        _layer_spec(conv_w.shape, layer),
    ]
    return pl.pallas_call(
        functools.partial(_merge_body, seq=seq, sub=min(sub, tm)),
        grid=(t // tm,),
        in_specs=in_specs,
        out_specs=pl.BlockSpec((tm, d), lambda i: (i, 0)),
        out_shape=jax.ShapeDtypeStruct((t, d), F32),
        compiler_params=_params(1),
        name="merge",
    )(x, p, p, p, p, p, p, p, y_s5, y_rwkv, norm_g, w_in, w_in, w_branch, w_out,
      pool_bd, pool_scale, conv_w)


def kernel(x, ffn1_norm, ffn1_w_gate, ffn1_w_up, ffn1_w_down, mix_norm, w_in, s5_lambda_re, s5_lambda_im, s5_log_dt, s5_b_re, s5_b_im, s5_c_re, s5_c_im, s5_d, s5_w_glu, pool_w, pool_scale, rwkv_mu_rkv, rwkv_mu_wag, rwkv_w0, rwkv_w1, rwkv_w2, rwkv_a0, rwkv_a1, rwkv_a2, rwkv_g1, rwkv_g2, rwkv_k_k, rwkv_k_a, rwkv_r_k, rwkv_ln_w, rwkv_ln_b, conv_w, w_branch, w_out, ffn2_norm, ffn2_w_gate, ffn2_w_up, ffn2_w_down, final_norm):
    batch, seq, d = x.shape
    depth = w_in.shape[0]
    bf = lambda w: w.astype(BF16)
    ffn1 = (ffn1_w_gate, ffn1_w_up, ffn1_w_down)
    ffn2 = (ffn2_w_gate, ffn2_w_up, ffn2_w_down)
    w_in_b, w_branch_b, w_out_b = bf(w_in), bf(w_branch), bf(w_out)
    vec = lambda a: a.reshape(depth, 1, -1)
    ffn1_g, ffn2_g, mix_g = vec(ffn1_norm), vec(ffn2_norm), vec(mix_norm)
    proj_params = (rwkv_mu_wag, vec(rwkv_w0), bf(rwkv_w1), rwkv_w2, vec(rwkv_a0), bf(rwkv_a1),
                   rwkv_a2, bf(rwkv_g1), rwkv_g2)
    s5_params = _s5_params(s5_lambda_re, s5_lambda_im, s5_log_dt, s5_b_re, s5_b_im, s5_c_re, s5_c_im,
                           s5_d, s5_w_glu)
    rwkv_params = (rwkv_mu_rkv, vec(rwkv_k_k), vec(rwkv_k_a), vec(rwkv_r_k), vec(rwkv_ln_w),
                   vec(rwkv_ln_b))
    pool_groups = len(POOL_WINDOWS)
    pool_bd = jnp.einsum("lgcd,gk->lgckd", pool_w, jnp.eye(pool_groups, dtype=F32)).reshape(
        depth, MIX_W, MIX_W)
    pool_s = vec(pool_scale)
    xf = x.reshape(batch * seq, d)
    for l in range(depth):
        xf = _ffn(xf, ffn1_g, *ffn1, l)
        p, lora = _proj(xf, mix_g, w_in_b, l, *proj_params, seq=seq)
        y_s5 = _s5(p, s5_params, l, batch=batch, seq=seq)
        y_rwkv = _rwkv(p, lora, rwkv_params, l, batch=batch, seq=seq)
        xf = _merge(xf, p, y_s5, y_rwkv, mix_g, w_in_b, w_branch_b, w_out_b, l,
                    pool_bd, pool_s, conv_w, seq=seq)
        xf = _ffn(xf, ffn2_g, *ffn2, l, final_norm if l == depth - 1 else None)
    return xf.reshape(batch, seq, d)
```

```python
import functools
import math

import jax
import jax.numpy as jnp
from jax import lax
from jax.experimental import pallas as pl
from jax.experimental.pallas import tpu as pltpu

F32 = jnp.float32
BF16 = jnp.bfloat16

MIX_W = 256
N_BRANCH = 4
S5_GROUPS = 16
S5_STATE = 64
POOL_WINDOWS = (2, 4, 8, 16)
POOL_HALO = 16
RW_HEAD = 64
RW_CHUNK = 64
RW_GN_EPS = 64e-5
NORM_EPS = 1e-6
SUBLANES = 8
VMEM_LIMIT_BYTES = 56 * 1024 * 1024


def _params(n_parallel, n_arbitrary=0):
    return pltpu.CompilerParams(
        dimension_semantics=("parallel",) * n_parallel + ("arbitrary",) * n_arbitrary,
        vmem_limit_bytes=VMEM_LIMIT_BYTES)


def _const_spec(shape):
    zeros = (0,) * len(shape)
    return pl.BlockSpec(shape, lambda *_: zeros, pipeline_mode=pl.Buffered(1))


def _layer_spec(stacked_shape, layer, col_block=None):
    shape = tuple(stacked_shape[1:])
    tail = (0,) * (len(shape) - 1)
    if col_block is None:
        index = (layer,) + tail + (0,)
    else:
        width, col = col_block
        shape = shape[:-1] + (width,)
        index = (layer,) + tail + (col,)
    return pl.BlockSpec((None,) + shape, lambda *_: index, pipeline_mode=pl.Buffered(1))


def _rmsnorm(x, g):
    return x * lax.rsqrt(jnp.mean(x * x, axis=-1, keepdims=True) + NORM_EPS) * g


def _mm(a, b):
    return jnp.dot(a.astype(BF16), b.astype(BF16), preferred_element_type=F32)


def _mm_nt(a, b):
    return lax.dot_general(a.astype(BF16), b.astype(BF16), (((1,), (1,)), ((), ())),
                           preferred_element_type=F32)


def _mm_tn(a, b):
    return lax.dot_general(a.astype(BF16), b.astype(BF16), (((0,), (0,)), ((), ())),
                           preferred_element_type=F32)


def _shift_rows(cur, prev_row):
    rolled = pltpu.roll(cur, 1, 0)
    row = lax.broadcasted_iota(jnp.int32, cur.shape, 0)
    return jnp.where(row == 0, prev_row, rolled)


def _softplus(z):
    return jnp.maximum(z, 0.0) + jnp.log(1.0 + jnp.exp(-jnp.abs(z)))


def _ffn_body(x_ref, g_ref, wg_ref, wu_ref, wd_ref, *rest, n_chunks, final):
    if final:
        fg_ref, o_ref, wg_s, wu_s, wd_s, h_s, acc_ref = rest
    else:
        o_ref, wg_s, wu_s, wd_s, h_s, acc_ref = rest
    s = pl.program_id(0)

    def chunk(h, j):
        gate = jnp.dot(h, wg_s[j], preferred_element_type=F32)
        up = jnp.dot(h, wu_s[j], preferred_element_type=F32)
        act = (gate * jax.nn.sigmoid(gate) * up).astype(BF16)
        return jnp.dot(act, wd_s[j], preferred_element_type=F32)

    def finish():
        y = x_ref[...] + 0.5 * acc_ref[...]
        if final:
            y = _rmsnorm(y, fg_ref[...])
        o_ref[...] = y

    @pl.when(s < n_chunks)
    def _():
        wg_s[s] = wg_ref[...].astype(BF16)
        wu_s[s] = wu_ref[...].astype(BF16)
        wd_s[s] = wd_ref[...].astype(BF16)

        @pl.when(s == 0)
        def _():
            h_s[...] = _rmsnorm(x_ref[...], g_ref[...]).astype(BF16)
            acc_ref[...] = jnp.zeros_like(acc_ref)

        acc_ref[...] += chunk(h_s[...], s)

        @pl.when(s == n_chunks - 1)
        def _():
            finish()

    @pl.when(s >= n_chunks)
    def _():
        h = _rmsnorm(x_ref[...], g_ref[...]).astype(BF16)
        for j in range(n_chunks):
            contrib = chunk(h, j)
            if j == 0:
                acc_ref[...] = contrib
            else:
                acc_ref[...] += contrib
        finish()


def _ffn(x, norm_g, w_gate, w_up, w_down, layer, final_g=None, *, tm=1024, f_chunk=256):
    t, d = x.shape
    d_ff = w_gate.shape[2]
    tm = min(tm, t)
    n_chunks = d_ff // f_chunk
    final = final_g is not None
    tile = lambda s: (jnp.maximum(s - (n_chunks - 1), 0), 0)
    in_specs = [
        pl.BlockSpec((tm, d), tile),
        _layer_spec(norm_g.shape, layer),
        pl.BlockSpec((None, d, f_chunk), lambda s: (layer, 0, jnp.minimum(s, n_chunks - 1))),
        pl.BlockSpec((None, d, f_chunk), lambda s: (layer, 0, jnp.minimum(s, n_chunks - 1))),
        pl.BlockSpec((None, f_chunk, d), lambda s: (layer, jnp.minimum(s, n_chunks - 1), 0)),
    ]
    args = [x, norm_g, w_gate, w_up, w_down]
    if final:
        in_specs.append(_const_spec((1, d)))
        args.append(final_g.reshape(1, d))
    return pl.pallas_call(
        functools.partial(_ffn_body, n_chunks=n_chunks, final=final),
        grid=(n_chunks - 1 + t // tm,),
        in_specs=in_specs,
        out_specs=pl.BlockSpec((tm, d), tile),
        out_shape=jax.ShapeDtypeStruct((t, d), F32),
        scratch_shapes=[pltpu.VMEM((n_chunks, d, f_chunk), BF16), pltpu.VMEM((n_chunks, d, f_chunk), BF16),
                        pltpu.VMEM((n_chunks, f_chunk, d), BF16), pltpu.VMEM((tm, d), BF16),
                        pltpu.VMEM((tm, d), F32)],
        compiler_params=_params(0, 1),
        name="ffn_final" if final else "ffn",
    )(*args)


def _proj_body(x_ref, xprev_ref, g_ref, win_ref, mumat_ref, wcat_ref, w0_ref, w2_ref, a0_ref, a2_ref,
               g2_ref, p_ref, lora_ref, wcat_s, wfold_s, *, seq, ranks):
    i = pl.program_id(0)
    tm = x_ref.shape[0]

    @pl.when(i == 0)
    def _():
        wcat = wcat_ref[...]
        wcat_s[...] = wcat.astype(BF16)
        wfold_s[...] = (mumat_ref[...] * wcat).astype(BF16)

    g = g_ref[...]
    h = _rmsnorm(x_ref[...], g)
    at_start = (i * tm) % seq == 0
    h_prev = _rmsnorm(xprev_ref[SUBLANES - 1:SUBLANES, :], g)
    h_prev = jnp.where(at_start, 0.0, h_prev)
    hb = h.astype(BF16)
    hxb = (_shift_rows(h, h_prev) - h).astype(BF16)

    def main_cols(lo, hi):
        for j in range(lo, hi):
            cols = slice(j * MIX_W, (j + 1) * MIX_W)
            p_ref[:, cols] = jnp.dot(hb, win_ref[:, cols], preferred_element_type=F32)

    main_cols(0, 4)
    t = (jnp.dot(hb, wcat_s[...], preferred_element_type=F32)
         + jnp.dot(hxb, wfold_s[...], preferred_element_type=F32))
    r_w, r_a, r_g = ranks
    t_w = jnp.tanh(t[:, 0:r_w])
    t_a = t[:, r_w:r_w + r_a]
    t_g = jax.nn.sigmoid(t[:, r_w + r_a:r_w + r_a + r_g])
    main_cols(4, 8)
    w_log = -_softplus(-(w0_ref[...] + _mm(t_w, w2_ref[...]))) - 0.5
    lora_ref[:, 0:MIX_W] = -jnp.exp(w_log)
    lora_ref[:, MIX_W:2 * MIX_W] = jax.nn.sigmoid(a0_ref[...] + _mm(t_a, a2_ref[...]))
    lora_ref[:, 2 * MIX_W:3 * MIX_W] = _mm(t_g, g2_ref[...])


def _proj(x, norm_g, w_in, layer, mu_mat, w_cat, w0, w2, a0, a2, g2, *, seq, tm=1024):
    t, d = x.shape
    tm = min(tm, seq)
    n_small = 8 * MIX_W
    blocks_per_tile = tm // SUBLANES
    ranks = (w2.shape[1], a2.shape[1], g2.shape[1])
    in_specs = [
        pl.BlockSpec((tm, d), lambda i: (i, 0)),
        pl.BlockSpec((SUBLANES, d), lambda i: (jnp.maximum(i * blocks_per_tile - 1, 0), 0)),
        _layer_spec(norm_g.shape, layer),
        _layer_spec(w_in.shape, layer, col_block=(n_small, 0)),
    ] + [_layer_spec(a.shape, layer) for a in (mu_mat, w_cat, w0, w2, a0, a2, g2)]
    return pl.pallas_call(
        functools.partial(_proj_body, seq=seq, ranks=ranks),
        grid=(t // tm,),
        in_specs=in_specs,
        out_specs=[pl.BlockSpec((tm, n_small), lambda i: (i, 0)),
                   pl.BlockSpec((tm, 3 * MIX_W), lambda i: (i, 0))],
        out_shape=[jax.ShapeDtypeStruct((t, n_small), F32),
                   jax.ShapeDtypeStruct((t, 3 * MIX_W), F32)],
        scratch_shapes=[pltpu.VMEM((d, sum(ranks)), BF16), pltpu.VMEM((d, sum(ranks)), BF16)],
        compiler_params=_params(0, 1),
        name="proj",
    )(x, x, norm_g, w_in, mu_mat, w_cat, w0, w2, a0, a2, g2)


def _s5_body(u_ref, lre_ref, lim_ref, ldt_ref, bre_ref, bim_ref, cre_ref, cim_ref, d_ref, wglu_ref,
             o_ref, wre_s, wim_s, abar_s, xre_s, xim_s, hre_s, him_s, st_s, *, unroll):
    @pl.when(pl.program_id(0) == 0)
    def _():
        lr = lre_ref[...]
        li = lim_ref[...]
        dt = jnp.exp(ldt_ref[...])
        mag = jnp.exp(lr * dt)
        ar = mag * jnp.cos(li * dt)
        ai = mag * jnp.sin(li * dt)
        inv = 1.0 / (lr * lr + li * li)
        qr, qi = lr * inv, -li * inv
        coef_re = (ar - 1.0) * qr - ai * qi
        coef_im = (ar - 1.0) * qi + ai * qr
        wre_s[...] = (coef_re * bre_ref[...] - coef_im * bim_ref[...]).astype(BF16)
        wim_s[...] = (coef_re * bim_ref[...] + coef_im * bre_ref[...]).astype(BF16)
        abar_s[0:1, :] = ar
        abar_s[1:2, :] = ai
        st_s[...] = jnp.zeros_like(st_s)

    nb, tl = u_ref.shape[0], u_ref.shape[1]
    rows, n_state = nb * tl, wre_s.shape[1]
    u = u_ref[...].reshape(rows, MIX_W)
    ub = u.astype(BF16)
    blocks_per_row = tl // SUBLANES
    blocks = (nb * blocks_per_row, SUBLANES, n_state)
    xre_s[...] = jnp.dot(ub, wre_s[...], preferred_element_type=F32).reshape(blocks)
    xim_s[...] = jnp.dot(ub, wim_s[...], preferred_element_type=F32).reshape(blocks)
    ar = abar_s[0:1, :]
    ai = abar_s[1:2, :]

    def step(j, carry):
        carry = list(carry)
        for q in range(SUBLANES):
            for b in range(nb):
                blk = b * blocks_per_row + j
                hr, hi = carry[b]
                nr = ar * hr - ai * hi + xre_s[blk, q:q + 1, :]
                ni = ar * hi + ai * hr + xim_s[blk, q:q + 1, :]
                hre_s[blk, q:q + 1, :] = nr
                him_s[blk, q:q + 1, :] = ni
                carry[b] = (nr, ni)
        return tuple(carry)

    init = tuple((st_s[2 * b:2 * b + 1, :], st_s[2 * b + 1:2 * b + 2, :]) for b in range(nb))
    final = lax.fori_loop(0, blocks_per_row, step, init, unroll=unroll)
    for b in range(nb):
        st_s[2 * b:2 * b + 1, :] = final[b][0]
        st_s[2 * b + 1:2 * b + 2, :] = final[b][1]

    y = (_mm(hre_s[...].reshape(rows, n_state), cre_ref[...])
         - _mm(him_s[...].reshape(rows, n_state), cim_ref[...]) + d_ref[...] * u)
    gl = jax.nn.gelu(y, approximate=True)
    o_ref[...] = (gl * jax.nn.sigmoid(_mm(gl, wglu_ref[...]))).reshape(nb, tl, MIX_W)


def _s5_params(lam_re, lam_im, log_dt, b_re, b_im, c_re, c_im, d_skip, w_glu):
    depth = lam_re.shape[0]
    n_state = S5_GROUPS * S5_STATE
    eye = jnp.eye(S5_GROUPS, dtype=F32)
    row = lambda v: v.reshape(depth, 1, -1)
    b_bd = lambda b: jnp.einsum("lgph,gk->lghkp", b, eye).reshape(depth, MIX_W, n_state)
    c_bd = lambda c: jnp.einsum("lghp,gk->lgpkh", c, eye).reshape(depth, n_state, MIX_W)
    dt_rows = jnp.broadcast_to(log_dt[:, :, None], (depth, S5_GROUPS, S5_STATE))
    return (row(lam_re), row(lam_im), row(dt_rows), b_bd(b_re), b_bd(b_im),
            c_bd(c_re).astype(BF16), c_bd(c_im).astype(BF16), row(d_skip), w_glu.astype(BF16))


def _s5(p, params, layer, *, batch, seq, tl=512):
    n_state = S5_GROUPS * S5_STATE
    tl = min(tl, seq)
    in_specs = ([pl.BlockSpec((batch, tl, MIX_W), lambda i: (0, i, 0))]
                + [_layer_spec(a.shape, layer) for a in params])
    scan_buf = pltpu.VMEM((batch * tl // SUBLANES, SUBLANES, n_state), F32)
    out = pl.pallas_call(
        functools.partial(_s5_body, unroll=2),
        grid=(seq // tl,),
        in_specs=in_specs,
        out_specs=pl.BlockSpec((batch, tl, MIX_W), lambda i: (0, i, 0)),
        out_shape=jax.ShapeDtypeStruct((batch, seq, MIX_W), F32),
        scratch_shapes=[pltpu.VMEM((MIX_W, n_state), BF16), pltpu.VMEM((MIX_W, n_state), BF16),
                        pltpu.VMEM((2, n_state), F32)]
                       + [scan_buf] * 4 + [pltpu.VMEM((2 * batch, n_state), F32)],
        compiler_params=_params(0, 1),
        name="s5",
    )(p.reshape(batch, seq, p.shape[1]), *params)
    return out.reshape(batch * seq, MIX_W)


def _segment_ones(n, seg):
    r = lax.broadcasted_iota(jnp.int32, (n, n), 0) // seg
    c = lax.broadcasted_iota(jnp.int32, (n, n), 1) // seg
    return r == c


def _rwkv_body(rp_ref, kp_ref, vp_ref, rprev_ref, kprev_ref, vprev_ref, lora_ref, mu_ref, kk_ref,
               ka_ref, rk_ref, lnw_ref, lnb_ref, o_ref, st_s):
    nb, tl = rp_ref.shape[0], rp_ref.shape[1]
    rows = nb * tl
    c = RW_CHUNK
    first = pl.program_id(0) == 0

    @pl.when(first)
    def _():
        st_s[...] = jnp.zeros_like(st_s)

    first_row = lax.broadcasted_iota(jnp.int32, (SUBLANES, MIX_W), 0) == 0

    def mixed(cur_ref, prev_ref, mu):
        cur = cur_ref[...].reshape(rows, MIX_W)
        shifted = pltpu.roll(cur, 1, 0)
        pieces = []
        for b in range(nb):
            prev_row = jnp.where(first, 0.0, prev_ref[b, SUBLANES - 1:SUBLANES, :])
            head = slice(b * tl, b * tl + SUBLANES)
            pieces += [jnp.where(first_row, prev_row, shifted[head]), shifted[b * tl + SUBLANES:(b + 1) * tl]]
        shifted = jnp.concatenate(pieces, axis=0)
        return cur + (shifted - cur) * mu

    r = mixed(rp_ref, rprev_ref, mu_ref[0:1, :])
    k = mixed(kp_ref, kprev_ref, mu_ref[1:2, :])
    v = mixed(vp_ref, vprev_ref, mu_ref[2:3, :])
    lora = lora_ref[...].reshape(rows, 3 * MIX_W)
    logw = lora[:, 0:MIX_W]
    a = lora[:, MIX_W:2 * MIX_W]
    head_ones = _segment_ones(MIX_W, RW_HEAD).astype(BF16)
    kraw = k * kk_ref[...]
    kk = kraw * lax.rsqrt(jnp.maximum(_mm(kraw * kraw, head_ones), 1e-24))
    kmod = k * (1.0 + (a - 1.0) * ka_ref[...])

    bdot = functools.partial(jnp.dot, preferred_element_type=F32)
    n_chunks = rows // c
    chunks_per_row = tl // c

    tri = (lax.broadcasted_iota(jnp.int32, (c, c), 1)
           <= lax.broadcasted_iota(jnp.int32, (c, c), 0)).astype(BF16)
    w_hi = logw.astype(BF16)
    rem = logw - w_hi.astype(F32)
    w_mid = rem.astype(BF16)
    w_lo = (rem - w_mid.astype(F32)).astype(BF16)
    cum = jnp.concatenate(
        [bdot(tri, w_hi[n * c:(n + 1) * c]) + bdot(tri, w_mid[n * c:(n + 1) * c])
         + bdot(tri, w_lo[n * c:(n + 1) * c]) for n in range(n_chunks)], axis=0)
    tot = jnp.concatenate(
        [jnp.broadcast_to(cum[(n + 1) * c - 1:(n + 1) * c, :], (c, MIX_W)) for n in range(n_chunks)],
        axis=0)
    inv = jnp.exp(-cum)
    to_end = jnp.exp(tot - cum)
    r_dec = (r * jnp.exp(cum)).astype(BF16)
    a_dec = (-kk * jnp.exp(cum - logw)).astype(BF16)
    b_inv = (kk * a * inv).astype(BF16)
    k_inv = (kmod * inv).astype(BF16)
    b_end = (kk * a * to_end).astype(BF16)
    k_end = (kmod * to_end).astype(BF16)
    v_b = v.astype(BF16)
    p_end = [jnp.exp(cum[(n + 1) * c - 1:(n + 1) * c, :]) for n in range(n_chunks)]

    pair_w = 2 * RW_HEAD
    n_pairs = MIX_W // pair_w
    row_i = lax.broadcasted_iota(jnp.int32, (c, pair_w), 0)
    lane_i = lax.broadcasted_iota(jnp.int32, (c, pair_w), 1)
    left = lane_i < RW_HEAD
    strict = (lane_i % RW_HEAD) < row_i
    incl = (lane_i % RW_HEAD) <= row_i
    eye_fam = ((lane_i % RW_HEAD) == row_i).astype(F32)
    pr = lax.broadcasted_iota(jnp.int32, (pair_w, pair_w), 0)
    pc = lax.broadcasted_iota(jnp.int32, (pair_w, pair_w), 1)
    same_head = (pr // RW_HEAD) == (pc // RW_HEAD)
    eye_pair = (pr == pc).astype(F32)

    def halves(x):
        z = jnp.zeros_like(x)
        return jnp.where(left, x, z), jnp.where(left, z, x)

    def blockdiag(x):
        return jnp.concatenate(halves(x), axis=0)

    probs = [(q, n) for q in range(n_pairs) for n in range(n_chunks)]
    n_probs = len(probs)

    def piece(x, q, n):
        return x[n * c:(n + 1) * c, q * pair_w:(q + 1) * pair_w]

    a_d = [piece(a_dec, q, n) for q, n in probs]
    r_d = [piece(r_dec, q, n) for q, n in probs]
    vv = [piece(v_b, q, n) for q, n in probs]
    vv_bd = [blockdiag(x) for x in vv]
    g = [_mm_nt(jnp.concatenate([a_d[i], r_d[i]], axis=0),
                jnp.concatenate(halves(piece(b_inv, q, n)) + halves(piece(k_inv, q, n)), axis=0))
         for i, (q, n) in enumerate(probs)]
    a_ab = [jnp.where(strict, x[0:c, 0:pair_w], 0.0) for x in g]
    a_ak = [jnp.where(strict, x[0:c, pair_w:2 * pair_w], 0.0).astype(BF16) for x in g]
    lhs_o = [jnp.concatenate([r_d[i], jnp.where(incl, x[c:2 * c, 0:pair_w], 0.0).astype(BF16),
                              jnp.where(incl, x[c:2 * c, pair_w:2 * pair_w], 0.0).astype(BF16)], axis=1)
             for i, x in enumerate(g)]
    tinv = [eye_fam + x for x in a_ab]
    pw = [x.astype(BF16) for x in a_ab]
    pw = [bdot(x, blockdiag(x)).astype(BF16) for x in pw]
    akv = [bdot(a_ak[i], vv_bd[i]).astype(BF16) for i in range(n_probs)]
    n_steps = int(math.log2(c)) - 1
    for s in range(n_steps):
        if s + 1 < n_steps:
            both = [bdot(jnp.concatenate([tinv[i].astype(BF16), pw[i]], axis=0), blockdiag(pw[i]))
                    for i in range(n_probs)]
            tinv = [tinv[i] + both[i][0:c] for i in range(n_probs)]
            pw = [x[c:2 * c].astype(BF16) for x in both]
        else:
            tinv = [tinv[i] + bdot(tinv[i].astype(BF16), blockdiag(pw[i])) for i in range(n_probs)]
    w1u0 = [bdot(tinv[i].astype(BF16), jnp.concatenate([blockdiag(a_d[i]), blockdiag(akv[i])], axis=1))
            for i in range(n_probs)]
    tn = [_mm_tn(piece(b_end, q, n), w1u0[i]) for i, (q, n) in enumerate(probs)]
    kv = [_mm_tn(piece(k_end, q, n), vv[i]) for i, (q, n) in enumerate(probs)]
    trans = [(jnp.where(same_head, tn[i][:, 0:pair_w], 0.0)
              + eye_pair * p_end[n][:, q * pair_w:(q + 1) * pair_w]).astype(BF16)
             for i, (q, n) in enumerate(probs)]
    add = [jnp.where(same_head, tn[i][:, pair_w:2 * pair_w] + kv[i], 0.0) for i in range(n_probs)]
    w1 = [x[:, 0:pair_w].astype(BF16) for x in w1u0]
    u0 = [x[:, pair_w:2 * pair_w] for x in w1u0]

    st_at = {}
    chains = [(q, b) for q in range(n_pairs) for b in range(nb)]
    st = {qb: st_s[qb[0] * nb + qb[1]] for qb in chains}
    for step in range(chunks_per_row):
        for q, b in chains:
            i = q * n_chunks + b * chunks_per_row + step
            st_b = st[q, b].astype(BF16)
            st_at[i] = st_b
            st[q, b] = bdot(trans[i], st_b) + add[i]
    for q, b in chains:
        st_s[q * nb + b] = st[q, b]
    u = [(bdot(w1[i], st_at[i]) + u0[i]).astype(BF16) for i in range(n_probs)]
    o_p = [bdot(lhs_o[i], jnp.concatenate([st_at[i], blockdiag(u[i]), vv_bd[i]], axis=0))
           for i in range(n_probs)]
    o = jnp.concatenate(
        [jnp.concatenate(o_p[q * n_chunks:(q + 1) * n_chunks], axis=0) for q in range(n_pairs)], axis=1)

    inv_n = 1.0 / RW_HEAD
    mean = _mm(o, head_ones) * inv_n
    dlt = o - mean
    var = _mm(dlt * dlt, head_ones) * inv_n
    o_n = dlt * lax.rsqrt(var + RW_GN_EPS) * lnw_ref[...] + lnb_ref[...]
    bonus = _mm(r * kmod * rk_ref[...], head_ones) * v
    o_ref[...] = ((o_n + bonus) * lora[:, 2 * MIX_W:3 * MIX_W]).reshape(nb, tl, MIX_W)


def _rwkv(p, lora, params, layer, *, batch, seq, tl=256):
    tl = min(tl, seq)
    blocks_per_tile = tl // SUBLANES
    p3 = p.reshape(batch, seq, p.shape[1])
    lora3 = lora.reshape(batch, seq, lora.shape[1])

    def cur(col):
        return pl.BlockSpec((batch, tl, MIX_W), lambda i: (0, i, col))

    def prev(col):
        return pl.BlockSpec((batch, SUBLANES, MIX_W),
                            lambda i: (0, jnp.maximum(i * blocks_per_tile - 1, 0), col))

    in_specs = [cur(2), cur(3), cur(4), prev(2), prev(3), prev(4),
                pl.BlockSpec((batch, tl, 3 * MIX_W), lambda i: (0, i, 0))]
    in_specs += [_layer_spec(a.shape, layer) for a in params]
    pair_w = 2 * RW_HEAD
    out = pl.pallas_call(
        _rwkv_body,
        grid=(seq // tl,),
        in_specs=in_specs,
        out_specs=pl.BlockSpec((batch, tl, MIX_W), lambda i: (0, i, 0)),
        out_shape=jax.ShapeDtypeStruct((batch, seq, MIX_W), F32),
        scratch_shapes=[pltpu.VMEM((batch * MIX_W // pair_w, pair_w, pair_w), F32)],
        compiler_params=_params(0, 1),
        name="rwkv",
    )(p3, p3, p3, p3, p3, p3, lora3, *params)
    return out.reshape(batch * seq, MIX_W)


def _merge_body(x_ref, ub_ref, ubprev_ref, z_ref, zprev_ref, bg_ref, cg_ref, cgprev_ref, ya_ref,
                yc_ref, g_ref, wgate01_ref, wgate23_ref, wbr_ref, wout_ref, poolw_ref, pscale_ref,
                convw_ref, o_ref, *, seq, sub):
    i = pl.program_id(0)
    tm, d = x_ref.shape
    at_start = (i * tm) % seq == 0
    lane = lax.broadcasted_iota(jnp.int32, (sub, MIX_W), 1)
    group = lane // (MIX_W // len(POOL_WINDOWS))
    row = lax.broadcasted_iota(jnp.int32, (sub, MIX_W), 0)

    for r0 in range(0, tm, sub):
        rows = slice(r0, r0 + sub)
        x = x_ref[rows, :]
        h = _rmsnorm(x, g_ref[...]).astype(BF16)

        u = ub_ref[rows, :]
        if r0 == 0:
            u_halo = jnp.where(at_start, 0.0, ubprev_ref[...])
            z_halo = jnp.where(at_start, 0.0, cgprev_ref[...] * zprev_ref[...])
        else:
            u_halo = ub_ref[r0 - POOL_HALO:r0, :]
            z_halo = cg_ref[r0 - SUBLANES:r0, :] * z_ref[r0 - SUBLANES:r0, :]
        ext = jnp.concatenate([u_halo, u], axis=0)
        sums = {1: ext}
        span = 1
        while span < POOL_WINDOWS[-1]:
            sums[2 * span] = sums[span] + pltpu.roll(sums[span], span, 0)
            span *= 2
        win_sum = sums[POOL_WINDOWS[-1]][POOL_HALO:, :]
        win = jnp.full((sub, MIX_W), float(POOL_WINDOWS[-1]), F32)
        for gi in range(len(POOL_WINDOWS) - 2, -1, -1):
            win_sum = jnp.where(group == gi, sums[POOL_WINDOWS[gi]][POOL_HALO:, :], win_sum)
            win = jnp.where(group == gi, float(POOL_WINDOWS[gi]), win)
        pos = (i * tm + r0) % seq + row
        count = jnp.minimum((pos + 1).astype(F32), win)
        y_pool = _mm(win_sum / count - u, poolw_ref[...]) * pscale_ref[...]

        zext = jnp.concatenate([z_halo, cg_ref[rows, :] * z_ref[rows, :]], axis=0)
        conv = (convw_ref[2:3, :] * zext
                + convw_ref[1:2, :] * pltpu.roll(zext, 1, 0)
                + convw_ref[0:1, :] * pltpu.roll(zext, 2, 0))
        y_conv = bg_ref[rows, :] * conv[SUBLANES:, :]

        ys = [y.astype(BF16) for y in (ya_ref[rows, :], y_pool, yc_ref[rows, :], y_conv)]
        merged = []
        for j in range(d // MIX_W):
            cols = slice(j * MIX_W, (j + 1) * MIX_W)
            acc = None
            for gi in range(N_BRANCH):
                wgate_ref = (wgate01_ref, wgate23_ref)[gi // 2]
                gcols = slice((gi % 2) * d + j * MIX_W, (gi % 2) * d + (j + 1) * MIX_W)
                gate = jax.nn.sigmoid(jnp.dot(h, wgate_ref[:, gcols], preferred_element_type=F32))
                term = jnp.dot(ys[gi], wbr_ref[gi, :, cols], preferred_element_type=F32) * gate
                acc = term if acc is None else acc + term
            merged.append(acc.astype(BF16))
        o_ref[rows, :] = x + jnp.dot(jnp.concatenate(merged, axis=1), wout_ref[...],
                                     preferred_element_type=F32)


def _merge(x, p, y_s5, y_rwkv, norm_g, w_in, w_branch, w_out, layer, pool_bd, pool_scale, conv_w, *,
           seq, tm=1024, sub=512):
    t, d = x.shape
    tm = min(tm, seq)
    gate_block = 2 * d
    first_gate_block = 8 * MIX_W // gate_block

    def cur(col):
        return pl.BlockSpec((tm, MIX_W), lambda i: (i, col))

    def prev(col, rows):
        per_tile = tm // rows
        return pl.BlockSpec((rows, MIX_W), lambda i: (jnp.maximum(i * per_tile - 1, 0), col))

    in_specs = [
        pl.BlockSpec((tm, d), lambda i: (i, 0)),
        cur(1), prev(1, POOL_HALO), cur(5), prev(5, SUBLANES), cur(6), cur(7), prev(7, SUBLANES),
        pl.BlockSpec((tm, MIX_W), lambda i: (i, 0)),
        pl.BlockSpec((tm, MIX_W), lambda i: (i, 0)),
        _layer_spec(norm_g.shape, layer),
        _layer_spec(w_in.shape, layer, col_block=(gate_block, first_gate_block)),
        _layer_spec(w_in.shape, layer, col_block=(gate_block, first_gate_block + 1)),
        _layer_spec(w_branch.shape, layer), _layer_spec(w_out.shape, layer),
        _layer_spec(pool_bd.shape, layer), _layer_spec(pool_scale.shape, layer),
        _layer_spec(conv_w.shape, layer),
    ]
    return pl.pallas_call(
        functools.partial(_merge_body, seq=seq, sub=min(sub, tm)),
        grid=(t // tm,),
        in_specs=in_specs,
        out_specs=pl.BlockSpec((tm, d), lambda i: (i, 0)),
        out_shape=jax.ShapeDtypeStruct((t, d), F32),
        compiler_params=_params(1),
        name="merge",
    )(x, p, p, p, p, p, p, p, y_s5, y_rwkv, norm_g, w_in, w_in, w_branch, w_out,
      pool_bd, pool_scale, conv_w)


def kernel(x, ffn1_norm, ffn1_w_gate, ffn1_w_up, ffn1_w_down, mix_norm, w_in, s5_lambda_re, s5_lambda_im, s5_log_dt, s5_b_re, s5_b_im, s5_c_re, s5_c_im, s5_d, s5_w_glu, pool_w, pool_scale, rwkv_mu_rkv, rwkv_mu_wag, rwkv_w0, rwkv_w1, rwkv_w2, rwkv_a0, rwkv_a1, rwkv_a2, rwkv_g1, rwkv_g2, rwkv_k_k, rwkv_k_a, rwkv_r_k, rwkv_ln_w, rwkv_ln_b, conv_w, w_branch, w_out, ffn2_norm, ffn2_w_gate, ffn2_w_up, ffn2_w_down, final_norm):
    batch, seq, d = x.shape
    depth = w_in.shape[0]
    bf = lambda w: w.astype(BF16)
    ffn1 = (ffn1_w_gate, ffn1_w_up, ffn1_w_down)
    ffn2 = (ffn2_w_gate, ffn2_w_up, ffn2_w_down)
    w_in_b, w_branch_b, w_out_b = bf(w_in), bf(w_branch), bf(w_out)
    vec = lambda a: a.reshape(depth, 1, -1)
    ffn1_g, ffn2_g, mix_g = vec(ffn1_norm), vec(ffn2_norm), vec(mix_norm)
    ranks = (rwkv_w1.shape[2], rwkv_a1.shape[2], rwkv_g1.shape[2])
    w_cat = jnp.concatenate([rwkv_w1, rwkv_a1, rwkv_g1], axis=2)
    mu_mat = jnp.concatenate(
        [jnp.broadcast_to(rwkv_mu_wag[:, j, :, None], (depth, d, rk)) for j, rk in enumerate(ranks)],
        axis=2)
    proj_params = (mu_mat, w_cat, vec(rwkv_w0), rwkv_w2, vec(rwkv_a0), rwkv_a2, rwkv_g2)
    s5_params = _s5_params(s5_lambda_re, s5_lambda_im, s5_log_dt, s5_b_re, s5_b_im, s5_c_re, s5_c_im,
                           s5_d, s5_w_glu)
    rwkv_params = (rwkv_mu_rkv, vec(rwkv_k_k), vec(rwkv_k_a), vec(rwkv_r_k), vec(rwkv_ln_w),
                   vec(rwkv_ln_b))
    pool_groups = len(POOL_WINDOWS)
    pool_bd = jnp.einsum("lgcd,gk->lgckd", pool_w, jnp.eye(pool_groups, dtype=F32)).reshape(
        depth, MIX_W, MIX_W)
    pool_s = vec(pool_scale)
    xf = x.reshape(batch * seq, d)
    for l in range(depth):
        xf = _ffn(xf, ffn1_g, *ffn1, l)
        p, lora = _proj(xf, mix_g, w_in_b, l, *proj_params, seq=seq)
        y_s5 = _s5(p, s5_params, l, batch=batch, seq=seq)
        y_rwkv = _rwkv(p, lora, rwkv_params, l, batch=batch, seq=seq)
        xf = _merge(xf, p, y_s5, y_rwkv, mix_g, w_in_b, w_branch_b, w_out_b, l,
                    pool_bd, pool_s, conv_w, seq=seq)
        xf = _ffn(xf, ffn2_g, *ffn2, l, final_norm if l == depth - 1 else None)
    return xf.reshape(batch, seq, d)
```

```python
import functools
import math

import jax
import jax.numpy as jnp
from jax import lax
from jax.experimental import pallas as pl
from jax.experimental.pallas import tpu as pltpu

F32 = jnp.float32
BF16 = jnp.bfloat16

MIX_W = 256
N_BRANCH = 4
S5_GROUPS = 16
S5_STATE = 64
POOL_WINDOWS = (2, 4, 8, 16)
POOL_HALO = 16
RW_HEAD = 64
RW_CHUNK = 64
RW_GN_EPS = 64e-5
NORM_EPS = 1e-6
SUBLANES = 8
VMEM_LIMIT_BYTES = 56 * 1024 * 1024


def _params(n_parallel, n_arbitrary=0):
    return pltpu.CompilerParams(
        dimension_semantics=("parallel",) * n_parallel + ("arbitrary",) * n_arbitrary,
        vmem_limit_bytes=VMEM_LIMIT_BYTES)


def _const_spec(shape):
    zeros = (0,) * len(shape)
    return pl.BlockSpec(shape, lambda *_: zeros, pipeline_mode=pl.Buffered(1))


def _layer_spec(stacked_shape, layer, col_block=None):
    shape = tuple(stacked_shape[1:])
    tail = (0,) * (len(shape) - 1)
    if col_block is None:
        index = (layer,) + tail + (0,)
    else:
        width, col = col_block
        shape = shape[:-1] + (width,)
        index = (layer,) + tail + (col,)
    return pl.BlockSpec((None,) + shape, lambda *_: index, pipeline_mode=pl.Buffered(1))


def _rmsnorm(x, g):
    return x * lax.rsqrt(jnp.mean(x * x, axis=-1, keepdims=True) + NORM_EPS) * g


def _mm(a, b):
    return jnp.dot(a.astype(BF16), b.astype(BF16), preferred_element_type=F32)


def _mm_nt(a, b):
    return lax.dot_general(a.astype(BF16), b.astype(BF16), (((1,), (1,)), ((), ())),
                           preferred_element_type=F32)


def _mm_tn(a, b):
    return lax.dot_general(a.astype(BF16), b.astype(BF16), (((0,), (0,)), ((), ())),
                           preferred_element_type=F32)


def _shift_rows(cur, prev_row):
    rolled = pltpu.roll(cur, 1, 0)
    row = lax.broadcasted_iota(jnp.int32, cur.shape, 0)
    return jnp.where(row == 0, prev_row, rolled)


def _softplus(z):
    return jnp.maximum(z, 0.0) + jnp.log(1.0 + jnp.exp(-jnp.abs(z)))


def _ffn_body(x_ref, g_ref, wg_ref, wu_ref, wd_ref, *rest, n_chunks, final):
    if final:
        fg_ref, o_ref, wg_s, wu_s, wd_s, h_s, acc_ref = rest
    else:
        o_ref, wg_s, wu_s, wd_s, h_s, acc_ref = rest
    s = pl.program_id(0)

    def chunk(h, j):
        gate = jnp.dot(h, wg_s[j], preferred_element_type=F32)
        up = jnp.dot(h, wu_s[j], preferred_element_type=F32)
        act = (gate * jax.nn.sigmoid(gate) * up).astype(BF16)
        return jnp.dot(act, wd_s[j], preferred_element_type=F32)

    def finish():
        y = x_ref[...] + 0.5 * acc_ref[...]
        if final:
            y = _rmsnorm(y, fg_ref[...])
        o_ref[...] = y

    @pl.when(s < n_chunks)
    def _():
        wg_s[s] = wg_ref[...].astype(BF16)
        wu_s[s] = wu_ref[...].astype(BF16)
        wd_s[s] = wd_ref[...].astype(BF16)

        @pl.when(s == 0)
        def _():
            h_s[...] = _rmsnorm(x_ref[...], g_ref[...]).astype(BF16)
            acc_ref[...] = jnp.zeros_like(acc_ref)

        acc_ref[...] += chunk(h_s[...], s)

        @pl.when(s == n_chunks - 1)
        def _():
            finish()

    @pl.when(s >= n_chunks)
    def _():
        h = _rmsnorm(x_ref[...], g_ref[...]).astype(BF16)
        for j in range(n_chunks):
            contrib = chunk(h, j)
            if j == 0:
                acc_ref[...] = contrib
            else:
                acc_ref[...] += contrib
        finish()


def _ffn(x, norm_g, w_gate, w_up, w_down, layer, final_g=None, *, tm=1024, f_chunk=256):
    t, d = x.shape
    d_ff = w_gate.shape[2]
    tm = min(tm, t)
    n_chunks = d_ff // f_chunk
    final = final_g is not None
    tile = lambda s: (jnp.maximum(s - (n_chunks - 1), 0), 0)
    in_specs = [
        pl.BlockSpec((tm, d), tile),
        _layer_spec(norm_g.shape, layer),
        pl.BlockSpec((None, d, f_chunk), lambda s: (layer, 0, jnp.minimum(s, n_chunks - 1))),
        pl.BlockSpec((None, d, f_chunk), lambda s: (layer, 0, jnp.minimum(s, n_chunks - 1))),
        pl.BlockSpec((None, f_chunk, d), lambda s: (layer, jnp.minimum(s, n_chunks - 1), 0)),
    ]
    args = [x, norm_g, w_gate, w_up, w_down]
    if final:
        in_specs.append(_const_spec((1, d)))
        args.append(final_g.reshape(1, d))
    return pl.pallas_call(
        functools.partial(_ffn_body, n_chunks=n_chunks, final=final),
        grid=(n_chunks - 1 + t // tm,),
        in_specs=in_specs,
        out_specs=pl.BlockSpec((tm, d), tile),
        out_shape=jax.ShapeDtypeStruct((t, d), F32),
        scratch_shapes=[pltpu.VMEM((n_chunks, d, f_chunk), BF16), pltpu.VMEM((n_chunks, d, f_chunk), BF16),
                        pltpu.VMEM((n_chunks, f_chunk, d), BF16), pltpu.VMEM((tm, d), BF16),
                        pltpu.VMEM((tm, d), F32)],
        compiler_params=_params(0, 1),
        name="ffn_final" if final else "ffn",
    )(*args)


P_MIX = (0, 2, 3, 4)
P_LOC = (1, 5, 6, 7)

def _proj_body(x_ref, xprev_ref, g_ref, win_ref, mumat_ref, wcat_ref, w0_ref, w2_ref, a0_ref, a2_ref,
               g2_ref, pmix_ref, ploc_ref, lora_ref, wcat_s, wfold_s, *, seq, ranks):
    i = pl.program_id(0)
    tm = x_ref.shape[0]

    @pl.when(i == 0)
    def _():
        wcat = wcat_ref[...]
        wcat_s[...] = wcat.astype(BF16)
        wfold_s[...] = (mumat_ref[...] * wcat).astype(BF16)

    g = g_ref[...]
    h = _rmsnorm(x_ref[...], g)
    at_start = (i * tm) % seq == 0
    h_prev = _rmsnorm(xprev_ref[SUBLANES - 1:SUBLANES, :], g)
    h_prev = jnp.where(at_start, 0.0, h_prev)
    hb = h.astype(BF16)
    hxb = (_shift_rows(h, h_prev) - h).astype(BF16)

    def main_cols(lo, hi):
        for j in range(lo, hi):
            cols = slice(j * MIX_W, (j + 1) * MIX_W)
            out_ref, slot = ((pmix_ref, P_MIX.index(j)) if j in P_MIX else (ploc_ref, P_LOC.index(j)))
            out_ref[:, slot * MIX_W:(slot + 1) * MIX_W] = jnp.dot(hb, win_ref[:, cols],
                                                                  preferred_element_type=F32)

    main_cols(0, 4)
    t = (jnp.dot(hb, wcat_s[...], preferred_element_type=F32)
         + jnp.dot(hxb, wfold_s[...], preferred_element_type=F32))
    r_w, r_a, r_g = ranks
    t_w = jnp.tanh(t[:, 0:r_w])
    t_a = t[:, r_w:r_w + r_a]
    t_g = jax.nn.sigmoid(t[:, r_w + r_a:r_w + r_a + r_g])
    main_cols(4, 8)
    w_log = -_softplus(-(w0_ref[...] + _mm(t_w, w2_ref[...]))) - 0.5
    lora_ref[:, 0:MIX_W] = -jnp.exp(w_log)
    lora_ref[:, MIX_W:2 * MIX_W] = jax.nn.sigmoid(a0_ref[...] + _mm(t_a, a2_ref[...]))
    lora_ref[:, 2 * MIX_W:3 * MIX_W] = _mm(t_g, g2_ref[...])


def _proj(x, norm_g, w_in, layer, mu_mat, w_cat, w0, w2, a0, a2, g2, *, seq, tm=1024):
    t, d = x.shape
    tm = min(tm, seq)
    n_small = 8 * MIX_W
    blocks_per_tile = tm // SUBLANES
    ranks = (w2.shape[1], a2.shape[1], g2.shape[1])
    in_specs = [
        pl.BlockSpec((tm, d), lambda i: (i, 0)),
        pl.BlockSpec((SUBLANES, d), lambda i: (jnp.maximum(i * blocks_per_tile - 1, 0), 0)),
        _layer_spec(norm_g.shape, layer),
        _layer_spec(w_in.shape, layer, col_block=(n_small, 0)),
    ] + [_layer_spec(a.shape, layer) for a in (mu_mat, w_cat, w0, w2, a0, a2, g2)]
    return pl.pallas_call(
        functools.partial(_proj_body, seq=seq, ranks=ranks),
        grid=(t // tm,),
        in_specs=in_specs,
        out_specs=[pl.BlockSpec((tm, len(P_MIX) * MIX_W), lambda i: (i, 0)),
                   pl.BlockSpec((tm, len(P_LOC) * MIX_W), lambda i: (i, 0)),
                   pl.BlockSpec((tm, 3 * MIX_W), lambda i: (i, 0))],
        out_shape=[jax.ShapeDtypeStruct((t, len(P_MIX) * MIX_W), F32),
                   jax.ShapeDtypeStruct((t, len(P_LOC) * MIX_W), F32),
                   jax.ShapeDtypeStruct((t, 3 * MIX_W), F32)],
        scratch_shapes=[pltpu.VMEM((d, sum(ranks)), BF16), pltpu.VMEM((d, sum(ranks)), BF16)],
        compiler_params=_params(0, 1),
        name="proj",
    )(x, x, norm_g, w_in, mu_mat, w_cat, w0, w2, a0, a2, g2)


def _s5_body(u_ref, lre_ref, lim_ref, ldt_ref, bre_ref, bim_ref, cre_ref, cim_ref, d_ref, wglu_ref,
             o_ref, wre_s, wim_s, abar_s, xre_s, xim_s, hre_s, him_s, st_s, *, unroll):
    @pl.when(pl.program_id(0) == 0)
    def _():
        lr = lre_ref[...]
        li = lim_ref[...]
        dt = jnp.exp(ldt_ref[...])
        mag = jnp.exp(lr * dt)
        ar = mag * jnp.cos(li * dt)
        ai = mag * jnp.sin(li * dt)
        inv = 1.0 / (lr * lr + li * li)
        qr, qi = lr * inv, -li * inv
        coef_re = (ar - 1.0) * qr - ai * qi
        coef_im = (ar - 1.0) * qi + ai * qr
        wre_s[...] = (coef_re * bre_ref[...] - coef_im * bim_ref[...]).astype(BF16)
        wim_s[...] = (coef_re * bim_ref[...] + coef_im * bre_ref[...]).astype(BF16)
        abar_s[0:1, :] = ar
        abar_s[1:2, :] = ai
        st_s[...] = jnp.zeros_like(st_s)

    nb, tl = u_ref.shape[0], u_ref.shape[1]
    rows, n_state = nb * tl, wre_s.shape[1]
    u = u_ref[...].reshape(rows, MIX_W)
    ub = u.astype(BF16)
    blocks_per_row = tl // SUBLANES
    blocks = (nb * blocks_per_row, SUBLANES, n_state)
    xre_s[...] = jnp.dot(ub, wre_s[...], preferred_element_type=F32).reshape(blocks)
    xim_s[...] = jnp.dot(ub, wim_s[...], preferred_element_type=F32).reshape(blocks)
    ar = abar_s[0:1, :]
    ai = abar_s[1:2, :]

    def step(j, carry):
        carry = list(carry)
        for q in range(SUBLANES):
            for b in range(nb):
                blk = b * blocks_per_row + j
                hr, hi = carry[b]
                nr = ar * hr - ai * hi + xre_s[blk, q:q + 1, :]
                ni = ar * hi + ai * hr + xim_s[blk, q:q + 1, :]
                hre_s[blk, q:q + 1, :] = nr
                him_s[blk, q:q + 1, :] = ni
                carry[b] = (nr, ni)
        return tuple(carry)

    init = tuple((st_s[2 * b:2 * b + 1, :], st_s[2 * b + 1:2 * b + 2, :]) for b in range(nb))
    final = lax.fori_loop(0, blocks_per_row, step, init, unroll=unroll)
    for b in range(nb):
        st_s[2 * b:2 * b + 1, :] = final[b][0]
        st_s[2 * b + 1:2 * b + 2, :] = final[b][1]

    y = (_mm(hre_s[...].reshape(rows, n_state), cre_ref[...])
         - _mm(him_s[...].reshape(rows, n_state), cim_ref[...]) + d_ref[...] * u)
    gl = jax.nn.gelu(y, approximate=True)
    o_ref[...] = (gl * jax.nn.sigmoid(_mm(gl, wglu_ref[...]))).reshape(nb, tl, MIX_W)


def _s5_params(lam_re, lam_im, log_dt, b_re, b_im, c_re, c_im, d_skip, w_glu):
    depth = lam_re.shape[0]
    n_state = S5_GROUPS * S5_STATE
    eye = jnp.eye(S5_GROUPS, dtype=F32)
    row = lambda v: v.reshape(depth, 1, -1)
    b_bd = lambda b: jnp.einsum("lgph,gk->lghkp", b, eye).reshape(depth, MIX_W, n_state)
    c_bd = lambda c: jnp.einsum("lghp,gk->lgpkh", c, eye).reshape(depth, n_state, MIX_W)
    dt_rows = jnp.broadcast_to(log_dt[:, :, None], (depth, S5_GROUPS, S5_STATE))
    return (row(lam_re), row(lam_im), row(dt_rows), b_bd(b_re), b_bd(b_im),
            c_bd(c_re).astype(BF16), c_bd(c_im).astype(BF16), row(d_skip), w_glu.astype(BF16))


def _s5(p, params, layer, *, batch, seq, tl=512):
    n_state = S5_GROUPS * S5_STATE
    tl = min(tl, seq)
    in_specs = ([pl.BlockSpec((batch, tl, MIX_W), lambda i: (0, i, 0))]
                + [_layer_spec(a.shape, layer) for a in params])
    scan_buf = pltpu.VMEM((batch * tl // SUBLANES, SUBLANES, n_state), F32)
    out = pl.pallas_call(
        functools.partial(_s5_body, unroll=2),
        grid=(seq // tl,),
        in_specs=in_specs,
        out_specs=pl.BlockSpec((batch, tl, MIX_W), lambda i: (0, i, 0)),
        out_shape=jax.ShapeDtypeStruct((batch, seq, MIX_W), F32),
        scratch_shapes=[pltpu.VMEM((MIX_W, n_state), BF16), pltpu.VMEM((MIX_W, n_state), BF16),
                        pltpu.VMEM((2, n_state), F32)]
                       + [scan_buf] * 4 + [pltpu.VMEM((2 * batch, n_state), F32)],
        compiler_params=_params(0, 1),
        name="s5",
    )(p.reshape(batch, seq, p.shape[1]), *params)
    return out.reshape(batch * seq, MIX_W)


def _segment_ones(n, seg):
    r = lax.broadcasted_iota(jnp.int32, (n, n), 0) // seg
    c = lax.broadcasted_iota(jnp.int32, (n, n), 1) // seg
    return r == c


def _rwkv_body(pm_ref, pmprev_ref, lora_ref, mu_ref, kk_ref, ka_ref, rk_ref, lnw_ref, lnb_ref, o_ref,
               st_s):
    nb, tl = pm_ref.shape[0], pm_ref.shape[1]
    rows = nb * tl
    c = RW_CHUNK
    first = pl.program_id(0) == 0

    @pl.when(first)
    def _():
        st_s[...] = jnp.zeros_like(st_s)

    first_row = lax.broadcasted_iota(jnp.int32, (SUBLANES, MIX_W), 0) == 0

    def mixed(slot, mu):
        cols = slice(slot * MIX_W, (slot + 1) * MIX_W)
        cur = pm_ref[:, :, cols].reshape(rows, MIX_W)
        shifted = pltpu.roll(cur, 1, 0)
        pieces = []
        for b in range(nb):
            prev_row = jnp.where(first, 0.0, pmprev_ref[b, SUBLANES - 1:SUBLANES, cols])
            head = slice(b * tl, b * tl + SUBLANES)
            pieces += [jnp.where(first_row, prev_row, shifted[head]), shifted[b * tl + SUBLANES:(b + 1) * tl]]
        shifted = jnp.concatenate(pieces, axis=0)
        return cur + (shifted - cur) * mu

    r = mixed(1, mu_ref[0:1, :])
    k = mixed(2, mu_ref[1:2, :])
    v = mixed(3, mu_ref[2:3, :])
    lora = lora_ref[...].reshape(rows, 3 * MIX_W)
    logw = lora[:, 0:MIX_W]
    a = lora[:, MIX_W:2 * MIX_W]
    head_ones = _segment_ones(MIX_W, RW_HEAD).astype(BF16)
    kraw = k * kk_ref[...]
    kk = kraw * lax.rsqrt(jnp.maximum(_mm(kraw * kraw, head_ones), 1e-24))
    kmod = k * (1.0 + (a - 1.0) * ka_ref[...])

    bdot = functools.partial(jnp.dot, preferred_element_type=F32)
    n_chunks = rows // c
    chunks_per_row = tl // c

    tri = (lax.broadcasted_iota(jnp.int32, (c, c), 1)
           <= lax.broadcasted_iota(jnp.int32, (c, c), 0)).astype(BF16)
    w_hi = logw.astype(BF16)
    rem = logw - w_hi.astype(F32)
    w_mid = rem.astype(BF16)
    w_lo = (rem - w_mid.astype(F32)).astype(BF16)
    cum = jnp.concatenate(
        [bdot(tri, w_hi[n * c:(n + 1) * c]) + bdot(tri, w_mid[n * c:(n + 1) * c])
         + bdot(tri, w_lo[n * c:(n + 1) * c]) for n in range(n_chunks)], axis=0)
    tot = jnp.concatenate(
        [jnp.broadcast_to(cum[(n + 1) * c - 1:(n + 1) * c, :], (c, MIX_W)) for n in range(n_chunks)],
        axis=0)
    inv = jnp.exp(-cum)
    to_end = jnp.exp(tot - cum)
    r_dec = (r * jnp.exp(cum)).astype(BF16)
    a_dec = (-kk * jnp.exp(cum - logw)).astype(BF16)
    b_inv = (kk * a * inv).astype(BF16)
    k_inv = (kmod * inv).astype(BF16)
    b_end = (kk * a * to_end).astype(BF16)
    k_end = (kmod * to_end).astype(BF16)
    v_b = v.astype(BF16)
    p_end = [jnp.exp(cum[(n + 1) * c - 1:(n + 1) * c, :]) for n in range(n_chunks)]

    pair_w = 2 * RW_HEAD
    n_pairs = MIX_W // pair_w
    row_i = lax.broadcasted_iota(jnp.int32, (c, pair_w), 0)
    lane_i = lax.broadcasted_iota(jnp.int32, (c, pair_w), 1)
    left = lane_i < RW_HEAD
    strict = (lane_i % RW_HEAD) < row_i
    incl = (lane_i % RW_HEAD) <= row_i
    eye_fam = ((lane_i % RW_HEAD) == row_i).astype(F32)
    pr = lax.broadcasted_iota(jnp.int32, (pair_w, pair_w), 0)
    pc = lax.broadcasted_iota(jnp.int32, (pair_w, pair_w), 1)
    same_head = (pr // RW_HEAD) == (pc // RW_HEAD)
    eye_pair = (pr == pc).astype(F32)

    def halves(x):
        z = jnp.zeros_like(x)
        return jnp.where(left, x, z), jnp.where(left, z, x)

    def blockdiag(x):
        return jnp.concatenate(halves(x), axis=0)

    probs = [(q, n) for q in range(n_pairs) for n in range(n_chunks)]
    n_probs = len(probs)

    def piece(x, q, n):
        return x[n * c:(n + 1) * c, q * pair_w:(q + 1) * pair_w]

    a_d = [piece(a_dec, q, n) for q, n in probs]
    r_d = [piece(r_dec, q, n) for q, n in probs]
    vv = [piece(v_b, q, n) for q, n in probs]
    vv_bd = [blockdiag(x) for x in vv]
    g = [_mm_nt(jnp.concatenate([a_d[i], r_d[i]], axis=0),
                jnp.concatenate(halves(piece(b_inv, q, n)) + halves(piece(k_inv, q, n)), axis=0))
         for i, (q, n) in enumerate(probs)]
    a_ab = [jnp.where(strict, x[0:c, 0:pair_w], 0.0) for x in g]
    a_ak = [jnp.where(strict, x[0:c, pair_w:2 * pair_w], 0.0).astype(BF16) for x in g]
    lhs_o = [jnp.concatenate([r_d[i], jnp.where(incl, x[c:2 * c, 0:pair_w], 0.0).astype(BF16),
                              jnp.where(incl, x[c:2 * c, pair_w:2 * pair_w], 0.0).astype(BF16)], axis=1)
             for i, x in enumerate(g)]
    tinv = [eye_fam + x for x in a_ab]
    pw = [x.astype(BF16) for x in a_ab]
    pw = [bdot(x, blockdiag(x)).astype(BF16) for x in pw]
    akv = [bdot(a_ak[i], vv_bd[i]).astype(BF16) for i in range(n_probs)]
    n_steps = int(math.log2(c)) - 1
    for s in range(n_steps):
        if s + 1 < n_steps:
            both = [bdot(jnp.concatenate([tinv[i].astype(BF16), pw[i]], axis=0), blockdiag(pw[i]))
                    for i in range(n_probs)]
            tinv = [tinv[i] + both[i][0:c] for i in range(n_probs)]
            pw = [x[c:2 * c].astype(BF16) for x in both]
        else:
            tinv = [tinv[i] + bdot(tinv[i].astype(BF16), blockdiag(pw[i])) for i in range(n_probs)]
    w1u0 = [bdot(tinv[i].astype(BF16), jnp.concatenate([blockdiag(a_d[i]), blockdiag(akv[i])], axis=1))
            for i in range(n_probs)]
    tn = [_mm_tn(piece(b_end, q, n), w1u0[i]) for i, (q, n) in enumerate(probs)]
    kv = [_mm_tn(piece(k_end, q, n), vv[i]) for i, (q, n) in enumerate(probs)]
    trans = [(jnp.where(same_head, tn[i][:, 0:pair_w], 0.0)
              + eye_pair * p_end[n][:, q * pair_w:(q + 1) * pair_w]).astype(BF16)
             for i, (q, n) in enumerate(probs)]
    add = [jnp.where(same_head, tn[i][:, pair_w:2 * pair_w] + kv[i], 0.0) for i in range(n_probs)]
    w1 = [x[:, 0:pair_w].astype(BF16) for x in w1u0]
    u0 = [x[:, pair_w:2 * pair_w] for x in w1u0]

    st_at = {}
    chains = [(q, b) for q in range(n_pairs) for b in range(nb)]
    st = {qb: st_s[qb[0] * nb + qb[1]] for qb in chains}
    for step in range(chunks_per_row):
        for q, b in chains:
            i = q * n_chunks + b * chunks_per_row + step
            st_b = st[q, b].astype(BF16)
            st_at[i] = st_b
            st[q, b] = bdot(trans[i], st_b) + add[i]
    for q, b in chains:
        st_s[q * nb + b] = st[q, b]
    u = [(bdot(w1[i], st_at[i]) + u0[i]).astype(BF16) for i in range(n_probs)]
    o_p = [bdot(lhs_o[i], jnp.concatenate([st_at[i], blockdiag(u[i]), vv_bd[i]], axis=0))
           for i in range(n_probs)]
    o = jnp.concatenate(
        [jnp.concatenate(o_p[q * n_chunks:(q + 1) * n_chunks], axis=0) for q in range(n_pairs)], axis=1)

    inv_n = 1.0 / RW_HEAD
    mean = _mm(o, head_ones) * inv_n
    dlt = o - mean
    var = _mm(dlt * dlt, head_ones) * inv_n
    o_n = dlt * lax.rsqrt(var + RW_GN_EPS) * lnw_ref[...] + lnb_ref[...]
    bonus = _mm(r * kmod * rk_ref[...], head_ones) * v
    o_ref[...] = ((o_n + bonus) * lora[:, 2 * MIX_W:3 * MIX_W]).reshape(nb, tl, MIX_W)


def _rwkv(p, lora, params, layer, *, batch, seq, tl=256):
    tl = min(tl, seq)
    blocks_per_tile = tl // SUBLANES
    width = p.shape[1]
    p3 = p.reshape(batch, seq, width)
    lora3 = lora.reshape(batch, seq, lora.shape[1])
    in_specs = [pl.BlockSpec((batch, tl, width), lambda i: (0, i, 0)),
                pl.BlockSpec((batch, SUBLANES, width),
                             lambda i: (0, jnp.maximum(i * blocks_per_tile - 1, 0), 0)),
                pl.BlockSpec((batch, tl, 3 * MIX_W), lambda i: (0, i, 0))]
    in_specs += [_layer_spec(a.shape, layer) for a in params]
    pair_w = 2 * RW_HEAD
    out = pl.pallas_call(
        _rwkv_body,
        grid=(seq // tl,),
        in_specs=in_specs,
        out_specs=pl.BlockSpec((batch, tl, MIX_W), lambda i: (0, i, 0)),
        out_shape=jax.ShapeDtypeStruct((batch, seq, MIX_W), F32),
        scratch_shapes=[pltpu.VMEM((batch * MIX_W // pair_w, pair_w, pair_w), F32)],
        compiler_params=_params(0, 1),
        name="rwkv",
    )(p3, p3, lora3, *params)
    return out.reshape(batch * seq, MIX_W)


def _merge_body(x_ref, pl_ref, plprev_ref, ya_ref, yc_ref, g_ref, wgate01_ref, wgate23_ref, wbr_ref, wout_ref, poolw_ref, pscale_ref,
                convw_ref, o_ref, *, seq, sub):
    i = pl.program_id(0)
    tm, d = x_ref.shape
    at_start = (i * tm) % seq == 0
    lane = lax.broadcasted_iota(jnp.int32, (sub, MIX_W), 1)
    group = lane // (MIX_W // len(POOL_WINDOWS))
    row = lax.broadcasted_iota(jnp.int32, (sub, MIX_W), 0)

    for r0 in range(0, tm, sub):
        rows = slice(r0, r0 + sub)
        x = x_ref[rows, :]
        h = _rmsnorm(x, g_ref[...]).astype(BF16)

        ub, zc, bc, cc = (slice(k * MIX_W, (k + 1) * MIX_W) for k in range(4))
        u = pl_ref[rows, ub]
        if r0 == 0:
            tail = slice(POOL_HALO - SUBLANES, POOL_HALO)
            u_halo = jnp.where(at_start, 0.0, plprev_ref[:, ub])
            z_halo = jnp.where(at_start, 0.0, plprev_ref[tail, cc] * plprev_ref[tail, zc])
        else:
            u_halo = pl_ref[r0 - POOL_HALO:r0, ub]
            z_halo = pl_ref[r0 - SUBLANES:r0, cc] * pl_ref[r0 - SUBLANES:r0, zc]
        ext = jnp.concatenate([u_halo, u], axis=0)
        sums = {1: ext}
        span = 1
        while span < POOL_WINDOWS[-1]:
            sums[2 * span] = sums[span] + pltpu.roll(sums[span], span, 0)
            span *= 2
        win_sum = sums[POOL_WINDOWS[-1]][POOL_HALO:, :]
        win = jnp.full((sub, MIX_W), float(POOL_WINDOWS[-1]), F32)
        for gi in range(len(POOL_WINDOWS) - 2, -1, -1):
            win_sum = jnp.where(group == gi, sums[POOL_WINDOWS[gi]][POOL_HALO:, :], win_sum)
            win = jnp.where(group == gi, float(POOL_WINDOWS[gi]), win)
        pos = (i * tm + r0) % seq + row
        count = jnp.minimum((pos + 1).astype(F32), win)
        y_pool = _mm(win_sum / count - u, poolw_ref[...]) * pscale_ref[...]

        zext = jnp.concatenate([z_halo, pl_ref[rows, cc] * pl_ref[rows, zc]], axis=0)
        conv = (convw_ref[2:3, :] * zext
                + convw_ref[1:2, :] * pltpu.roll(zext, 1, 0)
                + convw_ref[0:1, :] * pltpu.roll(zext, 2, 0))
        y_conv = pl_ref[rows, bc] * conv[SUBLANES:, :]

        ys = [y.astype(BF16) for y in (ya_ref[rows, :], y_pool, yc_ref[rows, :], y_conv)]
        merged = []
        for j in range(d // MIX_W):
            cols = slice(j * MIX_W, (j + 1) * MIX_W)
            acc = None
            for gi in range(N_BRANCH):
                wgate_ref = (wgate01_ref, wgate23_ref)[gi // 2]
                gcols = slice((gi % 2) * d + j * MIX_W, (gi % 2) * d + (j + 1) * MIX_W)
                gate = jax.nn.sigmoid(jnp.dot(h, wgate_ref[:, gcols], preferred_element_type=F32))
                term = jnp.dot(ys[gi], wbr_ref[gi, :, cols], preferred_element_type=F32) * gate
                acc = term if acc is None else acc + term
            merged.append(acc.astype(BF16))
        o_ref[rows, :] = x + jnp.dot(jnp.concatenate(merged, axis=1), wout_ref[...],
                                     preferred_element_type=F32)


def _merge(x, p, y_s5, y_rwkv, norm_g, w_in, w_branch, w_out, layer, pool_bd, pool_scale, conv_w, *,
           seq, tm=1024, sub=512):
    t, d = x.shape
    tm = min(tm, seq)
    gate_block = 2 * d
    first_gate_block = 8 * MIX_W // gate_block

    width = p.shape[1]
    halos_per_tile = tm // POOL_HALO
    in_specs = [
        pl.BlockSpec((tm, d), lambda i: (i, 0)),
        pl.BlockSpec((tm, width), lambda i: (i, 0)),
        pl.BlockSpec((POOL_HALO, width), lambda i: (jnp.maximum(i * halos_per_tile - 1, 0), 0)),
        pl.BlockSpec((tm, MIX_W), lambda i: (i, 0)),
        pl.BlockSpec((tm, MIX_W), lambda i: (i, 0)),
        _layer_spec(norm_g.shape, layer),
        _layer_spec(w_in.shape, layer, col_block=(gate_block, first_gate_block)),
        _layer_spec(w_in.shape, layer, col_block=(gate_block, first_gate_block + 1)),
        _layer_spec(w_branch.shape, layer), _layer_spec(w_out.shape, layer),
        _layer_spec(pool_bd.shape, layer), _layer_spec(pool_scale.shape, layer),
        _layer_spec(conv_w.shape, layer),
    ]
    return pl.pallas_call(
        functools.partial(_merge_body, seq=seq, sub=min(sub, tm)),
        grid=(t // tm,),
        in_specs=in_specs,
        out_specs=pl.BlockSpec((tm, d), lambda i: (i, 0)),
        out_shape=jax.ShapeDtypeStruct((t, d), F32),
        compiler_params=_params(1),
        name="merge",
    )(x, p, p, y_s5, y_rwkv, norm_g, w_in, w_in, w_branch, w_out,
      pool_bd, pool_scale, conv_w)


def kernel(x, ffn1_norm, ffn1_w_gate, ffn1_w_up, ffn1_w_down, mix_norm, w_in, s5_lambda_re, s5_lambda_im, s5_log_dt, s5_b_re, s5_b_im, s5_c_re, s5_c_im, s5_d, s5_w_glu, pool_w, pool_scale, rwkv_mu_rkv, rwkv_mu_wag, rwkv_w0, rwkv_w1, rwkv_w2, rwkv_a0, rwkv_a1, rwkv_a2, rwkv_g1, rwkv_g2, rwkv_k_k, rwkv_k_a, rwkv_r_k, rwkv_ln_w, rwkv_ln_b, conv_w, w_branch, w_out, ffn2_norm, ffn2_w_gate, ffn2_w_up, ffn2_w_down, final_norm):
    batch, seq, d = x.shape
    depth = w_in.shape[0]
    bf = lambda w: w.astype(BF16)
    ffn1 = (ffn1_w_gate, ffn1_w_up, ffn1_w_down)
    ffn2 = (ffn2_w_gate, ffn2_w_up, ffn2_w_down)
    w_in_b, w_branch_b, w_out_b = bf(w_in), bf(w_branch), bf(w_out)
    vec = lambda a: a.reshape(depth, 1, -1)
    ffn1_g, ffn2_g, mix_g = vec(ffn1_norm), vec(ffn2_norm), vec(mix_norm)
    ranks = (rwkv_w1.shape[2], rwkv_a1.shape[2], rwkv_g1.shape[2])
    w_cat = jnp.concatenate([rwkv_w1, rwkv_a1, rwkv_g1], axis=2)
    mu_mat = jnp.concatenate(
        [jnp.broadcast_to(rwkv_mu_wag[:, j, :, None], (depth, d, rk)) for j, rk in enumerate(ranks)],
        axis=2)
    proj_params = (mu_mat, w_cat, vec(rwkv_w0), rwkv_w2, vec(rwkv_a0), rwkv_a2, rwkv_g2)
    s5_params = _s5_params(s5_lambda_re, s5_lambda_im, s5_log_dt, s5_b_re, s5_b_im, s5_c_re, s5_c_im,
                           s5_d, s5_w_glu)
    rwkv_params = (rwkv_mu_rkv, vec(rwkv_k_k), vec(rwkv_k_a), vec(rwkv_r_k), vec(rwkv_ln_w),
                   vec(rwkv_ln_b))
    pool_groups = len(POOL_WINDOWS)
    pool_bd = jnp.einsum("lgcd,gk->lgckd", pool_w, jnp.eye(pool_groups, dtype=F32)).reshape(
        depth, MIX_W, MIX_W)
    pool_s = vec(pool_scale)
    xf = x.reshape(batch * seq, d)
    for l in range(depth):
        xf = _ffn(xf, ffn1_g, *ffn1, l)
        p_mix, p_loc, lora = _proj(xf, mix_g, w_in_b, l, *proj_params, seq=seq)
        y_s5 = _s5(p_mix, s5_params, l, batch=batch, seq=seq)
        y_rwkv = _rwkv(p_mix, lora, rwkv_params, l, batch=batch, seq=seq)
        xf = _merge(xf, p_loc, y_s5, y_rwkv, mix_g, w_in_b, w_branch_b, w_out_b, l,
                    pool_bd, pool_s, conv_w, seq=seq)
        xf = _ffn(xf, ffn2_g, *ffn2, l, final_norm if l == depth - 1 else None)
    return xf.reshape(batch, seq, d)
```

```python
import functools
import math

import jax
import jax.numpy as jnp
from jax import lax
from jax.experimental import pallas as pl
from jax.experimental.pallas import tpu as pltpu

F32 = jnp.float32
BF16 = jnp.bfloat16

MIX_W = 256
N_BRANCH = 4
S5_GROUPS = 16
S5_STATE = 64
POOL_WINDOWS = (2, 4, 8, 16)
POOL_HALO = 16
RW_HEAD = 64
RW_CHUNK = 64
RW_GN_EPS = 64e-5
NORM_EPS = 1e-6
SUBLANES = 8
VMEM_LIMIT_BYTES = 56 * 1024 * 1024


def _params(n_parallel, n_arbitrary=0):
    return pltpu.CompilerParams(
        dimension_semantics=("parallel",) * n_parallel + ("arbitrary",) * n_arbitrary,
        vmem_limit_bytes=VMEM_LIMIT_BYTES)


def _const_spec(shape):
    zeros = (0,) * len(shape)
    return pl.BlockSpec(shape, lambda *_: zeros, pipeline_mode=pl.Buffered(1))


def _layer_spec(stacked_shape, layer, col_block=None):
    shape = tuple(stacked_shape[1:])
    tail = (0,) * (len(shape) - 1)
    if col_block is None:
        index = (layer,) + tail + (0,)
    else:
        width, col = col_block
        shape = shape[:-1] + (width,)
        index = (layer,) + tail + (col,)
    return pl.BlockSpec((None,) + shape, lambda *_: index, pipeline_mode=pl.Buffered(1))


def _rmsnorm(x, g):
    return x * lax.rsqrt(jnp.mean(x * x, axis=-1, keepdims=True) + NORM_EPS) * g


def _mm(a, b):
    return jnp.dot(a.astype(BF16), b.astype(BF16), preferred_element_type=F32)


def _mm_nt(a, b):
    return lax.dot_general(a.astype(BF16), b.astype(BF16), (((1,), (1,)), ((), ())),
                           preferred_element_type=F32)


def _mm_tn(a, b):
    return lax.dot_general(a.astype(BF16), b.astype(BF16), (((0,), (0,)), ((), ())),
                           preferred_element_type=F32)


def _shift_rows(cur, prev_row):
    rolled = pltpu.roll(cur, 1, 0)
    row = lax.broadcasted_iota(jnp.int32, cur.shape, 0)
    return jnp.where(row == 0, prev_row, rolled)


def _softplus(z):
    return jnp.maximum(z, 0.0) + jnp.log(1.0 + jnp.exp(-jnp.abs(z)))


def _ffn_body(x_ref, g_ref, wg_ref, wu_ref, wd_ref, *rest, n_chunks, final):
    if final:
        fg_ref, o_ref, wg_s, wu_s, wd_s, h_s, acc_ref = rest
    else:
        o_ref, wg_s, wu_s, wd_s, h_s, acc_ref = rest
    s = pl.program_id(0)

    def chunk(h, j):
        gate = jnp.dot(h, wg_s[j], preferred_element_type=F32)
        up = jnp.dot(h, wu_s[j], preferred_element_type=F32)
        act = (gate * jax.nn.sigmoid(gate) * up).astype(BF16)
        return jnp.dot(act, wd_s[j], preferred_element_type=F32)

    def finish():
        y = x_ref[...] + 0.5 * acc_ref[...]
        if final:
            y = _rmsnorm(y, fg_ref[...])
        o_ref[...] = y

    @pl.when(s < n_chunks)
    def _():
        wg_s[s] = wg_ref[...].astype(BF16)
        wu_s[s] = wu_ref[...].astype(BF16)
        wd_s[s] = wd_ref[...].astype(BF16)

        @pl.when(s == 0)
        def _():
            h_s[...] = _rmsnorm(x_ref[...], g_ref[...]).astype(BF16)
            acc_ref[...] = jnp.zeros_like(acc_ref)

        acc_ref[...] += chunk(h_s[...], s)

        @pl.when(s == n_chunks - 1)
        def _():
            finish()

    @pl.when(s >= n_chunks)
    def _():
        h = _rmsnorm(x_ref[...], g_ref[...]).astype(BF16)
        for j in range(n_chunks):
            contrib = chunk(h, j)
            if j == 0:
                acc_ref[...] = contrib
            else:
                acc_ref[...] += contrib
        finish()


def _ffn(x, norm_g, w_gate, w_up, w_down, layer, final_g=None, *, tm=1024, f_chunk=256):
    t, d = x.shape
    d_ff = w_gate.shape[2]
    tm = min(tm, t)
    n_chunks = d_ff // f_chunk
    final = final_g is not None
    tile = lambda s: (jnp.maximum(s - (n_chunks - 1), 0), 0)
    in_specs = [
        pl.BlockSpec((tm, d), tile),
        _layer_spec(norm_g.shape, layer),
        pl.BlockSpec((None, d, f_chunk), lambda s: (layer, 0, jnp.minimum(s, n_chunks - 1))),
        pl.BlockSpec((None, d, f_chunk), lambda s: (layer, 0, jnp.minimum(s, n_chunks - 1))),
        pl.BlockSpec((None, f_chunk, d), lambda s: (layer, jnp.minimum(s, n_chunks - 1), 0)),
    ]
    args = [x, norm_g, w_gate, w_up, w_down]
    if final:
        in_specs.append(_const_spec((1, d)))
        args.append(final_g.reshape(1, d))
    return pl.pallas_call(
        functools.partial(_ffn_body, n_chunks=n_chunks, final=final),
        grid=(n_chunks - 1 + t // tm,),
        in_specs=in_specs,
        out_specs=pl.BlockSpec((tm, d), tile),
        out_shape=jax.ShapeDtypeStruct((t, d), F32),
        scratch_shapes=[pltpu.VMEM((n_chunks, d, f_chunk), BF16), pltpu.VMEM((n_chunks, d, f_chunk), BF16),
                        pltpu.VMEM((n_chunks, f_chunk, d), BF16), pltpu.VMEM((tm, d), BF16),
                        pltpu.VMEM((tm, d), F32)],
        compiler_params=_params(0, 1),
        name="ffn_final" if final else "ffn",
    )(*args)


P_MIX = (0, 2, 3, 4)
P_LOC = (1, 5, 6, 7)

def _proj_body(x_ref, xprev_ref, g_ref, win_ref, mumat_ref, wcat_ref, w0_ref, w2_ref, a0_ref, a2_ref,
               g2_ref, pmix_ref, ploc_ref, lora_ref, wcat_s, wfold_s, *, seq, ranks):
    i = pl.program_id(0)
    tm = x_ref.shape[0]

    @pl.when(i == 0)
    def _():
        wcat = wcat_ref[...]
        wcat_s[...] = wcat.astype(BF16)
        wfold_s[...] = (mumat_ref[...] * wcat).astype(BF16)

    g = g_ref[...]
    h = _rmsnorm(x_ref[...], g)
    at_start = (i * tm) % seq == 0
    h_prev = _rmsnorm(xprev_ref[SUBLANES - 1:SUBLANES, :], g)
    h_prev = jnp.where(at_start, 0.0, h_prev)
    hb = h.astype(BF16)
    hxb = (_shift_rows(h, h_prev) - h).astype(BF16)

    def main_cols(lo, hi):
        for j in range(lo, hi):
            cols = slice(j * MIX_W, (j + 1) * MIX_W)
            out_ref, slot = ((pmix_ref, P_MIX.index(j)) if j in P_MIX else (ploc_ref, P_LOC.index(j)))
            out_ref[:, slot * MIX_W:(slot + 1) * MIX_W] = jnp.dot(hb, win_ref[:, cols],
                                                                  preferred_element_type=F32)

    main_cols(0, 4)
    t = (jnp.dot(hb, wcat_s[...], preferred_element_type=F32)
         + jnp.dot(hxb, wfold_s[...], preferred_element_type=F32))
    r_w, r_a, r_g = ranks
    t_w = jnp.tanh(t[:, 0:r_w])
    t_a = t[:, r_w:r_w + r_a]
    t_g = jax.nn.sigmoid(t[:, r_w + r_a:r_w + r_a + r_g])
    main_cols(4, 8)
    w_log = -_softplus(-(w0_ref[...] + _mm(t_w, w2_ref[...]))) - 0.5
    lora_ref[:, 0:MIX_W] = -jnp.exp(w_log)
    lora_ref[:, MIX_W:2 * MIX_W] = jax.nn.sigmoid(a0_ref[...] + _mm(t_a, a2_ref[...]))
    lora_ref[:, 2 * MIX_W:3 * MIX_W] = _mm(t_g, g2_ref[...])


def _proj(x, norm_g, w_in, layer, mu_mat, w_cat, w0, w2, a0, a2, g2, *, seq, tm=1024):
    t, d = x.shape
    tm = min(tm, seq)
    n_small = 8 * MIX_W
    blocks_per_tile = tm // SUBLANES
    ranks = (w2.shape[1], a2.shape[1], g2.shape[1])
    in_specs = [
        pl.BlockSpec((tm, d), lambda i: (i, 0)),
        pl.BlockSpec((SUBLANES, d), lambda i: (jnp.maximum(i * blocks_per_tile - 1, 0), 0)),
        _layer_spec(norm_g.shape, layer),
        _layer_spec(w_in.shape, layer, col_block=(n_small, 0)),
    ] + [_layer_spec(a.shape, layer) for a in (mu_mat, w_cat, w0, w2, a0, a2, g2)]
    return pl.pallas_call(
        functools.partial(_proj_body, seq=seq, ranks=ranks),
        grid=(t // tm,),
        in_specs=in_specs,
        out_specs=[pl.BlockSpec((tm, len(P_MIX) * MIX_W), lambda i: (i, 0)),
                   pl.BlockSpec((tm, len(P_LOC) * MIX_W), lambda i: (i, 0)),
                   pl.BlockSpec((tm, 3 * MIX_W), lambda i: (i, 0))],
        out_shape=[jax.ShapeDtypeStruct((t, len(P_MIX) * MIX_W), F32),
                   jax.ShapeDtypeStruct((t, len(P_LOC) * MIX_W), F32),
                   jax.ShapeDtypeStruct((t, 3 * MIX_W), F32)],
        scratch_shapes=[pltpu.VMEM((d, sum(ranks)), BF16), pltpu.VMEM((d, sum(ranks)), BF16)],
        compiler_params=_params(0, 1),
        name="proj",
    )(x, x, norm_g, w_in, mu_mat, w_cat, w0, w2, a0, a2, g2)


def _s5_body(u_ref, lre_ref, lim_ref, ldt_ref, bre_ref, bim_ref, cre_ref, cim_ref, d_ref, wglu_ref,
             o_ref, wre_s, wim_s, abar_s, xre_s, xim_s, hre_s, him_s, st_s, *, unroll):
    @pl.when(pl.program_id(0) == 0)
    def _():
        lr = lre_ref[...]
        li = lim_ref[...]
        dt = jnp.exp(ldt_ref[...])
        mag = jnp.exp(lr * dt)
        ar = mag * jnp.cos(li * dt)
        ai = mag * jnp.sin(li * dt)
        inv = 1.0 / (lr * lr + li * li)
        qr, qi = lr * inv, -li * inv
        coef_re = (ar - 1.0) * qr - ai * qi
        coef_im = (ar - 1.0) * qi + ai * qr
        wre_s[...] = (coef_re * bre_ref[...] - coef_im * bim_ref[...]).astype(BF16)
        wim_s[...] = (coef_re * bim_ref[...] + coef_im * bre_ref[...]).astype(BF16)
        abar_s[0:1, :] = ar
        abar_s[1:2, :] = ai
        st_s[...] = jnp.zeros_like(st_s)

    nb, tl = u_ref.shape[0], u_ref.shape[1]
    rows, n_state = nb * tl, wre_s.shape[1]
    u = u_ref[...].reshape(rows, MIX_W)
    ub = u.astype(BF16)
    blocks_per_row = tl // SUBLANES
    blocks = (nb * blocks_per_row, SUBLANES, n_state)
    xre_s[...] = jnp.dot(ub, wre_s[...], preferred_element_type=F32).reshape(blocks)
    xim_s[...] = jnp.dot(ub, wim_s[...], preferred_element_type=F32).reshape(blocks)
    ar = abar_s[0:1, :]
    ai = abar_s[1:2, :]

    def step(j, carry):
        carry = list(carry)
        for q in range(SUBLANES):
            for b in range(nb):
                blk = b * blocks_per_row + j
                hr, hi = carry[b]
                nr = ar * hr - ai * hi + xre_s[blk, q:q + 1, :]
                ni = ar * hi + ai * hr + xim_s[blk, q:q + 1, :]
                hre_s[blk, q:q + 1, :] = nr
                him_s[blk, q:q + 1, :] = ni
                carry[b] = (nr, ni)
        return tuple(carry)

    init = tuple((st_s[2 * b:2 * b + 1, :], st_s[2 * b + 1:2 * b + 2, :]) for b in range(nb))
    final = lax.fori_loop(0, blocks_per_row, step, init, unroll=unroll)
    for b in range(nb):
        st_s[2 * b:2 * b + 1, :] = final[b][0]
        st_s[2 * b + 1:2 * b + 2, :] = final[b][1]

    y = (_mm(hre_s[...].reshape(rows, n_state), cre_ref[...])
         - _mm(him_s[...].reshape(rows, n_state), cim_ref[...]) + d_ref[...] * u)
    gl = jax.nn.gelu(y, approximate=True)
    o_ref[...] = (gl * jax.nn.sigmoid(_mm(gl, wglu_ref[...]))).reshape(nb, tl, MIX_W)


def _s5_params(lam_re, lam_im, log_dt, b_re, b_im, c_re, c_im, d_skip, w_glu):
    depth = lam_re.shape[0]
    n_state = S5_GROUPS * S5_STATE
    eye = jnp.eye(S5_GROUPS, dtype=F32)
    row = lambda v: v.reshape(depth, 1, -1)
    b_bd = lambda b: jnp.einsum("lgph,gk->lghkp", b, eye).reshape(depth, MIX_W, n_state)
    c_bd = lambda c: jnp.einsum("lghp,gk->lgpkh", c, eye).reshape(depth, n_state, MIX_W)
    dt_rows = jnp.broadcast_to(log_dt[:, :, None], (depth, S5_GROUPS, S5_STATE))
    return (row(lam_re), row(lam_im), row(dt_rows), b_bd(b_re), b_bd(b_im),
            c_bd(c_re).astype(BF16), c_bd(c_im).astype(BF16), row(d_skip), w_glu.astype(BF16))


def _s5(p, params, layer, *, batch, seq, tl=512):
    n_state = S5_GROUPS * S5_STATE
    tl = min(tl, seq)
    in_specs = ([pl.BlockSpec((batch, tl, MIX_W), lambda i: (0, i, 0))]
                + [_layer_spec(a.shape, layer) for a in params])
    scan_buf = pltpu.VMEM((batch * tl // SUBLANES, SUBLANES, n_state), F32)
    out = pl.pallas_call(
        functools.partial(_s5_body, unroll=2),
        grid=(seq // tl,),
        in_specs=in_specs,
        out_specs=pl.BlockSpec((batch, tl, MIX_W), lambda i: (0, i, 0)),
        out_shape=jax.ShapeDtypeStruct((batch, seq, MIX_W), F32),
        scratch_shapes=[pltpu.VMEM((MIX_W, n_state), BF16), pltpu.VMEM((MIX_W, n_state), BF16),
                        pltpu.VMEM((2, n_state), F32)]
                       + [scan_buf] * 4 + [pltpu.VMEM((2 * batch, n_state), F32)],
        compiler_params=_params(0, 1),
        name="s5",
    )(p.reshape(batch, seq, p.shape[1]), *params)
    return out.reshape(batch * seq, MIX_W)


def _segment_ones(n, seg):
    r = lax.broadcasted_iota(jnp.int32, (n, n), 0) // seg
    c = lax.broadcasted_iota(jnp.int32, (n, n), 1) // seg
    return r == c


def _rwkv_body(pm_ref, pmprev_ref, lora_ref, mu_ref, kk_ref, ka_ref, rk_ref, lnw_ref, lnb_ref, o_ref,
               st_s):
    nb, tl = pm_ref.shape[0], pm_ref.shape[1]
    rows = nb * tl
    c = RW_CHUNK
    first = pl.program_id(0) == 0

    @pl.when(first)
    def _():
        st_s[...] = jnp.zeros_like(st_s)

    first_row = lax.broadcasted_iota(jnp.int32, (SUBLANES, MIX_W), 0) == 0

    def mixed(slot, mu):
        cols = slice(slot * MIX_W, (slot + 1) * MIX_W)
        cur = pm_ref[:, :, cols].reshape(rows, MIX_W)
        shifted = pltpu.roll(cur, 1, 0)
        pieces = []
        for b in range(nb):
            prev_row = jnp.where(first, 0.0, pmprev_ref[b, SUBLANES - 1:SUBLANES, cols])
            head = slice(b * tl, b * tl + SUBLANES)
            pieces += [jnp.where(first_row, prev_row, shifted[head]), shifted[b * tl + SUBLANES:(b + 1) * tl]]
        shifted = jnp.concatenate(pieces, axis=0)
        return cur + (shifted - cur) * mu

    r = mixed(1, mu_ref[0:1, :])
    k = mixed(2, mu_ref[1:2, :])
    v = mixed(3, mu_ref[2:3, :])
    lora = lora_ref[...].reshape(rows, 3 * MIX_W)
    logw = lora[:, 0:MIX_W]
    a = lora[:, MIX_W:2 * MIX_W]
    head_ones = _segment_ones(MIX_W, RW_HEAD).astype(BF16)
    kraw = k * kk_ref[...]
    kk = kraw * lax.rsqrt(jnp.maximum(_mm(kraw * kraw, head_ones), 1e-24))
    kmod = k * (1.0 + (a - 1.0) * ka_ref[...])

    bdot = functools.partial(jnp.dot, preferred_element_type=F32)
    n_chunks = rows // c
    chunks_per_row = tl // c

    tri = (lax.broadcasted_iota(jnp.int32, (c, c), 1)
           <= lax.broadcasted_iota(jnp.int32, (c, c), 0)).astype(BF16)
    w_hi = logw.astype(BF16)
    rem = logw - w_hi.astype(F32)
    w_mid = rem.astype(BF16)
    w_lo = (rem - w_mid.astype(F32)).astype(BF16)
    cum = jnp.concatenate(
        [bdot(tri, w_hi[n * c:(n + 1) * c]) + bdot(tri, w_mid[n * c:(n + 1) * c])
         + bdot(tri, w_lo[n * c:(n + 1) * c]) for n in range(n_chunks)], axis=0)
    tot = jnp.concatenate(
        [jnp.broadcast_to(cum[(n + 1) * c - 1:(n + 1) * c, :], (c, MIX_W)) for n in range(n_chunks)],
        axis=0)
    inv = jnp.exp(-cum)
    to_end = jnp.exp(tot - cum)
    r_dec = (r * jnp.exp(cum)).astype(BF16)
    a_dec = (-kk * jnp.exp(cum - logw)).astype(BF16)
    b_inv = (kk * a * inv).astype(BF16)
    k_inv = (kmod * inv).astype(BF16)
    b_end = (kk * a * to_end).astype(BF16)
    k_end = (kmod * to_end).astype(BF16)
    v_b = v.astype(BF16)
    p_end = [jnp.exp(cum[(n + 1) * c - 1:(n + 1) * c, :]) for n in range(n_chunks)]

    pair_w = 2 * RW_HEAD
    n_pairs = MIX_W // pair_w
    row_i = lax.broadcasted_iota(jnp.int32, (c, pair_w), 0)
    lane_i = lax.broadcasted_iota(jnp.int32, (c, pair_w), 1)
    left = lane_i < RW_HEAD
    strict = (lane_i % RW_HEAD) < row_i
    incl = (lane_i % RW_HEAD) <= row_i
    eye_fam = ((lane_i % RW_HEAD) == row_i).astype(F32)
    pr = lax.broadcasted_iota(jnp.int32, (pair_w, pair_w), 0)
    pc = lax.broadcasted_iota(jnp.int32, (pair_w, pair_w), 1)
    same_head = (pr // RW_HEAD) == (pc // RW_HEAD)
    eye_pair = (pr == pc).astype(F32)

    def halves(x):
        z = jnp.zeros_like(x)
        return jnp.where(left, x, z), jnp.where(left, z, x)

    def blockdiag(x):
        return jnp.concatenate(halves(x), axis=0)

    probs = [(q, n) for q in range(n_pairs) for n in range(n_chunks)]
    n_probs = len(probs)

    def piece(x, q, n):
        return x[n * c:(n + 1) * c, q * pair_w:(q + 1) * pair_w]

    a_d = [piece(a_dec, q, n) for q, n in probs]
    r_d = [piece(r_dec, q, n) for q, n in probs]
    vv = [piece(v_b, q, n) for q, n in probs]
    vv_bd = [blockdiag(x) for x in vv]
    g = [_mm_nt(jnp.concatenate([a_d[i], r_d[i]], axis=0),
                jnp.concatenate(halves(piece(b_inv, q, n)) + halves(piece(k_inv, q, n)), axis=0))
         for i, (q, n) in enumerate(probs)]
    a_ab = [jnp.where(strict, x[0:c, 0:pair_w], 0.0) for x in g]
    a_ak = [jnp.where(strict, x[0:c, pair_w:2 * pair_w], 0.0).astype(BF16) for x in g]
    lhs_o = [jnp.concatenate([r_d[i], jnp.where(incl, x[c:2 * c, 0:pair_w], 0.0).astype(BF16),
                              jnp.where(incl, x[c:2 * c, pair_w:2 * pair_w], 0.0).astype(BF16)], axis=1)
             for i, x in enumerate(g)]
    tinv = [eye_fam + x for x in a_ab]
    pw = [x.astype(BF16) for x in a_ab]
    pw = [bdot(x, blockdiag(x)).astype(BF16) for x in pw]
    akv = [bdot(a_ak[i], vv_bd[i]).astype(BF16) for i in range(n_probs)]
    n_steps = int(math.log2(c)) - 1
    for s in range(n_steps):
        if s + 1 < n_steps:
            both = [bdot(jnp.concatenate([tinv[i].astype(BF16), pw[i]], axis=0), blockdiag(pw[i]))
                    for i in range(n_probs)]
            tinv = [tinv[i] + both[i][0:c] for i in range(n_probs)]
            pw = [x[c:2 * c].astype(BF16) for x in both]
        else:
            tinv = [tinv[i] + bdot(tinv[i].astype(BF16), blockdiag(pw[i])) for i in range(n_probs)]
    w1u0 = [bdot(tinv[i].astype(BF16), jnp.concatenate([blockdiag(a_d[i]), blockdiag(akv[i])], axis=1))
            for i in range(n_probs)]
    tn = [_mm_tn(piece(b_end, q, n), w1u0[i]) for i, (q, n) in enumerate(probs)]
    kv = [_mm_tn(piece(k_end, q, n), vv[i]) for i, (q, n) in enumerate(probs)]
    trans = [(jnp.where(same_head, tn[i][:, 0:pair_w], 0.0)
              + eye_pair * p_end[n][:, q * pair_w:(q + 1) * pair_w]).astype(BF16)
             for i, (q, n) in enumerate(probs)]
    add = [jnp.where(same_head, tn[i][:, pair_w:2 * pair_w] + kv[i], 0.0) for i in range(n_probs)]
    w1 = [x[:, 0:pair_w].astype(BF16) for x in w1u0]
    u0 = [x[:, pair_w:2 * pair_w] for x in w1u0]

    st_at = {}
    chains = [(q, b) for q in range(n_pairs) for b in range(nb)]
    st = {qb: st_s[qb[0] * nb + qb[1]] for qb in chains}
    for step in range(chunks_per_row):
        for q, b in chains:
            i = q * n_chunks + b * chunks_per_row + step
            st_b = st[q, b].astype(BF16)
            st_at[i] = st_b
            st[q, b] = bdot(trans[i], st_b) + add[i]
    for q, b in chains:
        st_s[q * nb + b] = st[q, b]
    u = [(bdot(w1[i], st_at[i]) + u0[i]).astype(BF16) for i in range(n_probs)]
    o_p = [bdot(lhs_o[i], jnp.concatenate([st_at[i], blockdiag(u[i]), vv_bd[i]], axis=0))
           for i in range(n_probs)]
    o = jnp.concatenate(
        [jnp.concatenate(o_p[q * n_chunks:(q + 1) * n_chunks], axis=0) for q in range(n_pairs)], axis=1)

    inv_n = 1.0 / RW_HEAD
    mean = _mm(o, head_ones) * inv_n
    dlt = o - mean
    var = _mm(dlt * dlt, head_ones) * inv_n
    o_n = dlt * lax.rsqrt(var + RW_GN_EPS) * lnw_ref[...] + lnb_ref[...]
    bonus = _mm(r * kmod * rk_ref[...], head_ones) * v
    o_ref[...] = ((o_n + bonus) * lora[:, 2 * MIX_W:3 * MIX_W]).reshape(nb, tl, MIX_W)


def _rwkv(p, lora, params, layer, *, batch, seq, tl=256):
    tl = min(tl, seq)
    blocks_per_tile = tl // SUBLANES
    width = p.shape[1]
    p3 = p.reshape(batch, seq, width)
    lora3 = lora.reshape(batch, seq, lora.shape[1])
    in_specs = [pl.BlockSpec((batch, tl, width), lambda i: (0, i, 0)),
                pl.BlockSpec((batch, SUBLANES, width),
                             lambda i: (0, jnp.maximum(i * blocks_per_tile - 1, 0), 0)),
                pl.BlockSpec((batch, tl, 3 * MIX_W), lambda i: (0, i, 0))]
    in_specs += [_layer_spec(a.shape, layer) for a in params]
    pair_w = 2 * RW_HEAD
    out = pl.pallas_call(
        _rwkv_body,
        grid=(seq // tl,),
        in_specs=in_specs,
        out_specs=pl.BlockSpec((batch, tl, MIX_W), lambda i: (0, i, 0)),
        out_shape=jax.ShapeDtypeStruct((batch, seq, MIX_W), F32),
        scratch_shapes=[pltpu.VMEM((batch * MIX_W // pair_w, pair_w, pair_w), F32)],
        compiler_params=_params(0, 1),
        name="rwkv",
    )(p3, p3, lora3, *params)
    return out.reshape(batch * seq, MIX_W)


def _merge_body(x_ref, pl_ref, plprev_ref, ya_ref, yc_ref, g_ref, wgate01_ref, wgate23_ref, wbr_ref, wout_ref, poolw_ref, pscale_ref,
                convw_ref, o_ref, *, seq, sub):
    i = pl.program_id(0)
    tm, d = x_ref.shape
    at_start = (i * tm) % seq == 0
    lane = lax.broadcasted_iota(jnp.int32, (sub, MIX_W), 1)
    group = lane // (MIX_W // len(POOL_WINDOWS))
    row = lax.broadcasted_iota(jnp.int32, (sub, MIX_W), 0)

    for r0 in range(0, tm, sub):
        rows = slice(r0, r0 + sub)
        x = x_ref[rows, :]
        h = _rmsnorm(x, g_ref[...]).astype(BF16)

        ub, zc, bc, cc = (slice(k * MIX_W, (k + 1) * MIX_W) for k in range(4))
        u = pl_ref[rows, ub]
        if r0 == 0:
            tail = slice(POOL_HALO - SUBLANES, POOL_HALO)
            u_halo = jnp.where(at_start, 0.0, plprev_ref[:, ub])
            z_halo = jnp.where(at_start, 0.0, plprev_ref[tail, cc] * plprev_ref[tail, zc])
        else:
            u_halo = pl_ref[r0 - POOL_HALO:r0, ub]
            z_halo = pl_ref[r0 - SUBLANES:r0, cc] * pl_ref[r0 - SUBLANES:r0, zc]
        ext = jnp.concatenate([u_halo, u], axis=0)
        sums = {1: ext}
        span = 1
        while span < POOL_WINDOWS[-1]:
            sums[2 * span] = sums[span] + pltpu.roll(sums[span], span, 0)
            span *= 2
        win_sum = sums[POOL_WINDOWS[-1]][POOL_HALO:, :]
        win = jnp.full((sub, MIX_W), float(POOL_WINDOWS[-1]), F32)
        for gi in range(len(POOL_WINDOWS) - 2, -1, -1):
            win_sum = jnp.where(group == gi, sums[POOL_WINDOWS[gi]][POOL_HALO:, :], win_sum)
            win = jnp.where(group == gi, float(POOL_WINDOWS[gi]), win)
        pos = (i * tm + r0) % seq + row
        count = jnp.minimum((pos + 1).astype(F32), win)
        y_pool = _mm(win_sum / count - u, poolw_ref[...]) * pscale_ref[...]

        zext = jnp.concatenate([z_halo, pl_ref[rows, cc] * pl_ref[rows, zc]], axis=0)
        conv = (convw_ref[2:3, :] * zext
                + convw_ref[1:2, :] * pltpu.roll(zext, 1, 0)
                + convw_ref[0:1, :] * pltpu.roll(zext, 2, 0))
        y_conv = pl_ref[rows, bc] * conv[SUBLANES:, :]

        ys = [y.astype(BF16) for y in (ya_ref[rows, :], y_pool, yc_ref[rows, :], y_conv)]
        merged = []
        for j in range(d // MIX_W):
            cols = slice(j * MIX_W, (j + 1) * MIX_W)
            acc = None
            for gi in range(N_BRANCH):
                wgate_ref = (wgate01_ref, wgate23_ref)[gi // 2]
                gcols = slice((gi % 2) * d + j * MIX_W, (gi % 2) * d + (j + 1) * MIX_W)
                gate = jax.nn.sigmoid(jnp.dot(h, wgate_ref[:, gcols], preferred_element_type=F32))
                term = jnp.dot(ys[gi], wbr_ref[gi, :, cols], preferred_element_type=F32) * gate
                acc = term if acc is None else acc + term
            merged.append(acc.astype(BF16))
        o_ref[rows, :] = x + jnp.dot(jnp.concatenate(merged, axis=1), wout_ref[...],
                                     preferred_element_type=F32)


def _merge(x, p, y_s5, y_rwkv, norm_g, w_in, w_branch, w_out, layer, pool_bd, pool_scale, conv_w, *,
           seq, tm=512, sub=256):
    t, d = x.shape
    tm = min(tm, seq)
    gate_block = 2 * d
    first_gate_block = 8 * MIX_W // gate_block

    width = p.shape[1]
    halos_per_tile = tm // POOL_HALO
    in_specs = [
        pl.BlockSpec((tm, d), lambda i: (i, 0)),
        pl.BlockSpec((tm, width), lambda i: (i, 0)),
        pl.BlockSpec((POOL_HALO, width), lambda i: (jnp.maximum(i * halos_per_tile - 1, 0), 0)),
        pl.BlockSpec((tm, MIX_W), lambda i: (i, 0)),
        pl.BlockSpec((tm, MIX_W), lambda i: (i, 0)),
        _layer_spec(norm_g.shape, layer),
        _layer_spec(w_in.shape, layer, col_block=(gate_block, first_gate_block)),
        _layer_spec(w_in.shape, layer, col_block=(gate_block, first_gate_block + 1)),
        _layer_spec(w_branch.shape, layer), _layer_spec(w_out.shape, layer),
        _layer_spec(pool_bd.shape, layer), _layer_spec(pool_scale.shape, layer),
        _layer_spec(conv_w.shape, layer),
    ]
    return pl.pallas_call(
        functools.partial(_merge_body, seq=seq, sub=min(sub, tm)),
        grid=(t // tm,),
        in_specs=in_specs,
        out_specs=pl.BlockSpec((tm, d), lambda i: (i, 0)),
        out_shape=jax.ShapeDtypeStruct((t, d), F32),
        compiler_params=_params(1),
        name="merge",
    )(x, p, p, y_s5, y_rwkv, norm_g, w_in, w_in, w_branch, w_out,
      pool_bd, pool_scale, conv_w)


def kernel(x, ffn1_norm, ffn1_w_gate, ffn1_w_up, ffn1_w_down, mix_norm, w_in, s5_lambda_re, s5_lambda_im, s5_log_dt, s5_b_re, s5_b_im, s5_c_re, s5_c_im, s5_d, s5_w_glu, pool_w, pool_scale, rwkv_mu_rkv, rwkv_mu_wag, rwkv_w0, rwkv_w1, rwkv_w2, rwkv_a0, rwkv_a1, rwkv_a2, rwkv_g1, rwkv_g2, rwkv_k_k, rwkv_k_a, rwkv_r_k, rwkv_ln_w, rwkv_ln_b, conv_w, w_branch, w_out, ffn2_norm, ffn2_w_gate, ffn2_w_up, ffn2_w_down, final_norm):
    batch, seq, d = x.shape
    depth = w_in.shape[0]
    bf = lambda w: w.astype(BF16)
    ffn1 = (ffn1_w_gate, ffn1_w_up, ffn1_w_down)
    ffn2 = (ffn2_w_gate, ffn2_w_up, ffn2_w_down)
    w_in_b, w_branch_b, w_out_b = bf(w_in), bf(w_branch), bf(w_out)
    vec = lambda a: a.reshape(depth, 1, -1)
    ffn1_g, ffn2_g, mix_g = vec(ffn1_norm), vec(ffn2_norm), vec(mix_norm)
    ranks = (rwkv_w1.shape[2], rwkv_a1.shape[2], rwkv_g1.shape[2])
    w_cat = jnp.concatenate([rwkv_w1, rwkv_a1, rwkv_g1], axis=2)
    mu_mat = jnp.concatenate(
        [jnp.broadcast_to(rwkv_mu_wag[:, j, :, None], (depth, d, rk)) for j, rk in enumerate(ranks)],
        axis=2)
    proj_params = (mu_mat, w_cat, vec(rwkv_w0), rwkv_w2, vec(rwkv_a0), rwkv_a2, rwkv_g2)
    s5_params = _s5_params(s5_lambda_re, s5_lambda_im, s5_log_dt, s5_b_re, s5_b_im, s5_c_re, s5_c_im,
                           s5_d, s5_w_glu)
    rwkv_params = (rwkv_mu_rkv, vec(rwkv_k_k), vec(rwkv_k_a), vec(rwkv_r_k), vec(rwkv_ln_w),
                   vec(rwkv_ln_b))
    pool_groups = len(POOL_WINDOWS)
    pool_bd = jnp.einsum("lgcd,gk->lgckd", pool_w, jnp.eye(pool_groups, dtype=F32)).reshape(
        depth, MIX_W, MIX_W)
    pool_s = vec(pool_scale)
    xf = x.reshape(batch * seq, d)
    for l in range(depth):
        xf = _ffn(xf, ffn1_g, *ffn1, l)
        p_mix, p_loc, lora = _proj(xf, mix_g, w_in_b, l, *proj_params, seq=seq)
        y_s5 = _s5(p_mix, s5_params, l, batch=batch, seq=seq)
        y_rwkv = _rwkv(p_mix, lora, rwkv_params, l, batch=batch, seq=seq)
        xf = _merge(xf, p_loc, y_s5, y_rwkv, mix_g, w_in_b, w_branch_b, w_out_b, l,
                    pool_bd, pool_s, conv_w, seq=seq)
        xf = _ffn(xf, ffn2_g, *ffn2, l, final_norm if l == depth - 1 else None)
    return xf.reshape(batch, seq, d)
```
